```python
import math
import jax
import jax.numpy as jnp
from jax import lax
import numpy as np

D_MODEL = 1024
BATCH = 16
SEQ = 2048
DEPTH = 1
DEC_BATCH = 128
DEC_SEQ = 1
PAST_LEN = 8192
PAGE_SIZE = 128

HEAD_DIM = 64
N_HEADS = 8
N_KV = 2
GROUP = N_HEADS // N_KV
ATT_WIDTH = N_HEADS * HEAD_DIM
KV_WIDTH = N_KV * HEAD_DIM
CONV_DIM = D_MODEL - ATT_WIDTH
CONV_GROUPS = 8
CONV_WIDTH = 3
MIX_WIDTH = ATT_WIDTH + CONV_DIM
PROJ_WIDTH = ATT_WIDTH + 6 * KV_WIDTH + 3 * N_HEADS + 3 * CONV_DIM
CMP_BLOCK = 32
SEL_BLOCK = 64
TOP_N = 8
WINDOW = 512
SEL_QCHUNK = 64
WIN_QBLOCK = 128
RP_BUCKETS = 32
RP_MAX_DIST = 128
D_FF = 4 * D_MODEL
PLE_DIM = 256
SCALE = HEAD_DIM ** -0.5
NEG = -1e30
FORCE = 1e4
EPS = 1e-6

kernel_name = 'nsa_shortconv_hybrid_decode_step'


def rmsnorm(x, g):
    xf = x.astype(jnp.float32)
    y = xf * lax.rsqrt(jnp.mean(xf * xf, axis=-1, keepdims=True) + EPS)
    return (y * g.astype(jnp.float32)).astype(x.dtype)


def rel_bucket(dist):
    n = jnp.maximum(dist, 0)
    max_exact = RP_BUCKETS // 2
    nf = jnp.maximum(n, 1).astype(jnp.float32)
    large = max_exact + (jnp.log(nf / max_exact) / math.log(RP_MAX_DIST / max_exact)
                         * (RP_BUCKETS - max_exact)).astype(jnp.int32)
    large = jnp.minimum(large, RP_BUCKETS - 1)
    return jnp.where(n < max_exact, n, large)


def rel_bias_lookup(rel_bias, dist):
    return rel_bias[rel_bucket(dist)].reshape(dist.shape + (N_KV, GROUP)).astype(jnp.float32)


def pad_seq(x):
    extra = (-x.shape[1]) % SEL_BLOCK
    return jnp.pad(x, ((0, 0), (0, extra)) + ((0, 0),) * (x.ndim - 2))


def join_pad(past, new):
    extra = (-(past.shape[1] + new.shape[1])) % SEL_BLOCK
    zeros = jnp.zeros((new.shape[0], extra) + new.shape[2:], new.dtype)
    return jnp.concatenate([past.astype(new.dtype), new, zeros], axis=1)


def mixer_inputs(h, norm_mix, w_in, q_norm, k_norm):
    B, T, _ = h.shape
    sizes = [ATT_WIDTH] + [KV_WIDTH] * 6 + [3 * N_HEADS] + [CONV_DIM] * 3
    points = [int(v) for v in np.cumsum(sizes)[:-1]]
    z = rmsnorm(h, norm_mix) @ w_in
    q, kc, vc, ks, vs, kw, vw, g, cb, cc, ch = jnp.split(z, points, axis=-1)
    heads = lambda t: t.reshape(B, T, N_KV, HEAD_DIM)
    q = rmsnorm(q.reshape(B, T, N_KV, GROUP, HEAD_DIM), q_norm)
    ks = rmsnorm(heads(ks), k_norm[1])
    kw = rmsnorm(heads(kw), k_norm[2])
    gates = jax.nn.sigmoid(g.astype(jnp.float32)).reshape(B, T, 3, N_KV, GROUP).astype(h.dtype)
    return q, heads(kc), heads(vc), ks, heads(vs), kw, heads(vw), gates, cb, cc * ch


def compress(x_raw, pe, w):
    B, Tk = x_raw.shape[:2]
    xb = x_raw.reshape(B, Tk // CMP_BLOCK, CMP_BLOCK, N_KV, HEAD_DIM) + pe[:, None, :]
    return jnp.einsum('bnlgd,lde->bnge', xb, w)


def cmp_branch(q, kc_raw, vc_raw, q_pos, pe, w, k_norm_c, rel_bias):
    k = rmsnorm(compress(kc_raw, pe[0], w[0]), k_norm_c)
    v = compress(vc_raw, pe[1], w[1])
    nc = k.shape[1]
    blk_end = jnp.arange(nc) * CMP_BLOCK + CMP_BLOCK - 1
    dist = q_pos[:, None] - blk_end[None, :]
    valid = dist >= 0
    bias = jnp.transpose(rel_bias_lookup(rel_bias, dist), (2, 3, 0, 1))
    s = jnp.einsum('btgrd,bngd->bgrtn', q, k, preferred_element_type=jnp.float32) * SCALE + bias
    s = jnp.where(valid, s, NEG)
    p = jax.nn.softmax(s, axis=-1) * valid
    o = jnp.einsum('bgrtn,bngd->btgrd', p.astype(v.dtype), v)
    return o, p


def select_blocks(p, q_pos):
    B, G, R, Tq, nc = p.shape
    ratio = SEL_BLOCK // CMP_BLOCK
    ns = nc // ratio
    imp = p.reshape(B, G, R, Tq, ns, ratio).sum(axis=(2, 5))
    cur = (q_pos // SEL_BLOCK)[:, None]
    j = jnp.arange(ns)[None, :]
    forced = (j == 0) | (j == cur) | (j == cur - 1)
    score = jnp.where(forced, FORCE, jnp.where(j <= cur, imp, NEG))
    _, idx = lax.top_k(score, min(TOP_N, ns))
    return idx


def sel_attend(q, q_pos, idx, ksb, vsb, rel_bias):
    B, tc = q.shape[:2]
    n = idx.shape[-1]
    bi = jnp.arange(B)[:, None, None, None]
    gi = jnp.arange(N_KV)[None, :, None, None]
    kg = ksb[bi, gi, idx].reshape(B, N_KV, tc, n * SEL_BLOCK, HEAD_DIM)
    vg = vsb[bi, gi, idx].reshape(B, N_KV, tc, n * SEL_BLOCK, HEAD_DIM)
    kpos = idx[..., None] * SEL_BLOCK + jnp.arange(SEL_BLOCK)
    dist = q_pos[None, None, :, None, None] - kpos
    rb = rel_bias.reshape(RP_BUCKETS, N_KV, GROUP)
    bias = rb[rel_bucket(dist), jnp.arange(N_KV)[None, :, None, None, None]].astype(jnp.float32)
    bias = jnp.moveaxis(bias.reshape(B, N_KV, tc, n * SEL_BLOCK, GROUP), -1, 2)
    valid = (dist >= 0).reshape(B, N_KV, 1, tc, n * SEL_BLOCK)
    s = jnp.einsum('btgrd,bgtkd->bgrtk', q, kg, preferred_element_type=jnp.float32) * SCALE
    p = jax.nn.softmax(jnp.where(valid, s + bias, NEG), axis=-1)
    return jnp.einsum('bgrtk,bgtkd->btgrd', p.astype(vg.dtype), vg)


def sel_branch(q, ks, vs, q_pos, idx, rel_bias):
    B, Tq = q.shape[:2]
    ns = ks.shape[1] // SEL_BLOCK
    ksb = jnp.moveaxis(ks.reshape(B, ns, SEL_BLOCK, N_KV, HEAD_DIM), 3, 1)
    vsb = jnp.moveaxis(vs.reshape(B, ns, SEL_BLOCK, N_KV, HEAD_DIM), 3, 1)
    tc = math.gcd(Tq, SEL_QCHUNK)
    nch = Tq // tc
    qc = jnp.moveaxis(q.reshape(B, nch, tc, N_KV, GROUP, HEAD_DIM), 1, 0)
    ic = jnp.moveaxis(idx.reshape(B, N_KV, nch, tc, idx.shape[-1]), 2, 0)
    pc = q_pos.reshape(nch, tc)
    o = lax.map(lambda a: sel_attend(a[0], a[1], a[2], ksb, vsb, rel_bias), (qc, pc, ic))
    return jnp.moveaxis(o, 0, 1).reshape(B, Tq, N_KV, GROUP, HEAD_DIM)


def win_attend(q, q_pos, k, v, k_pos, rel_bias):
    dist = q_pos[:, None] - k_pos[None, :]
    valid = (dist >= 0) & (dist < WINDOW) & (k_pos[None, :] >= 0)
    bias = jnp.transpose(rel_bias_lookup(rel_bias, dist), (2, 3, 0, 1))
    s = jnp.einsum('btgrd,bkgd->bgrtk', q, k, preferred_element_type=jnp.float32) * SCALE + bias
    p = jax.nn.softmax(jnp.where(valid, s, NEG), axis=-1)
    return jnp.einsum('bgrtk,bkgd->btgrd', p.astype(v.dtype), v)


def win_branch_full(q, kw, vw, rel_bias):
    B, T = q.shape[:2]
    tb = math.gcd(T, WIN_QBLOCK)
    nb = T // tb
    pad = ((0, 0), (WINDOW, 0), (0, 0), (0, 0))
    kp = jnp.pad(kw, pad)
    vp = jnp.pad(vw, pad)
    qb = jnp.moveaxis(q.reshape(B, nb, tb, N_KV, GROUP, HEAD_DIM), 1, 0)
    starts = jnp.arange(nb) * tb

    def blk(a):
        qi, s0 = a
        k = lax.dynamic_slice_in_dim(kp, s0, WINDOW + tb, axis=1)
        v = lax.dynamic_slice_in_dim(vp, s0, WINDOW + tb, axis=1)
        k_pos = s0 - WINDOW + jnp.arange(WINDOW + tb)
        return win_attend(qi, s0 + jnp.arange(tb), k, v, k_pos, rel_bias)

    o = lax.map(blk, (qb, starts))
    return jnp.moveaxis(o, 0, 1).reshape(B, T, N_KV, GROUP, HEAD_DIM)


def short_conv(u, prev, w):
    T = u.shape[1]
    uu = jnp.concatenate([prev.astype(u.dtype), u], axis=1)
    y = w[0] * uu[:, 0:T] + w[1] * uu[:, 1:T + 1] + w[2] * uu[:, 2:T + 2]
    return y, uu[:, -(CONV_WIDTH - 1):]


def layer_out(h, gates, o_c, o_s, o_w, y_conv, p_l, out_norm, w_out, norm_mlp, w_up, w_down,
              norm_ple, w_ple_gate, w_ple_proj):
    B, T, _ = h.shape
    o_att = (gates[:, :, 0, ..., None] * o_c + gates[:, :, 1, ..., None] * o_s
             + gates[:, :, 2, ..., None] * o_w).reshape(B, T, ATT_WIDTH)
    mix = jnp.concatenate([rmsnorm(o_att, out_norm[:ATT_WIDTH]),
                           rmsnorm(y_conv, out_norm[ATT_WIDTH:])], axis=-1)
    h = h + mix @ w_out
    h = h + jnp.square(jax.nn.relu(rmsnorm(h, norm_mlp) @ w_up)) @ w_down
    h = h + jax.nn.sigmoid(rmsnorm(h, norm_ple) @ w_ple_gate) * (p_l @ w_ple_proj)
    return h


def setup_inputs(seed: int = 0) -> dict:
    key = jax.random.key(seed)
    k = jax.random.split(key, 26)
    n_pages = PAST_LEN // PAGE_SIZE
    n_phys = (DEC_BATCH * n_pages * 5) // 4
    wlen = min(WINDOW, PAST_LEN)
    nrm = lambda kk, shape, scale=1.0: jax.random.normal(kk, shape, jnp.float32) * scale
    gain = lambda kk, shape: 1.0 + 0.02 * jax.random.normal(kk, shape, jnp.float32)
    page_table = jax.random.permutation(k[8], n_phys)[: DEC_BATCH * n_pages].reshape(
        DEC_BATCH, n_pages).astype(jnp.int32)
    return {
        'x_prompt': nrm(k[0], (BATCH, SEQ, D_MODEL)),
        'x_sample': nrm(k[1], (DEC_BATCH, DEC_SEQ, D_MODEL)),
        'p_prompt': nrm(k[2], (DEPTH, BATCH, SEQ, PLE_DIM)),
        'p_sample': nrm(k[3], (DEPTH, DEC_BATCH, DEC_SEQ, PLE_DIM)),
        'cache_cmp_kv': nrm(k[4], (DEPTH, n_phys, PAGE_SIZE, 2, N_KV, HEAD_DIM)),
        'cache_sel_kv': nrm(k[5], (DEPTH, n_phys, PAGE_SIZE, 2, N_KV, HEAD_DIM)),
        'state_win_kv': nrm(k[6], (DEPTH, DEC_BATCH, wlen, 2, N_KV, HEAD_DIM)),
        'state_conv': nrm(k[7], (DEPTH, DEC_BATCH, CONV_WIDTH - 1, CONV_DIM)),
        'page_table': page_table,
        'rel_bias': nrm(k[9], (RP_BUCKETS, N_HEADS), 0.5),
        'norm_mix': gain(k[10], (DEPTH, D_MODEL)),
        'w_in': nrm(k[11], (DEPTH, D_MODEL, PROJ_WIDTH), D_MODEL ** -0.5),
        'q_norm': gain(k[12], (DEPTH, HEAD_DIM)),
        'k_norm': gain(k[13], (DEPTH, 3, HEAD_DIM)),
        'cmp_pe': nrm(k[14], (DEPTH, 2, CMP_BLOCK, HEAD_DIM), 0.1),
        'w_cmp': nrm(k[15], (DEPTH, 2, CMP_BLOCK, HEAD_DIM, HEAD_DIM), (CMP_BLOCK * HEAD_DIM) ** -0.5),
        'conv_w': nrm(k[16], (DEPTH, CONV_WIDTH, CONV_DIM), CONV_WIDTH ** -0.5),
        'out_norm': gain(k[17], (DEPTH, MIX_WIDTH)),
        'w_out': nrm(k[18], (DEPTH, MIX_WIDTH, D_MODEL), MIX_WIDTH ** -0.5),
        'norm_mlp': gain(k[19], (DEPTH, D_MODEL)),
        'w_up': nrm(k[20], (DEPTH, D_MODEL, D_FF), D_MODEL ** -0.5),
        'w_down': nrm(k[21], (DEPTH, D_FF, D_MODEL), D_FF ** -0.5),
        'norm_ple': gain(k[22], (DEPTH, D_MODEL)),
        'w_ple_gate': nrm(k[23], (DEPTH, D_MODEL, D_MODEL), D_MODEL ** -0.5),
        'w_ple_proj': nrm(k[24], (DEPTH, PLE_DIM, D_MODEL), PLE_DIM ** -0.5),
    }


def reference(x_prompt, x_sample, p_prompt, p_sample, cache_cmp_kv, cache_sel_kv, state_win_kv,
              state_conv, page_table, rel_bias, norm_mix, w_in, q_norm, k_norm, cmp_pe, w_cmp,
              conv_w, out_norm, w_out, norm_mlp, w_up, w_down, norm_ple, w_ple_gate, w_ple_proj):
    h_p, h_s = x_prompt, x_sample
    bp, tp = h_p.shape[:2]
    bs, ts = h_s.shape[:2]
    pos_p = jnp.arange(tp)
    pos_s = PAST_LEN + jnp.arange(ts)
    pc_l, ps_l, pw_l, pv_l, sc_l, ss_l, sw_l, sv_l = [], [], [], [], [], [], [], []
    for l in range(DEPTH):
        lw = (norm_mix[l], w_in[l], q_norm[l], k_norm[l])
        tail = (out_norm[l], w_out[l], norm_mlp[l], w_up[l], w_down[l], norm_ple[l],
                w_ple_gate[l], w_ple_proj[l])
        q, kc, vc, ks, vs, kw, vw, gates, cb, u = mixer_inputs(h_p, *lw)
        o_c, p_c = cmp_branch(q, pad_seq(kc), pad_seq(vc), pos_p, cmp_pe[l], w_cmp[l], k_norm[l, 0], rel_bias)
        idx = select_blocks(p_c, pos_p)
        o_s = sel_branch(q, pad_seq(ks), pad_seq(vs), pos_p, idx, rel_bias)
        o_w = win_branch_full(q, kw, vw, rel_bias)
        y_c, conv_new = short_conv(u, jnp.zeros((bp, CONV_WIDTH - 1, CONV_DIM), u.dtype), conv_w[l])
        h_p = layer_out(h_p, gates, o_c, o_s, o_w, cb * y_c, p_prompt[l], *tail)
        pc_l.append(jnp.stack([kc, vc], axis=2))
        ps_l.append(jnp.stack([ks, vs], axis=2))
        pw_l.append(jnp.stack([kw, vw], axis=2)[:, -min(WINDOW, tp):])
        pv_l.append(conv_new)
        q, kc, vc, ks, vs, kw, vw, gates, cb, u = mixer_inputs(h_s, *lw)
        cc_l, cs_l = cache_cmp_kv[l], cache_sel_kv[l]
        past_kc = cc_l[page_table, :, 0].reshape(bs, -1, N_KV, HEAD_DIM)
        past_vc = cc_l[page_table, :, 1].reshape(bs, -1, N_KV, HEAD_DIM)
        past_ks = cs_l[page_table, :, 0].reshape(bs, -1, N_KV, HEAD_DIM)
        past_vs = cs_l[page_table, :, 1].reshape(bs, -1, N_KV, HEAD_DIM)
        o_c, p_c = cmp_branch(q, join_pad(past_kc, kc), join_pad(past_vc, vc), pos_s, cmp_pe[l], w_cmp[l],
                              k_norm[l, 0], rel_bias)
        idx = select_blocks(p_c, pos_s)
        o_s = sel_branch(q, join_pad(past_ks, ks), join_pad(past_vs, vs), pos_s, idx, rel_bias)
        win = state_win_kv[l]
        wlen = win.shape[1]
        kw_all = jnp.concatenate([win[:, :, 0].astype(kw.dtype), kw], axis=1)
        vw_all = jnp.concatenate([win[:, :, 1].astype(vw.dtype), vw], axis=1)
        k_pos = PAST_LEN - wlen + jnp.arange(wlen + ts)
        o_w = win_attend(q, pos_s, kw_all, vw_all, k_pos, rel_bias)
        y_c, conv_new = short_conv(u, state_conv[l], conv_w[l])
        h_s = layer_out(h_s, gates, o_c, o_s, o_w, cb * y_c, p_sample[l], *tail)
        sc_l.append(jnp.stack([kc, vc], axis=2))
        ss_l.append(jnp.stack([ks, vs], axis=2))
        sw_l.append(jnp.stack([kw_all, vw_all], axis=2)[:, -min(WINDOW, PAST_LEN + ts):])
        sv_l.append(conv_new)
    prompt_cmp_kv = jnp.stack(pc_l)
    prompt_sel_kv = jnp.stack(ps_l)
    prompt_win_kv = jnp.stack(pw_l)
    prompt_conv = jnp.stack(pv_l)
    sample_cmp_kv = jnp.stack(sc_l)
    sample_sel_kv = jnp.stack(ss_l)
    sample_win_kv = jnp.stack(sw_l)
    sample_conv = jnp.stack(sv_l)
    return (h_p, h_s, prompt_cmp_kv, prompt_sel_kv, prompt_win_kv, prompt_conv,
            sample_cmp_kv, sample_sel_kv, sample_win_kv, sample_conv)
```

```python
import functools
import math

import jax
import jax.numpy as jnp
import numpy as np
from jax import lax
from jax.experimental import pallas as pl
from jax.experimental.pallas import tpu as pltpu

D_MODEL = 1024
HEAD_DIM = 64
N_HEADS = 8
N_KV = 2
GROUP = N_HEADS // N_KV
ATT_WIDTH = N_HEADS * HEAD_DIM
KV_WIDTH = N_KV * HEAD_DIM
CONV_DIM = D_MODEL - ATT_WIDTH
PAGE_SIZE = 128
CMP_BLOCK = 32
SEL_BLOCK = 64
TOP_N = 8
WINDOW = 512
RP_BUCKETS = 32
RP_MAX_DIST = 128
D_FF = 4 * D_MODEL
PLE_DIM = 256
SCALE = HEAD_DIM ** -0.5
NEG = -1e30
EPS = 1e-6

LANES = 128
QBLK = 128
VMEM_LIMIT = 56 * 1024 * 1024

_BF = jnp.bfloat16
_F32 = jnp.float32


def _dot(a, b):
    return jnp.dot(a, b, preferred_element_type=_F32)


def _dot_nt(a, b):
    return lax.dot_general(a, b, (((1,), (1,)), ((), ())), preferred_element_type=_F32)


def _rms_rows(x, gain):
    return x * lax.rsqrt(jnp.mean(x * x, axis=-1, keepdims=True) + EPS) * gain


def _group_rms(z, gmat, gain):
    ssq = _dot((z * z).astype(_BF), gmat) * (1.0 / HEAD_DIM)
    return z * lax.rsqrt(ssq + EPS) * gain


def _inproj_body(x_ref, nm_ref, wq_ref, wkv_ref, wg_ref, wc_ref, qg_ref, kg_ref, g512_ref, g128_ref,
                 q_ref, cmp_ref, sel_ref, win_ref, gate_ref, cb_ref, u_ref):
    a = _rms_rows(x_ref[...], nm_ref[...]).astype(_BF)
    zq = _dot(a, wq_ref[...])
    q_ref[...] = (_group_rms(zq, g512_ref[...], qg_ref[...]) * SCALE).astype(_BF)
    zkv = _dot(a, wkv_ref[...])
    cmp_ref[...] = zkv[:, 0:256]
    g128 = g128_ref[...]
    sel_ref[:, 0:128] = _group_rms(zkv[:, 256:384], g128, kg_ref[0:1, :])
    sel_ref[:, 128:256] = zkv[:, 384:512]
    win_ref[:, 0:128] = _group_rms(zkv[:, 512:640], g128, kg_ref[1:2, :])
    win_ref[:, 128:256] = zkv[:, 640:768]
    gate_ref[...] = jax.nn.sigmoid(_dot(a, wg_ref[...]))
    zc = _dot(a, wc_ref[...])
    cb_ref[...] = zc[:, 0:512]
    u_ref[...] = zc[:, 512:1024] * zc[:, 1024:1536]


def _inproj(x, w):
    n = x.shape[0]
    tm = min(256, n)
    row = lambda c: pl.BlockSpec((tm, c), lambda i: (i, 0))
    full = lambda a: pl.BlockSpec(a.shape, lambda i: (0,) * a.ndim)
    consts = (w['norm_mix'], w['wq'], w['wkv'], w['wg'], w['wc'], w['q_gain'], w['k_gain'], w['g512'], w['g128'])
    return pl.pallas_call(
        _inproj_body,
        grid=(n // tm,),
        in_specs=[row(D_MODEL)] + [full(c) for c in consts],
        out_specs=[row(512), row(256), row(256), row(256), row(128), row(512), row(512)],
        out_shape=[jax.ShapeDtypeStruct((n, 512), _BF)] + [jax.ShapeDtypeStruct((n, 256), _F32)] * 3
        + [jax.ShapeDtypeStruct((n, 128), _F32)] + [jax.ShapeDtypeStruct((n, 512), _F32)] * 2,
        compiler_params=pltpu.CompilerParams(dimension_semantics=("arbitrary",), vmem_limit_bytes=VMEM_LIMIT),
        name="inproj",
    )(x, *consts)


FF_CHUNK = 1024


def _tail_body(halo, h_ref, o_ref, cb_ref, u_ref, up_ref, p_ref, cw_ref, on_ref, nmlp_ref, nple_ref,
               wout_hbm, wup_hbm, wdn_hbm, wgate_hbm, wproj_hbm, y_ref,
               uext_ref, wout_ref, wup_ref, wdn_ref, wgate_ref, wproj_ref, wsem):
    @pl.when(pl.program_id(0) == 0)
    def _load_weights():
        copies = [pltpu.make_async_copy(src, dst, wsem.at[i]) for i, (src, dst) in enumerate(
            ((wout_hbm, wout_ref), (wup_hbm, wup_ref), (wdn_hbm, wdn_ref), (wgate_hbm, wgate_ref),
             (wproj_hbm, wproj_ref)))]
        for c in copies:
            c.start()
        for c in copies:
            c.wait()

    tm = h_ref.shape[0]
    u = u_ref[...]
    if halo:
        first = (pl.program_id(0) % halo) == 0
        prev = jnp.where(first, 0.0, up_ref[...])
        uext_ref[0:8, :] = prev
        uext_ref[8:tm + 8, :] = u
        u2 = uext_ref[6:tm + 6, :]
        u1 = uext_ref[7:tm + 7, :]
    else:
        u2 = up_ref[0]
        u1 = up_ref[1]
    yc = cw_ref[0:1, :] * u2 + cw_ref[1:2, :] * u1 + cw_ref[2:3, :] * u
    mix_a = _rms_rows(o_ref[...], on_ref[:, 0:ATT_WIDTH]).astype(_BF)
    mix_c = _rms_rows(cb_ref[...] * yc, on_ref[:, ATT_WIDTH:]).astype(_BF)
    h = h_ref[...] + _dot(mix_a, wout_ref[0:ATT_WIDTH, :]) + _dot(mix_c, wout_ref[ATT_WIDTH:, :])
    a = _rms_rows(h, nmlp_ref[...]).astype(_BF)
    y_ref[...] = h
    for c in range(D_FF // FF_CHUNK):
        t = jnp.maximum(_dot(a, wup_ref[:, c * FF_CHUNK:(c + 1) * FF_CHUNK]), 0.0)
        y_ref[...] += _dot((t * t).astype(_BF), wdn_ref[c * FF_CHUNK:(c + 1) * FF_CHUNK, :])
    h = y_ref[...]
    a = _rms_rows(h, nple_ref[...]).astype(_BF)
    gate = jax.nn.sigmoid(_dot(a, wgate_ref[...]))
    y_ref[...] = h + gate * _dot(p_ref[...].astype(_BF), wproj_ref[...])


def _tail(h, o_att, cb, u, u_prev, p, w, seq_len):
    n = h.shape[0]
    tm = min(256, n)
    row = lambda c: pl.BlockSpec((tm, c), lambda i: (i, 0))
    const = lambda a: pl.BlockSpec(a.shape, lambda i: (0,) * a.ndim)
    if seq_len is not None:
        halo = seq_len // tm
        up_spec = pl.BlockSpec((8, CONV_DIM), lambda i: (jnp.maximum(i * (tm // 8) - 1, 0), 0))
        up = u
    else:
        halo = 0
        up_spec = pl.BlockSpec((2, tm, CONV_DIM), lambda i: (0, i, 0))
        up = u_prev
    consts = (w['conv_w'], w['out_norm'], w['norm_mlp'], w['norm_ple'])
    mats = (w['w_out'], w['w_up'], w['w_down'], w['w_ple_gate'], w['w_ple_proj'])
    return pl.pallas_call(
        functools.partial(_tail_body, halo),
        grid=(n // tm,),
        in_specs=[row(D_MODEL), row(ATT_WIDTH), row(CONV_DIM), row(CONV_DIM), up_spec, row(PLE_DIM)]
        + [const(c) for c in consts] + [pl.BlockSpec(memory_space=pl.ANY)] * len(mats),
        out_specs=row(D_MODEL),
        out_shape=jax.ShapeDtypeStruct((n, D_MODEL), _F32),
        scratch_shapes=[pltpu.VMEM((tm + 8, CONV_DIM), _F32)] + [pltpu.VMEM(m.shape, _BF) for m in mats]
        + [pltpu.SemaphoreType.DMA((len(mats),))],
        compiler_params=pltpu.CompilerParams(dimension_semantics=("arbitrary",), vmem_limit_bytes=VMEM_LIMIT),
        name="tail",
    )(h, o_att, cb, u, up, p, *consts, *mats)


def _split_heads(x, lane_lo):
    xr = pltpu.roll(x, HEAD_DIM, axis=1)
    zero = jnp.zeros_like(x)
    a = (jnp.where(lane_lo, x, zero), jnp.where(lane_lo, xr, zero))
    b = (jnp.where(lane_lo, zero, xr), jnp.where(lane_lo, zero, x))
    return a, b


def _compress_rows(load, ns, bd_ref, pe_ref):
    acc = [jnp.zeros((2 * ns, LANES), _F32), jnp.zeros((2 * ns, LANES), _F32)]
    for l in range(CMP_BLOCK):
        for plane in range(2):
            x = jnp.concatenate([load(2 * l + plane), load(2 * (CMP_BLOCK + l) + plane)], axis=0)
            x = x + pe_ref[l, plane:plane + 1, :]
            acc[plane] = acc[plane] + _dot(x.astype(_BF), bd_ref[l, plane])
    return acc


def _compress_prompt_body(slab_ref, bd_ref, pe_ref, kg_ref, g128_ref, kA_ref, kB_ref, vA_ref, vB_ref):
    ns = slab_ref.shape[1] // (2 * SEL_BLOCK)
    k, v = _compress_rows(lambda start: slab_ref[0, pl.ds(start, ns, stride=2 * SEL_BLOCK), :], ns, bd_ref, pe_ref)
    k = _group_rms(k, g128_ref[...], kg_ref[...])
    lane_lo = lax.broadcasted_iota(jnp.int32, k.shape, 1) < HEAD_DIM
    ka, kb = _split_heads(k, lane_lo)
    va, vb = _split_heads(v, lane_lo)
    pad = jnp.zeros((LANES - 2 * ns, LANES), _BF)
    for g in range(N_KV):
        for ref, val in ((kA_ref, ka[g]), (kB_ref, kb[g]), (vA_ref, va[g]), (vB_ref, vb[g])):
            ref[0, g, 0:2 * ns, :] = val.astype(_BF)
            if 2 * ns < LANES:
                ref[0, g, 2 * ns:, :] = pad


def _compress_prompt(slab, w):
    b, t, _ = slab.shape
    assert t % QBLK == 0 and t // CMP_BLOCK <= LANES
    slab = slab.reshape(b, 2 * t, LANES)
    full = lambda a: pl.BlockSpec(a.shape, lambda i: (0,) * a.ndim)
    consts = (w['bd'], w['pe_t'], w['kc_gain'], w['g128'])
    out = jax.ShapeDtypeStruct((b, N_KV, LANES, LANES), _BF)
    return pl.pallas_call(
        _compress_prompt_body,
        grid=(b,),
        in_specs=[pl.BlockSpec((1, 2 * t, LANES), lambda i: (i, 0, 0))] + [full(c) for c in consts],
        out_specs=[pl.BlockSpec((1, N_KV, LANES, LANES), lambda i: (i, 0, 0, 0))] * 4,
        out_shape=[out] * 4,
        compiler_params=pltpu.CompilerParams(dimension_semantics=("arbitrary",), vmem_limit_bytes=VMEM_LIMIT),
        name="compress_prompt",
    )(slab, *consts)


BUILD_ROWS = 256


def _top_extra(impb, cand, lane_f, n_extra):
    v = jnp.where(cand, impb, -1.0)
    picked = jnp.zeros(impb.shape, _F32)
    for _ in range(n_extra):
        mx = jnp.max(v, axis=-1, keepdims=True)
        first = jnp.min(jnp.where(v == mx, lane_f, 1e9), axis=-1, keepdims=True)
        hit = lane_f == first
        picked = jnp.where(hit, 1.0, picked)
        v = jnp.where(hit, -1.0, v)
    return picked


def _attn_prompt_body(q_ref, ks_ref, kw_ref, gate_ref, kcA_ref, kcB_ref, vcA_ref, vcB_ref, cbias_ref, tiles_ref,
                      eg_ref, o_ref, ksA, ksB, vsA, vsB, kwA, kwB, vwA, vwB):
    qb = pl.program_id(1)
    t_len = ks_ref.shape[1]
    n_sel = t_len // SEL_BLOCK

    @pl.when(qb == 0)
    def _build():
        def chunk(c, carry):
            r0 = pl.multiple_of(c * BUILD_ROWS, BUILD_ROWS)
            rows = pl.ds(r0, BUILD_ROWS)
            lane = lax.broadcasted_iota(jnp.int32, (BUILD_ROWS, LANES), 1)
            blk = (r0 + lax.broadcasted_iota(jnp.int32, (BUILD_ROWS, LANES), 0)) // SEL_BLOCK
            lane_lo = lane < HEAD_DIM
            oh_hi = jnp.where(lane == blk + HEAD_DIM, 1.0, 0.0)
            oh_lo = jnp.where(lane == blk, 1.0, 0.0)
            for src, k_a, k_b, v_a, v_b, onehot in ((ks_ref, ksA, ksB, vsA, vsB, True),
                                                    (kw_ref, kwA, kwB, vwA, vwB, False)):
                ka, kb = _split_heads(src[0, rows, 0:128], lane_lo)
                va, vb = _split_heads(src[0, rows, 128:256], lane_lo)
                for g in range(N_KV):
                    if onehot:
                        k_a[g, rows, :] = jnp.where(lane_lo, ka[g], oh_hi).astype(_BF)
                        k_b[g, rows, :] = jnp.where(lane_lo, oh_lo, kb[g]).astype(_BF)
                    else:
                        k_a[g, rows, :] = ka[g].astype(_BF)
                        k_b[g, rows, :] = kb[g].astype(_BF)
                    v_a[g, rows, :] = va[g].astype(_BF)
                    v_b[g, rows, :] = vb[g].astype(_BF)
            return carry
        lax.fori_loop(0, t_len // BUILD_ROWS, chunk, 0)

    lane2 = lax.broadcasted_iota(jnp.int32, (2 * QBLK, LANES), 1)
    lo2 = lane2 < HEAD_DIM
    lane1 = lax.broadcasted_iota(jnp.int32, (QBLK, LANES), 1)
    lane1_f = lane1.astype(_F32)
    tpos = qb * QBLK + lax.broadcasted_iota(jnp.int32, (QBLK, LANES), 0)
    cur = tpos // SEL_BLOCK
    forced = (lane1 == 0) | (lane1 == cur) | (lane1 == cur - 1)
    cand = (lane1 >= 1) & (lane1 <= cur - 2)
    few = cur <= TOP_N - 1

    gates = gate_ref[0]
    g_hi = gates.astype(_BF)
    g_lo = (gates - g_hi.astype(_F32)).astype(_BF)
    gexp = _dot(g_hi, eg_ref[...]) + _dot(g_lo, eg_ref[...])

    def attend(qx, k_ref, v_ref, g, x, c_lo, idx_cap):
        def body(c, carry):
            m, l, acc = carry
            rows = pl.ds(pl.multiple_of(c * QBLK, QBLK), QBLK)
            idx = jnp.minimum(qb - c, idx_cap)
            s = _dot_nt(qx, k_ref[g, rows, :]) + tiles_ref[idx, g, x]
            m2 = jnp.maximum(m, jnp.max(s, axis=-1, keepdims=True))
            p = jnp.exp(s - m2)
            al = jnp.exp(m - m2)
            l2 = al * l + jnp.sum(p, axis=-1, keepdims=True)
            acc2 = al * acc + _dot(p.astype(_BF), v_ref[g, rows, :])
            return m2, l2, acc2
        init = (jnp.full((2 * QBLK, 1), NEG, _F32), jnp.zeros((2 * QBLK, 1), _F32),
                jnp.zeros((2 * QBLK, LANES), _F32))
        _, l, acc = lax.fori_loop(c_lo, qb + 1, body, init)
        return acc / l

    for g in range(N_KV):
        qs = jnp.concatenate([q_ref[0, :, (2 * g) * LANES:(2 * g + 1) * LANES],
                              q_ref[0, :, (2 * g + 1) * LANES:(2 * g + 2) * LANES]], axis=0)
        zero = jnp.zeros_like(qs)
        q_a = jnp.where(lo2, qs, zero)
        q_b = jnp.where(lo2, zero, qs)

        def cmp_probs(qx, k_ref, x):
            s = _dot_nt(qx, k_ref[0, g]) + cbias_ref[0, g, x]
            m = jnp.max(s, axis=-1, keepdims=True)
            e = jnp.where(s > 0.5 * NEG, jnp.exp(s - m), 0.0)
            l = jnp.sum(e, axis=-1, keepdims=True)
            return e / jnp.where(l > 0.0, l, 1.0)
        p_a = cmp_probs(q_a, kcA_ref, 0)
        p_b = cmp_probs(q_b, kcB_ref, 1)
        o_c = _dot(p_a.astype(_BF), vcA_ref[0, g]) + _dot(p_b.astype(_BF), vcB_ref[0, g])
        imp = p_a[0:QBLK] + p_a[QBLK:] + p_b[0:QBLK] + p_b[QBLK:]
        impb = imp + pltpu.roll(imp, LANES - n_sel, axis=1)

        picked = lax.cond(qb * QBLK >= TOP_N * SEL_BLOCK,
                          lambda: _top_extra(impb, cand, lane1_f, TOP_N - 3),
                          lambda: jnp.zeros((QBLK, LANES), _F32))
        chosen = forced | (few & (lane1 <= cur)) | ((picked > 0.5) & jnp.logical_not(few))
        sb_lo = jnp.where(chosen | (lane1 >= n_sel), 0.0, NEG)
        sb_hi = pltpu.roll(sb_lo, HEAD_DIM, axis=1)
        sb_lo2 = jnp.concatenate([sb_lo, sb_lo], axis=0).astype(_BF)
        sb_hi2 = jnp.concatenate([sb_hi, sb_hi], axis=0).astype(_BF)
        qa_a = jnp.where(lo2, qs, sb_hi2)
        qa_b = jnp.where(lo2, sb_lo2, qs)
        o_s = attend(qa_a, ksA, vsA, g, 0, 0, 2) + attend(qa_b, ksB, vsB, g, 1, 0, 2)
        w_lo = jnp.maximum(qb - WINDOW // QBLK, 0)
        o_w = attend(q_a, kwA, vwA, g, 0, w_lo, WINDOW // QBLK) + attend(q_b, kwB, vwB, g, 1, w_lo, WINDOW // QBLK)

        for pr in range(2):
            rows = slice(pr * QBLK, (pr + 1) * QBLK)
            col = (2 * g + pr) * LANES
            o_ref[0, :, col:col + LANES] = (gexp[:, col:col + LANES] * o_c[rows]
                                            + gexp[:, ATT_WIDTH + col:ATT_WIDTH + col + LANES] * o_s[rows]
                                            + gexp[:, 2 * ATT_WIDTH + col:2 * ATT_WIDTH + col + LANES] * o_w[rows])


def _attn_prompt(q, sel, win, gates, kc, cbias, w):
    b, t, _ = sel.shape
    nq = t // QBLK
    assert t // SEL_BLOCK <= CMP_BLOCK
    kcA, kcB, vcA, vcB = kc
    full = lambda a: pl.BlockSpec(a.shape, lambda i, j: (0,) * a.ndim)
    slab = pl.BlockSpec((1, t, 256), lambda i, j: (i, 0, 0))
    kcs = pl.BlockSpec((1, N_KV, LANES, LANES), lambda i, j: (i, 0, 0, 0))
    scratch = [pltpu.VMEM((N_KV, t, LANES), _BF)] * 8
    return pl.pallas_call(
        _attn_prompt_body,
        grid=(b, nq),
        in_specs=[pl.BlockSpec((1, QBLK, ATT_WIDTH), lambda i, j: (i, j, 0)), slab, slab,
                  pl.BlockSpec((1, QBLK, LANES), lambda i, j: (i, j, 0)), kcs, kcs, kcs, kcs,
                  pl.BlockSpec((1, N_KV, 2, 2 * QBLK, LANES), lambda i, j: (j, 0, 0, 0, 0)),
                  full(w['tiles']), full(w['egate'])],
        out_specs=pl.BlockSpec((1, QBLK, ATT_WIDTH), lambda i, j: (i, j, 0)),
        out_shape=jax.ShapeDtypeStruct((b, t, ATT_WIDTH), _F32),
        scratch_shapes=scratch,
        compiler_params=pltpu.CompilerParams(dimension_semantics=("arbitrary", "arbitrary"),
                                             vmem_limit_bytes=VMEM_LIMIT),
        name="attn_prompt",
    )(q, sel, win, gates, kcA, kcB, vcA, vcB, cbias, w['tiles'], w['egate'])


N_PICK = TOP_N - 3
N_SLOT = TOP_N - 1
PAGE_ROWS = 2 * PAGE_SIZE


def _sample_cmp_body(pt_ref, q_ref, cache_ref, bd_ref, pe_ref, kg_ref, g128_ref, sb_ref, oc_ref, pick_ref,
                     buf, sem):
    b = pl.program_id(0)
    nb = pl.num_programs(0)
    n_pages = pt_ref.shape[1]
    nj = 2 * n_pages
    slot = b % 2

    def page_copy(bb, p, sl):
        return pltpu.make_async_copy(cache_ref.at[pt_ref[bb, p]],
                                     buf.at[sl, pl.ds(pl.multiple_of(p * PAGE_ROWS, PAGE_ROWS), PAGE_ROWS), :],
                                     sem.at[sl])

    def fetch(bb, sl):
        def start(p, c):
            page_copy(bb, p, sl).start()
            return c
        lax.fori_loop(0, n_pages, start, 0)

    @pl.when(b == 0)
    def _prime():
        fetch(0, 0)

    @pl.when(b + 1 < nb)
    def _prefetch():
        fetch(b + 1, 1 - slot)

    def wait(p, c):
        page_copy(b, p, slot).wait()
        return c
    lax.fori_loop(0, n_pages, wait, 0)

    k, v = _compress_rows(lambda start: buf[slot, pl.ds(start, nj, stride=2 * SEL_BLOCK), :], nj, bd_ref, pe_ref)
    k = _group_rms(k, g128_ref[...], kg_ref[...]).astype(_BF)
    v = v.astype(_BF)
    s = _dot_nt(q_ref[0], k) + sb_ref[...]
    m = jnp.max(s, axis=-1, keepdims=True)
    e = jnp.exp(s - m)
    p = e / jnp.sum(e, axis=-1, keepdims=True)
    oc_ref[0] = _dot(p.astype(_BF), v)
    imp8 = p[:, 0:nj] + p[:, nj:]
    rows = [imp8[h:h + 1] for h in range(N_HEADS)]
    imp_g = [rows[g * GROUP] + rows[g * GROUP + 1] + rows[g * GROUP + 2] + rows[g * GROUP + 3] for g in range(N_KV)]
    imp = jnp.concatenate(imp_g + [jnp.full((8 - N_KV, nj), -1.0, _F32)], axis=0)
    lane = lax.broadcasted_iota(jnp.int32, (8, nj), 1)
    lane_f = lane.astype(_F32)
    v_c = jnp.where((lane >= 1) & (lane <= nj - 2), imp, -1.0)
    picks = jnp.zeros((8, nj), _F32)
    for i in range(N_PICK):
        mx = jnp.max(v_c, axis=-1, keepdims=True)
        first = jnp.min(jnp.where(v_c == mx, lane_f, 1e9), axis=-1, keepdims=True)
        picks = jnp.where(lane == i, first, picks)
        v_c = jnp.where(lane_f == first, -1.0, v_c)
    pick_ref[0] = picks.astype(jnp.int32)


def _sample_cmp(page_table, q8, cache, sbias, w):
    db, n_pages = page_table.shape
    nj = 2 * n_pages
    assert nj - 2 >= N_PICK
    full = lambda a: pl.BlockSpec(a.shape, lambda i, pt: (0,) * a.ndim)
    consts = (w['bd'], w['pe_t'], w['kc_gain'], w['g128'], sbias)
    grid_spec = pltpu.PrefetchScalarGridSpec(
        num_scalar_prefetch=1,
        grid=(db,),
        in_specs=[pl.BlockSpec((1, 8, LANES), lambda i, pt: (i, 0, 0)), pl.BlockSpec(memory_space=pl.ANY)]
        + [full(c) for c in consts],
        out_specs=[pl.BlockSpec((1, 8, LANES), lambda i, pt: (i, 0, 0)),
                   pl.BlockSpec((1, 8, nj), lambda i, pt: (i, 0, 0))],
        scratch_shapes=[pltpu.VMEM((2, n_pages * PAGE_ROWS, LANES), _F32), pltpu.SemaphoreType.DMA((2,))],
    )
    return pl.pallas_call(
        _sample_cmp_body,
        grid_spec=grid_spec,
        out_shape=[jax.ShapeDtypeStruct((db, 8, LANES), _F32), jax.ShapeDtypeStruct((db, 8, nj), jnp.int32)],
        compiler_params=pltpu.CompilerParams(dimension_semantics=("arbitrary",), vmem_limit_bytes=VMEM_LIMIT),
        name="sample_cmp",
    )(page_table, q8, cache, *consts)


SEL_ROWS = 512


def _sample_attn_body(pt_ref, pick_ref, q_ref, cache_ref, win_ref, nsel_ref, nwin_ref, gate_ref, oc_ref,
                      base_ref, near_ref, bwin_ref, o_ref, wout_ref, kb, sem):
    b = pl.program_id(0)
    nb = pl.num_programs(0)
    nj = 2 * pt_ref.shape[1]
    wlen = win_ref.shape[1]
    slot = b % 2
    new_row = N_SLOT * SEL_BLOCK

    def block_of(bb, g, s):
        if s == 0:
            return 0
        if s == N_SLOT - 1:
            return nj - 1
        return pick_ref[bb, g * N_PICK + (s - 1)]

    def block_copy(bb, g, s, sl):
        j = block_of(bb, g, s)
        page = pt_ref[bb, j // 2]
        off = pl.multiple_of((j % 2) * SEL_BLOCK, SEL_BLOCK)
        return pltpu.make_async_copy(cache_ref.at[page, pl.ds(off, SEL_BLOCK), :],
                                     kb.at[sl, g, pl.ds(s * SEL_BLOCK, SEL_BLOCK), :], sem.at[sl])

    def fetch(bb, sl):
        for g in range(N_KV):
            for s in range(N_SLOT):
                block_copy(bb, g, s, sl).start()

    @pl.when(b == 0)
    def _prime():
        kb[...] = jnp.zeros(kb.shape, _F32)
        fetch(0, 0)

    @pl.when(b + 1 < nb)
    def _prefetch():
        fetch(b + 1, 1 - slot)

    wout_ref[0, 0:wlen - 1, :] = win_ref[0, 1:wlen, :]
    wout_ref[0, wlen - 1:wlen, :] = nwin_ref[0]
    q8 = q_ref[0]
    sw = _dot_nt(q8, wout_ref[0, :, 0:128].astype(_BF)) + bwin_ref[...]
    mw = jnp.max(sw, axis=-1, keepdims=True)
    ew = jnp.exp(sw - mw)
    pw = ew / jnp.sum(ew, axis=-1, keepdims=True)
    o_w = _dot(pw.astype(_BF), wout_ref[0, :, 128:256].astype(_BF))

    for g in range(N_KV):
        for s in range(N_SLOT):
            block_copy(b, g, s, slot).wait()

    lane = lax.broadcasted_iota(jnp.int32, (8, SEL_ROWS), 1)
    lane_slot = lane // SEL_BLOCK
    o_sel = []
    for g in range(N_KV):
        kb[slot, g, new_row:new_row + 1, :] = nsel_ref[0]
        near = jnp.zeros((8, SEL_ROWS), jnp.bool_)
        for s in range(1, N_SLOT - 1):
            hit = jnp.full((8, SEL_ROWS), pick_ref[b, g * N_PICK + (s - 1)], jnp.int32) == nj - 2
            near = near | (hit & (lane_slot == s))
        bias = jnp.where(near, near_ref[...], base_ref[...])
        ss = _dot_nt(q8, kb[slot, g, :, 0:128].astype(_BF)) + bias
        ms = jnp.max(ss, axis=-1, keepdims=True)
        es = jnp.exp(ss - ms)
        ps = es / jnp.sum(es, axis=-1, keepdims=True)
        o_sel.append(_dot(ps.astype(_BF), kb[slot, g, :, 128:256].astype(_BF)))
    head_g = lax.broadcasted_iota(jnp.int32, (8, LANES), 0) // GROUP
    o_s = jnp.where(head_g == 0, o_sel[0], o_sel[1])
    gates = gate_ref[0]
    o_ref[0] = gates[:, 0:1] * oc_ref[0] + gates[:, 1:2] * o_s + gates[:, 2:3] * o_w


def _sample_attn(page_table, picks, q8, cache, win, new_sel, new_win, gates8, o_c, tabs):
    db, n_pages = page_table.shape
    wlen = win.shape[1]
    full = lambda a: pl.BlockSpec(a.shape, lambda i, pt, pk: (0,) * a.ndim)
    per_b = lambda r, c: pl.BlockSpec((1, r, c), lambda i, pt, pk: (i, 0, 0))
    base, near, bwin = tabs
    grid_spec = pltpu.PrefetchScalarGridSpec(
        num_scalar_prefetch=2,
        grid=(db,),
        in_specs=[per_b(8, LANES), pl.BlockSpec(memory_space=pl.ANY), per_b(wlen, 256), per_b(1, 256), per_b(1, 256),
                  per_b(8, LANES), per_b(8, LANES), full(base), full(near), full(bwin)],
        out_specs=[per_b(8, LANES), per_b(wlen, 256)],
        scratch_shapes=[pltpu.VMEM((2, N_KV, SEL_ROWS, 256), _F32), pltpu.SemaphoreType.DMA((2,))],
    )
    return pl.pallas_call(
        _sample_attn_body,
        grid_spec=grid_spec,
        out_shape=[jax.ShapeDtypeStruct((db, 8, LANES), _F32), jax.ShapeDtypeStruct((db, wlen, 256), _F32)],
        compiler_params=pltpu.CompilerParams(dimension_semantics=("arbitrary",), vmem_limit_bytes=VMEM_LIMIT),
        name="sample_attn",
    )(page_table, picks, q8, cache, win, new_sel, new_win, gates8, o_c, base, near, bwin)


def _rel_bucket(dist):
    n = jnp.maximum(dist, 0)
    max_exact = RP_BUCKETS // 2
    nf = jnp.maximum(n, 1).astype(_F32)
    large = max_exact + (jnp.log(nf / max_exact) / math.log(RP_MAX_DIST / max_exact)
                         * (RP_BUCKETS - max_exact)).astype(jnp.int32)
    large = jnp.minimum(large, RP_BUCKETS - 1)
    return jnp.where(n < max_exact, n, large)


def _bias_of(rel_bias, dist):
    onehot = _rel_bucket(dist)[..., None, None] == jnp.arange(RP_BUCKETS)[:, None]
    return jnp.sum(jnp.where(onehot, rel_bias, 0.0), axis=-2)


def _head_major(x, lead):
    n = x.shape[-2]
    x = x.reshape(lead + (QBLK, n, N_KV, 2, 2))
    nl = len(lead)
    x = jnp.transpose(x, tuple(range(nl)) + (nl + 2, nl + 4, nl + 3, nl, nl + 1))
    return x.reshape(lead + (N_KV, 2, 2 * QBLK, n))


def _prompt_tables(rel_bias, t_len):
    nq = t_len // QBLK
    ns = t_len // SEL_BLOCK
    ti = np.arange(QBLK)[:, None]
    ki = np.arange(QBLK)[None, :]
    n_idx = WINDOW // QBLK + 1
    dist = np.stack([i * QBLK + ti - ki for i in range(n_idx)])
    valid = dist >= 0
    valid[n_idx - 1] &= dist[n_idx - 1] < WINDOW
    tiles = jnp.where(valid[..., None], _bias_of(rel_bias, jnp.asarray(dist)), NEG)
    tiles = _head_major(tiles, (n_idx,))
    lane = np.arange(LANES)
    blk = np.where(lane < ns, 2 * lane, 2 * (lane - ns) + 1)
    dist_c = np.arange(t_len)[:, None] - (CMP_BLOCK * blk + CMP_BLOCK - 1)[None, :]
    valid_c = (dist_c >= 0) & (lane < 2 * ns)[None, :]
    cb = jnp.where(valid_c[..., None], _bias_of(rel_bias, jnp.asarray(dist_c)), NEG)
    cb = _head_major(cb.reshape(nq, QBLK, LANES, N_HEADS), (nq,))
    return tiles, cb


def _sample_tables(rel_bias, past_len, wlen):
    nj = past_len // SEL_BLOCK
    j = np.arange(nj)
    blk = np.concatenate([2 * j, 2 * j + 1])
    sbias = _bias_of(rel_bias, jnp.asarray(past_len - (CMP_BLOCK * blk + CMP_BLOCK - 1))).T
    lane = np.arange(SEL_ROWS)
    new_row = N_SLOT * SEL_BLOCK
    dist_b = np.where(lane < new_row - SEL_BLOCK, past_len, np.where(lane < new_row, SEL_BLOCK - lane % SEL_BLOCK, 0))
    base = jnp.where((lane <= new_row)[None, :], _bias_of(rel_bias, jnp.asarray(dist_b)).T, NEG)
    near = _bias_of(rel_bias, jnp.asarray(2 * SEL_BLOCK - lane % SEL_BLOCK)).T
    bwin = _bias_of(rel_bias, jnp.asarray(wlen - 1 - np.arange(wlen))).T
    return sbias, (base, near, bwin)


def _prep(norm_mix, w_in, q_norm, k_norm, cmp_pe, w_cmp, conv_w, out_norm, w_out, norm_mlp, w_up, w_down,
          norm_ple, w_ple_gate, w_ple_proj):
    w_in = w_in[0]
    o_kv = ATT_WIDTH
    o_g = o_kv + 6 * KV_WIDTH
    o_c = o_g + 3 * N_HEADS
    bd = jnp.einsum('gh,plde->lpgdhe', jnp.eye(N_KV, dtype=_F32), w_cmp[0]).reshape(CMP_BLOCK, 2, LANES, LANES)
    pe_t = jnp.transpose(jnp.tile(cmp_pe[0], (1, 1, N_KV)), (1, 0, 2))
    egate = np.zeros((LANES, 3 * ATT_WIDTH), np.float32)
    for br in range(3):
        for h in range(N_HEADS):
            egate[br * N_HEADS + h, br * ATT_WIDTH + h * HEAD_DIM:br * ATT_WIDTH + (h + 1) * HEAD_DIM] = 1.0
    ones = np.ones((HEAD_DIM, HEAD_DIM), np.float32)
    return {
        'norm_mix': norm_mix[0][None], 'wq': w_in[:, :o_kv].astype(_BF), 'wkv': w_in[:, o_kv:o_g].astype(_BF),
        'wg': jnp.pad(w_in[:, o_g:o_c], ((0, 0), (0, LANES - 3 * N_HEADS))).astype(_BF),
        'wc': w_in[:, o_c:].astype(_BF),
        'q_gain': jnp.tile(q_norm[0], N_HEADS)[None],
        'k_gain': jnp.stack([jnp.tile(k_norm[0, 1], N_KV), jnp.tile(k_norm[0, 2], N_KV)]),
        'kc_gain': jnp.tile(k_norm[0, 0], N_KV)[None],
        'g512': jnp.asarray(np.kron(np.eye(N_HEADS, dtype=np.float32), ones), _BF),
        'g128': jnp.asarray(np.kron(np.eye(N_KV, dtype=np.float32), ones), _BF),
        'bd': bd.astype(_BF), 'pe_t': pe_t,
        'egate': jnp.asarray(egate, _BF),
        'conv_w': conv_w[0], 'out_norm': out_norm[0][None], 'w_out': w_out[0].astype(_BF),
        'norm_mlp': norm_mlp[0][None], 'w_up': w_up[0].astype(_BF), 'w_down': w_down[0].astype(_BF),
        'norm_ple': norm_ple[0][None], 'w_ple_gate': w_ple_gate[0].astype(_BF),
        'w_ple_proj': w_ple_proj[0].astype(_BF),
    }


def kernel(x_prompt, x_sample, p_prompt, p_sample, cache_cmp_kv, cache_sel_kv, state_win_kv, state_conv, page_table, rel_bias, norm_mix, w_in, q_norm, k_norm, cmp_pe, w_cmp, conv_w, out_norm, w_out, norm_mlp, w_up, w_down, norm_ple, w_ple_gate, w_ple_proj):
    bp, tp, _ = x_prompt.shape
    db, ts, _ = x_sample.shape
    assert norm_mix.shape[0] == 1 and ts == 1 and tp >= WINDOW
    n_pages = page_table.shape[1]
    past_len = n_pages * PAGE_SIZE
    wlen = state_win_kv.shape[2]
    assert wlen == WINDOW and past_len >= 4 * SEL_BLOCK
    w = _prep(norm_mix, w_in, q_norm, k_norm, cmp_pe, w_cmp, conv_w, out_norm, w_out, norm_mlp, w_up, w_down,
              norm_ple, w_ple_gate, w_ple_proj)
    w['tiles'], cbias = _prompt_tables(rel_bias, tp)
    sbias, stabs = _sample_tables(rel_bias, past_len, wlen)
    kv6 = lambda a, b, t: a.reshape(1, b, t, 2, N_KV, HEAD_DIM)

    xp = x_prompt.reshape(bp * tp, D_MODEL)
    q, cmp_p, sel_p, win_p, gates, cb, u = _inproj(xp, w)
    seq = lambda a: a.reshape(bp, tp, a.shape[-1])
    kc = _compress_prompt(seq(cmp_p), w)
    o_att = _attn_prompt(seq(q), seq(sel_p), seq(win_p), seq(gates), kc, cbias, w)
    y_p = _tail(xp, o_att.reshape(bp * tp, ATT_WIDTH), cb, u, None, p_prompt[0].reshape(bp * tp, PLE_DIM), w, tp)

    xs = x_sample.reshape(db, D_MODEL)
    q_s, cmp_s, sel_s, win_s, gates_s, cb_s, u_s = _inproj(xs, w)
    qh = q_s.reshape(db, N_KV, GROUP, HEAD_DIM)
    zq = jnp.zeros_like(qh[:, 0])
    q8 = jnp.concatenate([jnp.concatenate([qh[:, 0], zq], axis=-1), jnp.concatenate([zq, qh[:, 1]], axis=-1)], axis=1)
    n_phys = cache_cmp_kv.shape[1]
    o_c, picks = _sample_cmp(page_table, q8, cache_cmp_kv.reshape(n_phys, PAGE_ROWS, LANES), sbias, w)
    picks = picks[:, :N_KV, :N_PICK].reshape(db, N_KV * N_PICK)
    gates8 = jnp.pad(jnp.transpose(gates_s[:, :3 * N_HEADS].reshape(db, 3, N_HEADS), (0, 2, 1)),
                     ((0, 0), (0, 0), (0, LANES - 3)))
    o8, win_new = _sample_attn(page_table, picks, q8, cache_sel_kv.reshape(n_phys, PAGE_SIZE, 256),
                               state_win_kv[0].reshape(db, wlen, 256), sel_s[:, None, :], win_s[:, None, :],
                               gates8, o_c, stabs)
    o8 = o8.reshape(db, N_KV, GROUP, N_KV, HEAD_DIM)
    o_att_s = jnp.stack([o8[:, 0, :, 0], o8[:, 1, :, 1]], axis=1).reshape(db, ATT_WIDTH)
    u_prev = jnp.transpose(state_conv[0], (1, 0, 2))
    y_s = _tail(xs, o_att_s, cb_s, u_s, u_prev, p_sample[0].reshape(db, PLE_DIM), w, None)

    return (y_p.reshape(bp, tp, D_MODEL), y_s.reshape(db, 1, D_MODEL),
            kv6(cmp_p, bp, tp), kv6(sel_p, bp, tp), kv6(seq(win_p)[:, tp - WINDOW:], bp, WINDOW),
            seq(u)[:, tp - 2:][None],
            kv6(cmp_s, db, 1), kv6(sel_s, db, 1), kv6(win_new, db, wlen),
            jnp.concatenate([state_conv[0][:, 1:], u_s[:, None, :]], axis=1)[None])
```

```python
import functools
import math

import jax
import jax.numpy as jnp
import numpy as np
from jax import lax
from jax.experimental import pallas as pl
from jax.experimental.pallas import tpu as pltpu

D_MODEL = 1024
HEAD_DIM = 64
N_HEADS = 8
N_KV = 2
GROUP = N_HEADS // N_KV
ATT_WIDTH = N_HEADS * HEAD_DIM
KV_WIDTH = N_KV * HEAD_DIM
CONV_DIM = D_MODEL - ATT_WIDTH
PAGE_SIZE = 128
CMP_BLOCK = 32
SEL_BLOCK = 64
TOP_N = 8
WINDOW = 512
RP_BUCKETS = 32
RP_MAX_DIST = 128
D_FF = 4 * D_MODEL
PLE_DIM = 256
SCALE = HEAD_DIM ** -0.5
NEG = -1e30
EPS = 1e-6

LANES = 128
QBLK = 128
VMEM_LIMIT = 56 * 1024 * 1024

_BF = jnp.bfloat16
_F32 = jnp.float32


def _dot(a, b):
    return jnp.dot(a, b, preferred_element_type=_F32)


def _dot_nt(a, b):
    return lax.dot_general(a, b, (((1,), (1,)), ((), ())), preferred_element_type=_F32)


def _rms_rows(x, gain):
    return x * lax.rsqrt(jnp.mean(x * x, axis=-1, keepdims=True) + EPS) * gain


def _group_rms(z, gmat, gain):
    ssq = _dot((z * z).astype(_BF), gmat) * (1.0 / HEAD_DIM)
    return z * lax.rsqrt(ssq + EPS) * gain


def _inproj_body(x_ref, nm_ref, wq_ref, wkv_ref, wg_ref, wc_ref, qg_ref, kg_ref, g512_ref, g128_ref,
                 q_ref, cmp_ref, sel_ref, win_ref, gate_ref, cb_ref, u_ref):
    a = _rms_rows(x_ref[...], nm_ref[...]).astype(_BF)
    zq = _dot(a, wq_ref[...])
    q_ref[...] = (_group_rms(zq, g512_ref[...], qg_ref[...]) * SCALE).astype(_BF)
    zkv = _dot(a, wkv_ref[...])
    cmp_ref[...] = zkv[:, 0:256]
    g128 = g128_ref[...]
    sel_ref[:, 0:128] = _group_rms(zkv[:, 256:384], g128, kg_ref[0:1, :])
    sel_ref[:, 128:256] = zkv[:, 384:512]
    win_ref[:, 0:128] = _group_rms(zkv[:, 512:640], g128, kg_ref[1:2, :])
    win_ref[:, 128:256] = zkv[:, 640:768]
    gate_ref[...] = jax.nn.sigmoid(_dot(a, wg_ref[...]))
    zc = _dot(a, wc_ref[...])
    cb_ref[...] = zc[:, 0:512]
    u_ref[...] = zc[:, 512:1024] * zc[:, 1024:1536]


def _inproj(x, w):
    n = x.shape[0]
    tm = min(256, n)
    row = lambda c: pl.BlockSpec((tm, c), lambda i: (i, 0))
    full = lambda a: pl.BlockSpec(a.shape, lambda i: (0,) * a.ndim)
    consts = (w['norm_mix'], w['wq'], w['wkv'], w['wg'], w['wc'], w['q_gain'], w['k_gain'], w['g512'], w['g128'])
    return pl.pallas_call(
        _inproj_body,
        grid=(n // tm,),
        in_specs=[row(D_MODEL)] + [full(c) for c in consts],
        out_specs=[row(512), row(256), row(256), row(256), row(128), row(512), row(512)],
        out_shape=[jax.ShapeDtypeStruct((n, 512), _BF)] + [jax.ShapeDtypeStruct((n, 256), _F32)] * 3
        + [jax.ShapeDtypeStruct((n, 128), _F32)] + [jax.ShapeDtypeStruct((n, 512), _F32)] * 2,
        compiler_params=pltpu.CompilerParams(dimension_semantics=("arbitrary",), vmem_limit_bytes=VMEM_LIMIT),
        name="inproj",
    )(x, *consts)


FF_CHUNK = 1024


def _tail_body(halo, h_ref, o_ref, cb_ref, u_ref, up_ref, p_ref, cw_ref, on_ref, nmlp_ref, nple_ref,
               wout_hbm, wup_hbm, wdn_hbm, wgate_hbm, wproj_hbm, y_ref,
               uext_ref, wout_ref, wup_ref, wdn_ref, wgate_ref, wproj_ref, wsem):
    @pl.when(pl.program_id(0) == 0)
    def _load_weights():
        copies = [pltpu.make_async_copy(src, dst, wsem.at[i]) for i, (src, dst) in enumerate(
            ((wout_hbm, wout_ref), (wup_hbm, wup_ref), (wdn_hbm, wdn_ref), (wgate_hbm, wgate_ref),
             (wproj_hbm, wproj_ref)))]
        for c in copies:
            c.start()
        for c in copies:
            c.wait()

    tm = h_ref.shape[0]
    u = u_ref[...]
    if halo:
        first = (pl.program_id(0) % halo) == 0
        prev = jnp.where(first, 0.0, up_ref[...])
        uext_ref[0:8, :] = prev
        uext_ref[8:tm + 8, :] = u
        u2 = uext_ref[6:tm + 6, :]
        u1 = uext_ref[7:tm + 7, :]
    else:
        u2 = up_ref[0]
        u1 = up_ref[1]
    yc = cw_ref[0:1, :] * u2 + cw_ref[1:2, :] * u1 + cw_ref[2:3, :] * u
    mix_a = _rms_rows(o_ref[...], on_ref[:, 0:ATT_WIDTH]).astype(_BF)
    mix_c = _rms_rows(cb_ref[...] * yc, on_ref[:, ATT_WIDTH:]).astype(_BF)
    h = h_ref[...] + _dot(mix_a, wout_ref[0:ATT_WIDTH, :]) + _dot(mix_c, wout_ref[ATT_WIDTH:, :])
    a = _rms_rows(h, nmlp_ref[...]).astype(_BF)
    y_ref[...] = h
    for c in range(D_FF // FF_CHUNK):
        t = jnp.maximum(_dot(a, wup_ref[:, c * FF_CHUNK:(c + 1) * FF_CHUNK]), 0.0)
        y_ref[...] += _dot((t * t).astype(_BF), wdn_ref[c * FF_CHUNK:(c + 1) * FF_CHUNK, :])
    h = y_ref[...]
    a = _rms_rows(h, nple_ref[...]).astype(_BF)
    gate = jax.nn.sigmoid(_dot(a, wgate_ref[...]))
    y_ref[...] = h + gate * _dot(p_ref[...].astype(_BF), wproj_ref[...])


def _tail(h, o_att, cb, u, u_prev, p, w, seq_len):
    n = h.shape[0]
    tm = min(256, n)
    row = lambda c: pl.BlockSpec((tm, c), lambda i: (i, 0))
    const = lambda a: pl.BlockSpec(a.shape, lambda i: (0,) * a.ndim)
    if seq_len is not None:
        halo = seq_len // tm
        up_spec = pl.BlockSpec((8, CONV_DIM), lambda i: (jnp.maximum(i * (tm // 8) - 1, 0), 0))
        up = u
    else:
        halo = 0
        up_spec = pl.BlockSpec((2, tm, CONV_DIM), lambda i: (0, i, 0))
        up = u_prev
    consts = (w['conv_w'], w['out_norm'], w['norm_mlp'], w['norm_ple'])
    mats = (w['w_out'], w['w_up'], w['w_down'], w['w_ple_gate'], w['w_ple_proj'])
    return pl.pallas_call(
        functools.partial(_tail_body, halo),
        grid=(n // tm,),
        in_specs=[row(D_MODEL), row(ATT_WIDTH), row(CONV_DIM), row(CONV_DIM), up_spec, row(PLE_DIM)]
        + [const(c) for c in consts] + [pl.BlockSpec(memory_space=pl.ANY)] * len(mats),
        out_specs=row(D_MODEL),
        out_shape=jax.ShapeDtypeStruct((n, D_MODEL), _F32),
        scratch_shapes=[pltpu.VMEM((tm + 8, CONV_DIM), _F32)] + [pltpu.VMEM(m.shape, _BF) for m in mats]
        + [pltpu.SemaphoreType.DMA((len(mats),))],
        compiler_params=pltpu.CompilerParams(dimension_semantics=("arbitrary",), vmem_limit_bytes=VMEM_LIMIT),
        name="tail",
    )(h, o_att, cb, u, up, p, *consts, *mats)


def _split_heads(x, lane_lo):
    xr = pltpu.roll(x, HEAD_DIM, axis=1)
    zero = jnp.zeros_like(x)
    a = (jnp.where(lane_lo, x, zero), jnp.where(lane_lo, xr, zero))
    b = (jnp.where(lane_lo, zero, xr), jnp.where(lane_lo, zero, x))
    return a, b


def _compress_rows(load, ns, bd_ref, pe_ref):
    acc = [jnp.zeros((2 * ns, LANES), _F32), jnp.zeros((2 * ns, LANES), _F32)]
    for l in range(CMP_BLOCK):
        for plane in range(2):
            x = jnp.concatenate([load(2 * l + plane), load(2 * (CMP_BLOCK + l) + plane)], axis=0)
            x = x + pe_ref[l, plane:plane + 1, :]
            acc[plane] = acc[plane] + _dot(x.astype(_BF), bd_ref[l, plane])
    return acc


def _compress_prompt_body(slab_ref, bd_ref, pe_ref, kg_ref, g128_ref, kA_ref, kB_ref, vA_ref, vB_ref):
    ns = slab_ref.shape[1] // (2 * SEL_BLOCK)
    k, v = _compress_rows(lambda start: slab_ref[0, pl.ds(start, ns, stride=2 * SEL_BLOCK), :], ns, bd_ref, pe_ref)
    k = _group_rms(k, g128_ref[...], kg_ref[...])
    lane_lo = lax.broadcasted_iota(jnp.int32, k.shape, 1) < HEAD_DIM
    ka, kb = _split_heads(k, lane_lo)
    va, vb = _split_heads(v, lane_lo)
    pad = jnp.zeros((LANES - 2 * ns, LANES), _BF)
    for g in range(N_KV):
        for ref, val in ((kA_ref, ka[g]), (kB_ref, kb[g]), (vA_ref, va[g]), (vB_ref, vb[g])):
            ref[0, g, 0:2 * ns, :] = val.astype(_BF)
            if 2 * ns < LANES:
                ref[0, g, 2 * ns:, :] = pad


def _compress_prompt(slab, w):
    b, t, _ = slab.shape
    assert t % QBLK == 0 and t // CMP_BLOCK <= LANES
    slab = slab.reshape(b, 2 * t, LANES)
    full = lambda a: pl.BlockSpec(a.shape, lambda i: (0,) * a.ndim)
    consts = (w['bd'], w['pe_t'], w['kc_gain'], w['g128'])
    out = jax.ShapeDtypeStruct((b, N_KV, LANES, LANES), _BF)
    return pl.pallas_call(
        _compress_prompt_body,
        grid=(b,),
        in_specs=[pl.BlockSpec((1, 2 * t, LANES), lambda i: (i, 0, 0))] + [full(c) for c in consts],
        out_specs=[pl.BlockSpec((1, N_KV, LANES, LANES), lambda i: (i, 0, 0, 0))] * 4,
        out_shape=[out] * 4,
        compiler_params=pltpu.CompilerParams(dimension_semantics=("arbitrary",), vmem_limit_bytes=VMEM_LIMIT),
        name="compress_prompt",
    )(slab, *consts)


BUILD_ROWS = 256


def _top_extra(impb, cand, lane_f, n_extra):
    v = jnp.where(cand, impb, -1.0)
    picked = jnp.zeros(impb.shape, _F32)
    for _ in range(n_extra):
        mx = jnp.max(v, axis=-1, keepdims=True)
        first = jnp.min(jnp.where(v == mx, lane_f, 1e9), axis=-1, keepdims=True)
        hit = lane_f == first
        picked = jnp.where(hit, 1.0, picked)
        v = jnp.where(hit, -1.0, v)
    return picked


def _attn_prompt_body(q_ref, ks_ref, kw_ref, gate_ref, kcA_ref, kcB_ref, vcA_ref, vcB_ref, cbias_ref, tiles_ref,
                      eg_ref, o_ref, ksA, ksB, vsA, vsB, kwA, kwB, vwA, vwB, s_scr, p_scr, m_scr, acc_scr):
    qb = pl.program_id(1)
    t_len = ks_ref.shape[1]
    n_sel = t_len // SEL_BLOCK

    @pl.when(qb == 0)
    def _build():
        def chunk(c, carry):
            r0 = pl.multiple_of(c * BUILD_ROWS, BUILD_ROWS)
            rows = pl.ds(r0, BUILD_ROWS)
            lane = lax.broadcasted_iota(jnp.int32, (BUILD_ROWS, LANES), 1)
            blk = (r0 + lax.broadcasted_iota(jnp.int32, (BUILD_ROWS, LANES), 0)) // SEL_BLOCK
            lane_lo = lane < HEAD_DIM
            oh_hi = jnp.where(lane == blk + HEAD_DIM, 1.0, 0.0)
            oh_lo = jnp.where(lane == blk, 1.0, 0.0)
            one_hi = jnp.where(lane == HEAD_DIM, 1.0, 0.0)
            one_lo = jnp.where(lane == 0, 1.0, 0.0)
            for src, k_a, k_b, v_a, v_b, onehot in ((ks_ref, ksA, ksB, vsA, vsB, True),
                                                    (kw_ref, kwA, kwB, vwA, vwB, False)):
                ka, kb = _split_heads(src[0, rows, 0:128], lane_lo)
                va, vb = _split_heads(src[0, rows, 128:256], lane_lo)
                for g in range(N_KV):
                    if onehot:
                        k_a[g, rows, :] = jnp.where(lane_lo, ka[g], oh_hi).astype(_BF)
                        k_b[g, rows, :] = jnp.where(lane_lo, oh_lo, kb[g]).astype(_BF)
                    else:
                        k_a[g, rows, :] = ka[g].astype(_BF)
                        k_b[g, rows, :] = kb[g].astype(_BF)
                    v_a[g, rows, :] = jnp.where(lane_lo, va[g], one_hi).astype(_BF)
                    v_b[g, rows, :] = jnp.where(lane_lo, one_lo, vb[g]).astype(_BF)
            return carry
        lax.fori_loop(0, t_len // BUILD_ROWS, chunk, 0)
        p_scr[...] = jnp.zeros(p_scr.shape, _BF)

    lane2 = lax.broadcasted_iota(jnp.int32, (2 * QBLK, LANES), 1)
    lo2 = lane2 < HEAD_DIM
    lane1 = lax.broadcasted_iota(jnp.int32, (QBLK, LANES), 1)
    lane1_f = lane1.astype(_F32)
    tpos = qb * QBLK + lax.broadcasted_iota(jnp.int32, (QBLK, LANES), 0)
    cur = tpos // SEL_BLOCK
    forced = (lane1 == 0) | (lane1 == cur) | (lane1 == cur - 1)
    cand = (lane1 >= 1) & (lane1 <= cur - 2)
    few = cur <= TOP_N - 1

    gates = gate_ref[0]
    g_hi = gates.astype(_BF)
    g_lo = (gates - g_hi.astype(_F32)).astype(_BF)
    gexp = _dot(g_hi, eg_ref[...]) + _dot(g_lo, eg_ref[...])

    def normalize(acc, x):
        l = jnp.sum(jnp.where(lane2 == (HEAD_DIM if x == 0 else 0), acc, 0.0), axis=-1, keepdims=True)
        keep = lo2 if x == 0 else jnp.logical_not(lo2)
        return jnp.where(keep, acc / l, 0.0)

    o_cmp, q_plain, q_aug = [], [], []
    for g in range(N_KV):
        qs = jnp.concatenate([q_ref[0, :, (2 * g) * LANES:(2 * g + 1) * LANES],
                              q_ref[0, :, (2 * g + 1) * LANES:(2 * g + 2) * LANES]], axis=0)
        zero = jnp.zeros_like(qs)
        q_a = jnp.where(lo2, qs, zero)
        q_b = jnp.where(lo2, zero, qs)

        def cmp_probs(qx, k_ref, x):
            s = _dot_nt(qx, k_ref[0, g]) + cbias_ref[0, g, x]
            m = jnp.max(s, axis=-1, keepdims=True)
            e = jnp.where(s > 0.5 * NEG, jnp.exp(s - m), 0.0)
            l = jnp.sum(e, axis=-1, keepdims=True)
            return e / jnp.where(l > 0.0, l, 1.0)
        p_a = cmp_probs(q_a, kcA_ref, 0)
        p_b = cmp_probs(q_b, kcB_ref, 1)
        o_c = _dot(p_a.astype(_BF), vcA_ref[0, g]) + _dot(p_b.astype(_BF), vcB_ref[0, g])
        imp = p_a[0:QBLK] + p_a[QBLK:] + p_b[0:QBLK] + p_b[QBLK:]
        impb = imp + pltpu.roll(imp, LANES - n_sel, axis=1)

        picked = lax.cond(qb * QBLK >= TOP_N * SEL_BLOCK,
                          lambda: _top_extra(impb, cand, lane1_f, TOP_N - 3),
                          lambda: jnp.zeros((QBLK, LANES), _F32))
        chosen = forced | (few & (lane1 <= cur)) | ((picked > 0.5) & jnp.logical_not(few))
        sb_lo = jnp.where(chosen | (lane1 >= n_sel), 0.0, NEG)
        sb_hi = pltpu.roll(sb_lo, HEAD_DIM, axis=1)
        sb_lo2 = jnp.concatenate([sb_lo, sb_lo], axis=0).astype(_BF)
        sb_hi2 = jnp.concatenate([sb_hi, sb_hi], axis=0).astype(_BF)
        o_cmp.append(o_c)
        q_plain += [q_a, q_b]
        q_aug += [jnp.where(lo2, qs, sb_hi2), jnp.where(lo2, sb_lo2, qs)]

    chains = [(g, x) for g in range(N_KV) for x in range(2)]
    k_sel, v_sel, k_win, v_win = (ksA, ksB), (vsA, vsB), (kwA, kwB), (vwA, vwB)
    n_chunk = t_len // QBLK
    n_win = WINDOW // QBLK + 1

    w_first = jnp.maximum(qb - (n_win - 1), 0)
    w_rows = pl.ds(pl.multiple_of(w_first * QBLK, QBLK), n_win * QBLK)
    o_win = []
    for ch, (g, x) in enumerate(chains):
        s = _dot_nt(q_plain[ch], k_win[x][g, w_rows, :])
        parts = []
        for i in range(n_win):
            idx = qb - w_first - i
            parts.append(s[:, i * QBLK:(i + 1) * QBLK] + tiles_ref[jnp.where(idx < 0, n_win, idx), g, x])
        m = parts[0]
        for part in parts[1:]:
            m = jnp.maximum(m, part)
        m = jnp.max(m, axis=-1, keepdims=True)
        p = jnp.concatenate([jnp.exp(part - m).astype(_BF) for part in parts], axis=1)
        o_win.append(normalize(_dot(p, v_win[x][g, w_rows, :]), x))

    n_quart = 4
    qw = t_len // n_quart
    for qi in range(n_quart):
        @pl.when(qb * QBLK >= qi * qw)
        def _scores(qi=qi):
            for ch, (g, x) in enumerate(chains):
                s_scr[ch, :, qi * qw:(qi + 1) * qw] = _dot_nt(q_aug[ch], k_sel[x][g, qi * qw:(qi + 1) * qw, :])

    def biased(ch, c):
        g, x = chains[ch]
        return s_scr[ch, :, c * QBLK:(c + 1) * QBLK] + tiles_ref[jnp.minimum(qb - c, 2), g, x]

    m_scr[...] = jnp.full(m_scr.shape, NEG, _F32)
    for c in range(n_chunk):
        @pl.when(c <= qb)
        def _run_max(c=c):
            for ch in range(len(chains)):
                m_scr[ch] = jnp.maximum(m_scr[ch], biased(ch, c))
    m_sel = [jnp.max(m_scr[ch], axis=-1, keepdims=True) for ch in range(len(chains))]

    for c in range(n_chunk):
        @pl.when(c <= qb)
        def _probs(c=c):
            for ch in range(len(chains)):
                p_scr[ch, :, c * QBLK:(c + 1) * QBLK] = jnp.exp(biased(ch, c) - m_sel[ch]).astype(_BF)

    acc_scr[...] = jnp.zeros(acc_scr.shape, _F32)
    for qi in range(n_quart):
        @pl.when(qb * QBLK >= qi * qw)
        def _weighted(qi=qi):
            for ch, (g, x) in enumerate(chains):
                acc_scr[ch] += _dot(p_scr[ch, :, qi * qw:(qi + 1) * qw], v_sel[x][g, qi * qw:(qi + 1) * qw, :])

    for g in range(N_KV):
        o_c = o_cmp[g]
        o_s = normalize(acc_scr[2 * g], 0) + normalize(acc_scr[2 * g + 1], 1)
        o_w = o_win[2 * g] + o_win[2 * g + 1]
        for pr in range(2):
            rows = slice(pr * QBLK, (pr + 1) * QBLK)
            col = (2 * g + pr) * LANES
            o_ref[0, :, col:col + LANES] = (gexp[:, col:col + LANES] * o_c[rows]
                                            + gexp[:, ATT_WIDTH + col:ATT_WIDTH + col + LANES] * o_s[rows]
                                            + gexp[:, 2 * ATT_WIDTH + col:2 * ATT_WIDTH + col + LANES] * o_w[rows])


def _attn_prompt(q, sel, win, gates, kc, cbias, w):
    b, t, _ = sel.shape
    nq = t // QBLK
    assert t // SEL_BLOCK <= CMP_BLOCK and t % (4 * QBLK) == 0 and t >= WINDOW + QBLK
    kcA, kcB, vcA, vcB = kc
    full = lambda a: pl.BlockSpec(a.shape, lambda i, j: (0,) * a.ndim)
    slab = pl.BlockSpec((1, t, 256), lambda i, j: (i, 0, 0))
    kcs = pl.BlockSpec((1, N_KV, LANES, LANES), lambda i, j: (i, 0, 0, 0))
    n_chain = 2 * N_KV
    scratch = [pltpu.VMEM((N_KV, t, LANES), _BF)] * 8 + [
        pltpu.VMEM((n_chain, 2 * QBLK, t), _F32), pltpu.VMEM((n_chain, 2 * QBLK, t), _BF),
        pltpu.VMEM((n_chain, 2 * QBLK, LANES), _F32), pltpu.VMEM((n_chain, 2 * QBLK, LANES), _F32)]
    return pl.pallas_call(
        _attn_prompt_body,
        grid=(b, nq),
        in_specs=[pl.BlockSpec((1, QBLK, ATT_WIDTH), lambda i, j: (i, j, 0)), slab, slab,
                  pl.BlockSpec((1, QBLK, LANES), lambda i, j: (i, j, 0)), kcs, kcs, kcs, kcs,
                  pl.BlockSpec((1, N_KV, 2, 2 * QBLK, LANES), lambda i, j: (j, 0, 0, 0, 0)),
                  full(w['tiles']), full(w['egate'])],
        out_specs=pl.BlockSpec((1, QBLK, ATT_WIDTH), lambda i, j: (i, j, 0)),
        out_shape=jax.ShapeDtypeStruct((b, t, ATT_WIDTH), _F32),
        scratch_shapes=scratch,
        compiler_params=pltpu.CompilerParams(dimension_semantics=("arbitrary", "arbitrary"),
                                             vmem_limit_bytes=VMEM_LIMIT),
        name="attn_prompt",
    )(q, sel, win, gates, kcA, kcB, vcA, vcB, cbias, w['tiles'], w['egate'])


N_PICK = TOP_N - 3
N_SLOT = TOP_N - 1
PAGE_ROWS = 2 * PAGE_SIZE


def _sample_cmp_body(pt_ref, q_ref, cache_ref, bd_ref, pe_ref, kg_ref, g128_ref, sb_ref, oc_ref, pick_ref,
                     buf, sem):
    b = pl.program_id(0)
    nb = pl.num_programs(0)
    n_pages = pt_ref.shape[1]
    nj = 2 * n_pages
    slot = b % 2

    def page_copy(bb, p, sl):
        return pltpu.make_async_copy(cache_ref.at[pt_ref[bb, p]],
                                     buf.at[sl, pl.ds(pl.multiple_of(p * PAGE_ROWS, PAGE_ROWS), PAGE_ROWS), :],
                                     sem.at[sl])

    def fetch(bb, sl):
        def start(p, c):
            page_copy(bb, p, sl).start()
            return c
        lax.fori_loop(0, n_pages, start, 0)

    @pl.when(b == 0)
    def _prime():
        fetch(0, 0)

    @pl.when(b + 1 < nb)
    def _prefetch():
        fetch(b + 1, 1 - slot)

    def wait(p, c):
        page_copy(b, p, slot).wait()
        return c
    lax.fori_loop(0, n_pages, wait, 0)

    k, v = _compress_rows(lambda start: buf[slot, pl.ds(start, nj, stride=2 * SEL_BLOCK), :], nj, bd_ref, pe_ref)
    k = _group_rms(k, g128_ref[...], kg_ref[...]).astype(_BF)
    v = v.astype(_BF)
    s = _dot_nt(q_ref[0], k) + sb_ref[...]
    m = jnp.max(s, axis=-1, keepdims=True)
    e = jnp.exp(s - m)
    p = e / jnp.sum(e, axis=-1, keepdims=True)
    oc_ref[0] = _dot(p.astype(_BF), v)
    imp8 = p[:, 0:nj] + p[:, nj:]
    rows = [imp8[h:h + 1] for h in range(N_HEADS)]
    imp_g = [rows[g * GROUP] + rows[g * GROUP + 1] + rows[g * GROUP + 2] + rows[g * GROUP + 3] for g in range(N_KV)]
    imp = jnp.concatenate(imp_g + [jnp.full((8 - N_KV, nj), -1.0, _F32)], axis=0)
    lane = lax.broadcasted_iota(jnp.int32, (8, nj), 1)
    lane_f = lane.astype(_F32)
    v_c = jnp.where((lane >= 1) & (lane <= nj - 2), imp, -1.0)
    picks = jnp.zeros((8, nj), _F32)
    for i in range(N_PICK):
        mx = jnp.max(v_c, axis=-1, keepdims=True)
        first = jnp.min(jnp.where(v_c == mx, lane_f, 1e9), axis=-1, keepdims=True)
        picks = jnp.where(lane == i, first, picks)
        v_c = jnp.where(lane_f == first, -1.0, v_c)
    pick_ref[0] = picks.astype(jnp.int32)


def _sample_cmp(page_table, q8, cache, sbias, w):
    db, n_pages = page_table.shape
    nj = 2 * n_pages
    assert nj - 2 >= N_PICK
    full = lambda a: pl.BlockSpec(a.shape, lambda i, pt: (0,) * a.ndim)
    consts = (w['bd'], w['pe_t'], w['kc_gain'], w['g128'], sbias)
    grid_spec = pltpu.PrefetchScalarGridSpec(
        num_scalar_prefetch=1,
        grid=(db,),
        in_specs=[pl.BlockSpec((1, 8, LANES), lambda i, pt: (i, 0, 0)), pl.BlockSpec(memory_space=pl.ANY)]
        + [full(c) for c in consts],
        out_specs=[pl.BlockSpec((1, 8, LANES), lambda i, pt: (i, 0, 0)),
                   pl.BlockSpec((1, 8, nj), lambda i, pt: (i, 0, 0))],
        scratch_shapes=[pltpu.VMEM((2, n_pages * PAGE_ROWS, LANES), _F32), pltpu.SemaphoreType.DMA((2,))],
    )
    return pl.pallas_call(
        _sample_cmp_body,
        grid_spec=grid_spec,
        out_shape=[jax.ShapeDtypeStruct((db, 8, LANES), _F32), jax.ShapeDtypeStruct((db, 8, nj), jnp.int32)],
        compiler_params=pltpu.CompilerParams(dimension_semantics=("arbitrary",), vmem_limit_bytes=VMEM_LIMIT),
        name="sample_cmp",
    )(page_table, q8, cache, *consts)


SEL_ROWS = 512


def _sample_attn_body(pt_ref, pick_ref, q_ref, cache_ref, win_ref, nsel_ref, nwin_ref, gate_ref, oc_ref,
                      base_ref, near_ref, bwin_ref, o_ref, wout_ref, kb, sem):
    b = pl.program_id(0)
    nb = pl.num_programs(0)
    nj = 2 * pt_ref.shape[1]
    wlen = win_ref.shape[1]
    slot = b % 2
    new_row = N_SLOT * SEL_BLOCK

    def block_of(bb, g, s):
        if s == 0:
            return 0
        if s == N_SLOT - 1:
            return nj - 1
        return pick_ref[bb, g * N_PICK + (s - 1)]

    def block_copy(bb, g, s, sl):
        j = block_of(bb, g, s)
        page = pt_ref[bb, j // 2]
        off = pl.multiple_of((j % 2) * SEL_BLOCK, SEL_BLOCK)
        return pltpu.make_async_copy(cache_ref.at[page, pl.ds(off, SEL_BLOCK), :],
                                     kb.at[sl, g, pl.ds(s * SEL_BLOCK, SEL_BLOCK), :], sem.at[sl])

    def fetch(bb, sl):
        for g in range(N_KV):
            for s in range(N_SLOT):
                block_copy(bb, g, s, sl).start()

    @pl.when(b == 0)
    def _prime():
        kb[...] = jnp.zeros(kb.shape, _F32)
        fetch(0, 0)

    @pl.when(b + 1 < nb)
    def _prefetch():
        fetch(b + 1, 1 - slot)

    wout_ref[0, 0:wlen - 1, :] = win_ref[0, 1:wlen, :]
    wout_ref[0, wlen - 1:wlen, :] = nwin_ref[0]
    q8 = q_ref[0]
    sw = _dot_nt(q8, wout_ref[0, :, 0:128].astype(_BF)) + bwin_ref[...]
    mw = jnp.max(sw, axis=-1, keepdims=True)
    ew = jnp.exp(sw - mw)
    pw = ew / jnp.sum(ew, axis=-1, keepdims=True)
    o_w = _dot(pw.astype(_BF), wout_ref[0, :, 128:256].astype(_BF))

    for g in range(N_KV):
        for s in range(N_SLOT):
            block_copy(b, g, s, slot).wait()

    lane = lax.broadcasted_iota(jnp.int32, (8, SEL_ROWS), 1)
    lane_slot = lane // SEL_BLOCK
    o_sel = []
    for g in range(N_KV):
        kb[slot, g, new_row:new_row + 1, :] = nsel_ref[0]
        near = jnp.zeros((8, SEL_ROWS), jnp.bool_)
        for s in range(1, N_SLOT - 1):
            hit = jnp.full((8, SEL_ROWS), pick_ref[b, g * N_PICK + (s - 1)], jnp.int32) == nj - 2
            near = near | (hit & (lane_slot == s))
        bias = jnp.where(near, near_ref[...], base_ref[...])
        ss = _dot_nt(q8, kb[slot, g, :, 0:128].astype(_BF)) + bias
        ms = jnp.max(ss, axis=-1, keepdims=True)
        es = jnp.exp(ss - ms)
        ps = es / jnp.sum(es, axis=-1, keepdims=True)
        o_sel.append(_dot(ps.astype(_BF), kb[slot, g, :, 128:256].astype(_BF)))
    head_g = lax.broadcasted_iota(jnp.int32, (8, LANES), 0) // GROUP
    o_s = jnp.where(head_g == 0, o_sel[0], o_sel[1])
    gates = gate_ref[0]
    o_ref[0] = gates[:, 0:1] * oc_ref[0] + gates[:, 1:2] * o_s + gates[:, 2:3] * o_w


def _sample_attn(page_table, picks, q8, cache, win, new_sel, new_win, gates8, o_c, tabs):
    db, n_pages = page_table.shape
    wlen = win.shape[1]
    full = lambda a: pl.BlockSpec(a.shape, lambda i, pt, pk: (0,) * a.ndim)
    per_b = lambda r, c: pl.BlockSpec((1, r, c), lambda i, pt, pk: (i, 0, 0))
    base, near, bwin = tabs
    grid_spec = pltpu.PrefetchScalarGridSpec(
        num_scalar_prefetch=2,
        grid=(db,),
        in_specs=[per_b(8, LANES), pl.BlockSpec(memory_space=pl.ANY), per_b(wlen, 256), per_b(1, 256), per_b(1, 256),
                  per_b(8, LANES), per_b(8, LANES), full(base), full(near), full(bwin)],
        out_specs=[per_b(8, LANES), per_b(wlen, 256)],
        scratch_shapes=[pltpu.VMEM((2, N_KV, SEL_ROWS, 256), _F32), pltpu.SemaphoreType.DMA((2,))],
    )
    return pl.pallas_call(
        _sample_attn_body,
        grid_spec=grid_spec,
        out_shape=[jax.ShapeDtypeStruct((db, 8, LANES), _F32), jax.ShapeDtypeStruct((db, wlen, 256), _F32)],
        compiler_params=pltpu.CompilerParams(dimension_semantics=("arbitrary",), vmem_limit_bytes=VMEM_LIMIT),
        name="sample_attn",
    )(page_table, picks, q8, cache, win, new_sel, new_win, gates8, o_c, base, near, bwin)


def _rel_bucket(dist):
    n = jnp.maximum(dist, 0)
    max_exact = RP_BUCKETS // 2
    nf = jnp.maximum(n, 1).astype(_F32)
    large = max_exact + (jnp.log(nf / max_exact) / math.log(RP_MAX_DIST / max_exact)
                         * (RP_BUCKETS - max_exact)).astype(jnp.int32)
    large = jnp.minimum(large, RP_BUCKETS - 1)
    return jnp.where(n < max_exact, n, large)


def _bias_of(rel_bias, dist):
    onehot = _rel_bucket(dist)[..., None, None] == jnp.arange(RP_BUCKETS)[:, None]
    return jnp.sum(jnp.where(onehot, rel_bias, 0.0), axis=-2)


def _head_major(x, lead):
    n = x.shape[-2]
    x = x.reshape(lead + (QBLK, n, N_KV, 2, 2))
    nl = len(lead)
    x = jnp.transpose(x, tuple(range(nl)) + (nl + 2, nl + 4, nl + 3, nl, nl + 1))
    return x.reshape(lead + (N_KV, 2, 2 * QBLK, n))


def _prompt_tables(rel_bias, t_len):
    nq = t_len // QBLK
    ns = t_len // SEL_BLOCK
    ti = np.arange(QBLK)[:, None]
    ki = np.arange(QBLK)[None, :]
    n_idx = WINDOW // QBLK + 1
    dist = np.stack([i * QBLK + ti - ki for i in range(n_idx)])
    valid = dist >= 0
    valid[n_idx - 1] &= dist[n_idx - 1] < WINDOW
    tiles = jnp.where(valid[..., None], _bias_of(rel_bias, jnp.asarray(dist)), NEG)
    tiles = jnp.concatenate([tiles, jnp.full_like(tiles[:1], NEG)])
    tiles = _head_major(tiles, (n_idx + 1,))
    lane = np.arange(LANES)
    blk = np.where(lane < ns, 2 * lane, 2 * (lane - ns) + 1)
    dist_c = np.arange(t_len)[:, None] - (CMP_BLOCK * blk + CMP_BLOCK - 1)[None, :]
    valid_c = (dist_c >= 0) & (lane < 2 * ns)[None, :]
    cb = jnp.where(valid_c[..., None], _bias_of(rel_bias, jnp.asarray(dist_c)), NEG)
    cb = _head_major(cb.reshape(nq, QBLK, LANES, N_HEADS), (nq,))
    return tiles, cb


def _sample_tables(rel_bias, past_len, wlen):
    nj = past_len // SEL_BLOCK
    j = np.arange(nj)
    blk = np.concatenate([2 * j, 2 * j + 1])
    sbias = _bias_of(rel_bias, jnp.asarray(past_len - (CMP_BLOCK * blk + CMP_BLOCK - 1))).T
    lane = np.arange(SEL_ROWS)
    new_row = N_SLOT * SEL_BLOCK
    dist_b = np.where(lane < new_row - SEL_BLOCK, past_len, np.where(lane < new_row, SEL_BLOCK - lane % SEL_BLOCK, 0))
    base = jnp.where((lane <= new_row)[None, :], _bias_of(rel_bias, jnp.asarray(dist_b)).T, NEG)
    near = _bias_of(rel_bias, jnp.asarray(2 * SEL_BLOCK - lane % SEL_BLOCK)).T
    bwin = _bias_of(rel_bias, jnp.asarray(wlen - 1 - np.arange(wlen))).T
    return sbias, (base, near, bwin)


def _prep(norm_mix, w_in, q_norm, k_norm, cmp_pe, w_cmp, conv_w, out_norm, w_out, norm_mlp, w_up, w_down,
          norm_ple, w_ple_gate, w_ple_proj):
    w_in = w_in[0]
    o_kv = ATT_WIDTH
    o_g = o_kv + 6 * KV_WIDTH
    o_c = o_g + 3 * N_HEADS
    bd = jnp.einsum('gh,plde->lpgdhe', jnp.eye(N_KV, dtype=_F32), w_cmp[0]).reshape(CMP_BLOCK, 2, LANES, LANES)
    pe_t = jnp.transpose(jnp.tile(cmp_pe[0], (1, 1, N_KV)), (1, 0, 2))
    egate = np.zeros((LANES, 3 * ATT_WIDTH), np.float32)
    for br in range(3):
        for h in range(N_HEADS):
            egate[br * N_HEADS + h, br * ATT_WIDTH + h * HEAD_DIM:br * ATT_WIDTH + (h + 1) * HEAD_DIM] = 1.0
    ones = np.ones((HEAD_DIM, HEAD_DIM), np.float32)
    return {
        'norm_mix': norm_mix[0][None], 'wq': w_in[:, :o_kv].astype(_BF), 'wkv': w_in[:, o_kv:o_g].astype(_BF),
        'wg': jnp.pad(w_in[:, o_g:o_c], ((0, 0), (0, LANES - 3 * N_HEADS))).astype(_BF),
        'wc': w_in[:, o_c:].astype(_BF),
        'q_gain': jnp.tile(q_norm[0], N_HEADS)[None],
        'k_gain': jnp.stack([jnp.tile(k_norm[0, 1], N_KV), jnp.tile(k_norm[0, 2], N_KV)]),
        'kc_gain': jnp.tile(k_norm[0, 0], N_KV)[None],
        'g512': jnp.asarray(np.kron(np.eye(N_HEADS, dtype=np.float32), ones), _BF),
        'g128': jnp.asarray(np.kron(np.eye(N_KV, dtype=np.float32), ones), _BF),
        'bd': bd.astype(_BF), 'pe_t': pe_t,
        'egate': jnp.asarray(egate, _BF),
        'conv_w': conv_w[0], 'out_norm': out_norm[0][None], 'w_out': w_out[0].astype(_BF),
        'norm_mlp': norm_mlp[0][None], 'w_up': w_up[0].astype(_BF), 'w_down': w_down[0].astype(_BF),
        'norm_ple': norm_ple[0][None], 'w_ple_gate': w_ple_gate[0].astype(_BF),
        'w_ple_proj': w_ple_proj[0].astype(_BF),
    }


def kernel(x_prompt, x_sample, p_prompt, p_sample, cache_cmp_kv, cache_sel_kv, state_win_kv, state_conv, page_table, rel_bias, norm_mix, w_in, q_norm, k_norm, cmp_pe, w_cmp, conv_w, out_norm, w_out, norm_mlp, w_up, w_down, norm_ple, w_ple_gate, w_ple_proj):
    bp, tp, _ = x_prompt.shape
    db, ts, _ = x_sample.shape
    assert norm_mix.shape[0] == 1 and ts == 1 and tp >= WINDOW
    n_pages = page_table.shape[1]
    past_len = n_pages * PAGE_SIZE
    wlen = state_win_kv.shape[2]
    assert wlen == WINDOW and past_len >= 4 * SEL_BLOCK
    w = _prep(norm_mix, w_in, q_norm, k_norm, cmp_pe, w_cmp, conv_w, out_norm, w_out, norm_mlp, w_up, w_down,
              norm_ple, w_ple_gate, w_ple_proj)
    w['tiles'], cbias = _prompt_tables(rel_bias, tp)
    sbias, stabs = _sample_tables(rel_bias, past_len, wlen)
    kv6 = lambda a, b, t: a.reshape(1, b, t, 2, N_KV, HEAD_DIM)

    xp = x_prompt.reshape(bp * tp, D_MODEL)
    q, cmp_p, sel_p, win_p, gates, cb, u = _inproj(xp, w)
    seq = lambda a: a.reshape(bp, tp, a.shape[-1])
    kc = _compress_prompt(seq(cmp_p), w)
    o_att = _attn_prompt(seq(q), seq(sel_p), seq(win_p), seq(gates), kc, cbias, w)
    y_p = _tail(xp, o_att.reshape(bp * tp, ATT_WIDTH), cb, u, None, p_prompt[0].reshape(bp * tp, PLE_DIM), w, tp)

    xs = x_sample.reshape(db, D_MODEL)
    q_s, cmp_s, sel_s, win_s, gates_s, cb_s, u_s = _inproj(xs, w)
    qh = q_s.reshape(db, N_KV, GROUP, HEAD_DIM)
    zq = jnp.zeros_like(qh[:, 0])
    q8 = jnp.concatenate([jnp.concatenate([qh[:, 0], zq], axis=-1), jnp.concatenate([zq, qh[:, 1]], axis=-1)], axis=1)
    n_phys = cache_cmp_kv.shape[1]
    o_c, picks = _sample_cmp(page_table, q8, cache_cmp_kv.reshape(n_phys, PAGE_ROWS, LANES), sbias, w)
    picks = picks[:, :N_KV, :N_PICK].reshape(db, N_KV * N_PICK)
    gates8 = jnp.pad(jnp.transpose(gates_s[:, :3 * N_HEADS].reshape(db, 3, N_HEADS), (0, 2, 1)),
                     ((0, 0), (0, 0), (0, LANES - 3)))
    o8, win_new = _sample_attn(page_table, picks, q8, cache_sel_kv.reshape(n_phys, PAGE_SIZE, 256),
                               state_win_kv[0].reshape(db, wlen, 256), sel_s[:, None, :], win_s[:, None, :],
                               gates8, o_c, stabs)
    o8 = o8.reshape(db, N_KV, GROUP, N_KV, HEAD_DIM)
    o_att_s = jnp.stack([o8[:, 0, :, 0], o8[:, 1, :, 1]], axis=1).reshape(db, ATT_WIDTH)
    u_prev = jnp.transpose(state_conv[0], (1, 0, 2))
    y_s = _tail(xs, o_att_s, cb_s, u_s, u_prev, p_sample[0].reshape(db, PLE_DIM), w, None)

    return (y_p.reshape(bp, tp, D_MODEL), y_s.reshape(db, 1, D_MODEL),
            kv6(cmp_p, bp, tp), kv6(sel_p, bp, tp), kv6(seq(win_p)[:, tp - WINDOW:], bp, WINDOW),
            seq(u)[:, tp - 2:][None],
            kv6(cmp_s, db, 1), kv6(sel_s, db, 1), kv6(win_new, db, wlen),
            jnp.concatenate([state_conv[0][:, 1:], u_s[:, None, :]], axis=1)[None])
```

```python
import functools
import math

import jax
import jax.numpy as jnp
import numpy as np
from jax import lax
from jax.experimental import pallas as pl
from jax.experimental.pallas import tpu as pltpu

D_MODEL = 1024
HEAD_DIM = 64
N_HEADS = 8
N_KV = 2
GROUP = N_HEADS // N_KV
ATT_WIDTH = N_HEADS * HEAD_DIM
KV_WIDTH = N_KV * HEAD_DIM
CONV_DIM = D_MODEL - ATT_WIDTH
PAGE_SIZE = 128
CMP_BLOCK = 32
SEL_BLOCK = 64
TOP_N = 8
WINDOW = 512
RP_BUCKETS = 32
RP_MAX_DIST = 128
D_FF = 4 * D_MODEL
PLE_DIM = 256
SCALE = HEAD_DIM ** -0.5
NEG = -1e30
EPS = 1e-6

LANES = 128
QBLK = 128
VMEM_LIMIT = 56 * 1024 * 1024

_BF = jnp.bfloat16
_F32 = jnp.float32


def _dot(a, b):
    return jnp.dot(a, b, preferred_element_type=_F32)


def _dot_nt(a, b):
    return lax.dot_general(a, b, (((1,), (1,)), ((), ())), preferred_element_type=_F32)


def _rms_rows(x, gain):
    return x * lax.rsqrt(jnp.mean(x * x, axis=-1, keepdims=True) + EPS) * gain


def _group_rms(z, gmat, gain):
    ssq = _dot((z * z).astype(_BF), gmat) * (1.0 / HEAD_DIM)
    return z * lax.rsqrt(ssq + EPS) * gain


def _inproj_body(x_ref, nm_ref, wq_ref, wkv_ref, wg_ref, wc_ref, qg_ref, kg_ref, g512_ref, g128_ref,
                 q_ref, cmp_ref, sel_ref, win_ref, gate_ref, cb_ref, u_ref):
    a = _rms_rows(x_ref[...], nm_ref[...]).astype(_BF)
    zq = _dot(a, wq_ref[...])
    q_ref[...] = (_group_rms(zq, g512_ref[...], qg_ref[...]) * SCALE).astype(_BF)
    zkv = _dot(a, wkv_ref[...])
    cmp_ref[...] = zkv[:, 0:256]
    g128 = g128_ref[...]
    sel_ref[:, 0:128] = _group_rms(zkv[:, 256:384], g128, kg_ref[0:1, :])
    sel_ref[:, 128:256] = zkv[:, 384:512]
    win_ref[:, 0:128] = _group_rms(zkv[:, 512:640], g128, kg_ref[1:2, :])
    win_ref[:, 128:256] = zkv[:, 640:768]
    gate_ref[...] = jax.nn.sigmoid(_dot(a, wg_ref[...]))
    zc = _dot(a, wc_ref[...])
    cb_ref[...] = zc[:, 0:512]
    u_ref[...] = zc[:, 512:1024] * zc[:, 1024:1536]


def _inproj(x, w):
    n = x.shape[0]
    tm = min(256, n)
    row = lambda c: pl.BlockSpec((tm, c), lambda i: (i, 0))
    full = lambda a: pl.BlockSpec(a.shape, lambda i: (0,) * a.ndim)
    consts = (w['norm_mix'], w['wq'], w['wkv'], w['wg'], w['wc'], w['q_gain'], w['k_gain'], w['g512'], w['g128'])
    return pl.pallas_call(
        _inproj_body,
        grid=(n // tm,),
        in_specs=[row(D_MODEL)] + [full(c) for c in consts],
        out_specs=[row(512), row(256), row(256), row(256), row(128), row(512), row(512)],
        out_shape=[jax.ShapeDtypeStruct((n, 512), _BF)] + [jax.ShapeDtypeStruct((n, 256), _F32)] * 3
        + [jax.ShapeDtypeStruct((n, 128), _F32)] + [jax.ShapeDtypeStruct((n, 512), _F32)] * 2,
        compiler_params=pltpu.CompilerParams(dimension_semantics=("arbitrary",), vmem_limit_bytes=VMEM_LIMIT),
        name="inproj",
    )(x, *consts)


FF_CHUNK = 1024


def _tail_body(halo, h_ref, o_ref, cb_ref, u_ref, up_ref, p_ref, cw_ref, on_ref, nmlp_ref, nple_ref,
               wout_hbm, wup_hbm, wdn_hbm, wgate_hbm, wproj_hbm, y_ref,
               uext_ref, wout_ref, wup_ref, wdn_ref, wgate_ref, wproj_ref, wsem):
    @pl.when(pl.program_id(0) == 0)
    def _load_weights():
        copies = [pltpu.make_async_copy(src, dst, wsem.at[i]) for i, (src, dst) in enumerate(
            ((wout_hbm, wout_ref), (wup_hbm, wup_ref), (wdn_hbm, wdn_ref), (wgate_hbm, wgate_ref),
             (wproj_hbm, wproj_ref)))]
        for c in copies:
            c.start()
        for c in copies:
            c.wait()

    tm = h_ref.shape[0]
    u = u_ref[...]
    if halo:
        first = (pl.program_id(0) % halo) == 0
        prev = jnp.where(first, 0.0, up_ref[...])
        uext_ref[0:8, :] = prev
        uext_ref[8:tm + 8, :] = u
        u2 = uext_ref[6:tm + 6, :]
        u1 = uext_ref[7:tm + 7, :]
    else:
        u2 = up_ref[0]
        u1 = up_ref[1]
    yc = cw_ref[0:1, :] * u2 + cw_ref[1:2, :] * u1 + cw_ref[2:3, :] * u
    mix_a = _rms_rows(o_ref[...], on_ref[:, 0:ATT_WIDTH]).astype(_BF)
    mix_c = _rms_rows(cb_ref[...] * yc, on_ref[:, ATT_WIDTH:]).astype(_BF)
    h = h_ref[...] + _dot(mix_a, wout_ref[0:ATT_WIDTH, :]) + _dot(mix_c, wout_ref[ATT_WIDTH:, :])
    a = _rms_rows(h, nmlp_ref[...]).astype(_BF)
    y_ref[...] = h
    for c in range(D_FF // FF_CHUNK):
        t = jnp.maximum(_dot(a, wup_ref[:, c * FF_CHUNK:(c + 1) * FF_CHUNK]), 0.0)
        y_ref[...] += _dot((t * t).astype(_BF), wdn_ref[c * FF_CHUNK:(c + 1) * FF_CHUNK, :])
    h = y_ref[...]
    a = _rms_rows(h, nple_ref[...]).astype(_BF)
    gate = jax.nn.sigmoid(_dot(a, wgate_ref[...]))
    y_ref[...] = h + gate * _dot(p_ref[...].astype(_BF), wproj_ref[...])


def _tail(h, o_att, cb, u, u_prev, p, w, seq_len):
    n = h.shape[0]
    tm = min(256, n)
    row = lambda c: pl.BlockSpec((tm, c), lambda i: (i, 0))
    const = lambda a: pl.BlockSpec(a.shape, lambda i: (0,) * a.ndim)
    if seq_len is not None:
        halo = seq_len // tm
        up_spec = pl.BlockSpec((8, CONV_DIM), lambda i: (jnp.maximum(i * (tm // 8) - 1, 0), 0))
        up = u
    else:
        halo = 0
        up_spec = pl.BlockSpec((2, tm, CONV_DIM), lambda i: (0, i, 0))
        up = u_prev
    consts = (w['conv_w'], w['out_norm'], w['norm_mlp'], w['norm_ple'])
    mats = (w['w_out'], w['w_up'], w['w_down'], w['w_ple_gate'], w['w_ple_proj'])
    return pl.pallas_call(
        functools.partial(_tail_body, halo),
        grid=(n // tm,),
        in_specs=[row(D_MODEL), row(ATT_WIDTH), row(CONV_DIM), row(CONV_DIM), up_spec, row(PLE_DIM)]
        + [const(c) for c in consts] + [pl.BlockSpec(memory_space=pl.ANY)] * len(mats),
        out_specs=row(D_MODEL),
        out_shape=jax.ShapeDtypeStruct((n, D_MODEL), _F32),
        scratch_shapes=[pltpu.VMEM((tm + 8, CONV_DIM), _F32)] + [pltpu.VMEM(m.shape, _BF) for m in mats]
        + [pltpu.SemaphoreType.DMA((len(mats),))],
        compiler_params=pltpu.CompilerParams(dimension_semantics=("arbitrary",), vmem_limit_bytes=VMEM_LIMIT),
        name="tail",
    )(h, o_att, cb, u, up, p, *consts, *mats)


def _split_heads(x, lane_lo):
    xr = pltpu.roll(x, HEAD_DIM, axis=1)
    zero = jnp.zeros_like(x)
    a = (jnp.where(lane_lo, x, zero), jnp.where(lane_lo, xr, zero))
    b = (jnp.where(lane_lo, zero, xr), jnp.where(lane_lo, zero, x))
    return a, b


def _compress_rows(load, ns, bd_ref, pe_ref):
    acc = [jnp.zeros((2 * ns, LANES), _F32), jnp.zeros((2 * ns, LANES), _F32)]
    for l in range(CMP_BLOCK):
        for plane in range(2):
            x = jnp.concatenate([load(plane, l), load(plane, CMP_BLOCK + l)], axis=0)
            x = x + pe_ref[l, plane:plane + 1, :]
            acc[plane] = acc[plane] + _dot(x.astype(_BF), bd_ref[l, plane])
    return acc


def _compress_prompt_body(slab_ref, bd_ref, pe_ref, kg_ref, g128_ref, kA_ref, kB_ref, vA_ref, vB_ref):
    ns = slab_ref.shape[1] // (2 * SEL_BLOCK)
    k, v = _compress_rows(lambda plane, t0: slab_ref[0, pl.ds(2 * t0 + plane, ns, stride=2 * SEL_BLOCK), :],
                          ns, bd_ref, pe_ref)
    k = _group_rms(k, g128_ref[...], kg_ref[...])
    lane_lo = lax.broadcasted_iota(jnp.int32, k.shape, 1) < HEAD_DIM
    ka, kb = _split_heads(k, lane_lo)
    va, vb = _split_heads(v, lane_lo)
    pad = jnp.zeros((LANES - 2 * ns, LANES), _BF)
    for g in range(N_KV):
        for ref, val in ((kA_ref, ka[g]), (kB_ref, kb[g]), (vA_ref, va[g]), (vB_ref, vb[g])):
            ref[0, g, 0:2 * ns, :] = val.astype(_BF)
            if 2 * ns < LANES:
                ref[0, g, 2 * ns:, :] = pad


def _compress_prompt(slab, w):
    b, t, _ = slab.shape
    assert t % QBLK == 0 and t // CMP_BLOCK <= LANES
    slab = slab.reshape(b, 2 * t, LANES)
    full = lambda a: pl.BlockSpec(a.shape, lambda i: (0,) * a.ndim)
    consts = (w['bd'], w['pe_t'], w['kc_gain'], w['g128'])
    out = jax.ShapeDtypeStruct((b, N_KV, LANES, LANES), _BF)
    return pl.pallas_call(
        _compress_prompt_body,
        grid=(b,),
        in_specs=[pl.BlockSpec((1, 2 * t, LANES), lambda i: (i, 0, 0))] + [full(c) for c in consts],
        out_specs=[pl.BlockSpec((1, N_KV, LANES, LANES), lambda i: (i, 0, 0, 0))] * 4,
        out_shape=[out] * 4,
        compiler_params=pltpu.CompilerParams(dimension_semantics=("arbitrary",), vmem_limit_bytes=VMEM_LIMIT),
        name="compress_prompt",
    )(slab, *consts)


BUILD_ROWS = 256


def _top_extra(impb, cand, lane_f, n_extra):
    v = jnp.where(cand, impb, -1.0)
    picked = jnp.zeros(impb.shape, _F32)
    for _ in range(n_extra):
        mx = jnp.max(v, axis=-1, keepdims=True)
        first = jnp.min(jnp.where(v == mx, lane_f, 1e9), axis=-1, keepdims=True)
        hit = lane_f == first
        picked = jnp.where(hit, 1.0, picked)
        v = jnp.where(hit, -1.0, v)
    return picked


def _attn_prompt_body(q_ref, ks_ref, kw_ref, gate_ref, kcA_ref, kcB_ref, vcA_ref, vcB_ref, cbias_ref, tiles_ref,
                      eg_ref, o_ref, ksA, ksB, vsA, vsB, kwA, kwB, vwA, vwB, s_scr, p_scr, m_scr, acc_scr):
    qb = pl.program_id(1)
    t_len = ks_ref.shape[1]
    n_sel = t_len // SEL_BLOCK

    @pl.when(qb == 0)
    def _build():
        def chunk(c, carry):
            r0 = pl.multiple_of(c * BUILD_ROWS, BUILD_ROWS)
            rows = pl.ds(r0, BUILD_ROWS)
            lane = lax.broadcasted_iota(jnp.int32, (BUILD_ROWS, LANES), 1)
            blk = (r0 + lax.broadcasted_iota(jnp.int32, (BUILD_ROWS, LANES), 0)) // SEL_BLOCK
            lane_lo = lane < HEAD_DIM
            oh_hi = jnp.where(lane == blk + HEAD_DIM, 1.0, 0.0)
            oh_lo = jnp.where(lane == blk, 1.0, 0.0)
            one_hi = jnp.where(lane == HEAD_DIM, 1.0, 0.0)
            one_lo = jnp.where(lane == 0, 1.0, 0.0)
            for src, k_a, k_b, v_a, v_b, onehot in ((ks_ref, ksA, ksB, vsA, vsB, True),
                                                    (kw_ref, kwA, kwB, vwA, vwB, False)):
                ka, kb = _split_heads(src[0, rows, 0:128], lane_lo)
                va, vb = _split_heads(src[0, rows, 128:256], lane_lo)
                for g in range(N_KV):
                    if onehot:
                        k_a[g, rows, :] = jnp.where(lane_lo, ka[g], oh_hi).astype(_BF)
                        k_b[g, rows, :] = jnp.where(lane_lo, oh_lo, kb[g]).astype(_BF)
                    else:
                        k_a[g, rows, :] = ka[g].astype(_BF)
                        k_b[g, rows, :] = kb[g].astype(_BF)
                    v_a[g, rows, :] = jnp.where(lane_lo, va[g], one_hi).astype(_BF)
                    v_b[g, rows, :] = jnp.where(lane_lo, one_lo, vb[g]).astype(_BF)
            return carry
        lax.fori_loop(0, t_len // BUILD_ROWS, chunk, 0)
        p_scr[...] = jnp.zeros(p_scr.shape, _BF)

    lane2 = lax.broadcasted_iota(jnp.int32, (2 * QBLK, LANES), 1)
    lo2 = lane2 < HEAD_DIM
    lane1 = lax.broadcasted_iota(jnp.int32, (QBLK, LANES), 1)
    lane1_f = lane1.astype(_F32)
    tpos = qb * QBLK + lax.broadcasted_iota(jnp.int32, (QBLK, LANES), 0)
    cur = tpos // SEL_BLOCK
    forced = (lane1 == 0) | (lane1 == cur) | (lane1 == cur - 1)
    cand = (lane1 >= 1) & (lane1 <= cur - 2)
    few = cur <= TOP_N - 1

    gates = gate_ref[0]
    g_hi = gates.astype(_BF)
    g_lo = (gates - g_hi.astype(_F32)).astype(_BF)
    gexp = _dot(g_hi, eg_ref[...]) + _dot(g_lo, eg_ref[...])

    def normalize(acc, x):
        l = jnp.sum(jnp.where(lane2 == (HEAD_DIM if x == 0 else 0), acc, 0.0), axis=-1, keepdims=True)
        keep = lo2 if x == 0 else jnp.logical_not(lo2)
        return jnp.where(keep, acc / l, 0.0)

    o_cmp, q_plain, q_aug = [], [], []
    for g in range(N_KV):
        qs = jnp.concatenate([q_ref[0, :, (2 * g) * LANES:(2 * g + 1) * LANES],
                              q_ref[0, :, (2 * g + 1) * LANES:(2 * g + 2) * LANES]], axis=0)
        zero = jnp.zeros_like(qs)
        q_a = jnp.where(lo2, qs, zero)
        q_b = jnp.where(lo2, zero, qs)

        def cmp_probs(qx, k_ref, x):
            s = _dot_nt(qx, k_ref[0, g]) + cbias_ref[0, g, x]
            m = jnp.max(s, axis=-1, keepdims=True)
            e = jnp.where(s > 0.5 * NEG, jnp.exp(s - m), 0.0)
            l = jnp.sum(e, axis=-1, keepdims=True)
            return e / jnp.where(l > 0.0, l, 1.0)
        p_a = cmp_probs(q_a, kcA_ref, 0)
        p_b = cmp_probs(q_b, kcB_ref, 1)
        o_c = _dot(p_a.astype(_BF), vcA_ref[0, g]) + _dot(p_b.astype(_BF), vcB_ref[0, g])
        imp = p_a[0:QBLK] + p_a[QBLK:] + p_b[0:QBLK] + p_b[QBLK:]
        impb = imp + pltpu.roll(imp, LANES - n_sel, axis=1)

        picked = lax.cond(qb * QBLK >= TOP_N * SEL_BLOCK,
                          lambda: _top_extra(impb, cand, lane1_f, TOP_N - 3),
                          lambda: jnp.zeros((QBLK, LANES), _F32))
        chosen = forced | (few & (lane1 <= cur)) | ((picked > 0.5) & jnp.logical_not(few))
        sb_lo = jnp.where(chosen | (lane1 >= n_sel), 0.0, NEG)
        sb_hi = pltpu.roll(sb_lo, HEAD_DIM, axis=1)
        sb_lo2 = jnp.concatenate([sb_lo, sb_lo], axis=0).astype(_BF)
        sb_hi2 = jnp.concatenate([sb_hi, sb_hi], axis=0).astype(_BF)
        o_cmp.append(o_c)
        q_plain += [q_a, q_b]
        q_aug += [jnp.where(lo2, qs, sb_hi2), jnp.where(lo2, sb_lo2, qs)]

    chains = [(g, x) for g in range(N_KV) for x in range(2)]
    k_sel, v_sel, k_win, v_win = (ksA, ksB), (vsA, vsB), (kwA, kwB), (vwA, vwB)
    n_chunk = t_len // QBLK
    n_win = WINDOW // QBLK + 1

    w_first = jnp.maximum(qb - (n_win - 1), 0)
    w_rows = pl.ds(pl.multiple_of(w_first * QBLK, QBLK), n_win * QBLK)
    o_win = []
    for ch, (g, x) in enumerate(chains):
        s = _dot_nt(q_plain[ch], k_win[x][g, w_rows, :])
        parts = []
        for i in range(n_win):
            idx = qb - w_first - i
            parts.append(s[:, i * QBLK:(i + 1) * QBLK] + tiles_ref[jnp.where(idx < 0, n_win, idx), g, x])
        m = parts[0]
        for part in parts[1:]:
            m = jnp.maximum(m, part)
        m = jnp.max(m, axis=-1, keepdims=True)
        p = jnp.concatenate([jnp.exp(part - m).astype(_BF) for part in parts], axis=1)
        o_win.append(normalize(_dot(p, v_win[x][g, w_rows, :]), x))

    n_quart = 4
    qw = t_len // n_quart
    for qi in range(n_quart):
        @pl.when(qb * QBLK >= qi * qw)
        def _scores(qi=qi):
            for ch, (g, x) in enumerate(chains):
                s_scr[ch, :, qi * qw:(qi + 1) * qw] = _dot_nt(q_aug[ch], k_sel[x][g, qi * qw:(qi + 1) * qw, :])

    def biased(ch, c):
        g, x = chains[ch]
        return s_scr[ch, :, c * QBLK:(c + 1) * QBLK] + tiles_ref[jnp.minimum(qb - c, 2), g, x]

    m_scr[...] = jnp.full(m_scr.shape, NEG, _F32)
    for c in range(n_chunk):
        @pl.when(c <= qb)
        def _run_max(c=c):
            for ch in range(len(chains)):
                m_scr[ch] = jnp.maximum(m_scr[ch], biased(ch, c))
    m_sel = [jnp.max(m_scr[ch], axis=-1, keepdims=True) for ch in range(len(chains))]

    for c in range(n_chunk):
        @pl.when(c <= qb)
        def _probs(c=c):
            for ch in range(len(chains)):
                p_scr[ch, :, c * QBLK:(c + 1) * QBLK] = jnp.exp(biased(ch, c) - m_sel[ch]).astype(_BF)

    acc_scr[...] = jnp.zeros(acc_scr.shape, _F32)
    for qi in range(n_quart):
        @pl.when(qb * QBLK >= qi * qw)
        def _weighted(qi=qi):
            for ch, (g, x) in enumerate(chains):
                acc_scr[ch] += _dot(p_scr[ch, :, qi * qw:(qi + 1) * qw], v_sel[x][g, qi * qw:(qi + 1) * qw, :])

    for g in range(N_KV):
        o_c = o_cmp[g]
        o_s = normalize(acc_scr[2 * g], 0) + normalize(acc_scr[2 * g + 1], 1)
        o_w = o_win[2 * g] + o_win[2 * g + 1]
        for pr in range(2):
            rows = slice(pr * QBLK, (pr + 1) * QBLK)
            col = (2 * g + pr) * LANES
            o_ref[0, :, col:col + LANES] = (gexp[:, col:col + LANES] * o_c[rows]
                                            + gexp[:, ATT_WIDTH + col:ATT_WIDTH + col + LANES] * o_s[rows]
                                            + gexp[:, 2 * ATT_WIDTH + col:2 * ATT_WIDTH + col + LANES] * o_w[rows])


def _attn_prompt(q, sel, win, gates, kc, cbias, w):
    b, t, _ = sel.shape
    nq = t // QBLK
    assert t // SEL_BLOCK <= CMP_BLOCK and t % (4 * QBLK) == 0 and t >= WINDOW + QBLK
    kcA, kcB, vcA, vcB = kc
    full = lambda a: pl.BlockSpec(a.shape, lambda i, j: (0,) * a.ndim)
    slab = pl.BlockSpec((1, t, 256), lambda i, j: (i, 0, 0))
    kcs = pl.BlockSpec((1, N_KV, LANES, LANES), lambda i, j: (i, 0, 0, 0))
    n_chain = 2 * N_KV
    scratch = [pltpu.VMEM((N_KV, t, LANES), _BF)] * 8 + [
        pltpu.VMEM((n_chain, 2 * QBLK, t), _F32), pltpu.VMEM((n_chain, 2 * QBLK, t), _BF),
        pltpu.VMEM((n_chain, 2 * QBLK, LANES), _F32), pltpu.VMEM((n_chain, 2 * QBLK, LANES), _F32)]
    return pl.pallas_call(
        _attn_prompt_body,
        grid=(b, nq),
        in_specs=[pl.BlockSpec((1, QBLK, ATT_WIDTH), lambda i, j: (i, j, 0)), slab, slab,
                  pl.BlockSpec((1, QBLK, LANES), lambda i, j: (i, j, 0)), kcs, kcs, kcs, kcs,
                  pl.BlockSpec((1, N_KV, 2, 2 * QBLK, LANES), lambda i, j: (j, 0, 0, 0, 0)),
                  full(w['tiles']), full(w['egate'])],
        out_specs=pl.BlockSpec((1, QBLK, ATT_WIDTH), lambda i, j: (i, j, 0)),
        out_shape=jax.ShapeDtypeStruct((b, t, ATT_WIDTH), _F32),
        scratch_shapes=scratch,
        compiler_params=pltpu.CompilerParams(dimension_semantics=("arbitrary", "arbitrary"),
                                             vmem_limit_bytes=VMEM_LIMIT),
        name="attn_prompt",
    )(q, sel, win, gates, kcA, kcB, vcA, vcB, cbias, w['tiles'], w['egate'])


N_PICK = TOP_N - 3
N_SLOT = TOP_N - 1
PAGE_ROWS = 2 * KV_WIDTH


def _sample_cmp_body(pt_ref, q_ref, cache_ref, bd_ref, pe_ref, kg_ref, g128_ref, sb_ref, oc_ref, pick_ref,
                     buf, tk, tv, sem):
    b = pl.program_id(0)
    nb = pl.num_programs(0)
    n_pages = pt_ref.shape[1]
    nj = 2 * n_pages
    slot = b % 2

    def page_copy(bb, p, sl):
        return pltpu.make_async_copy(cache_ref.at[pt_ref[bb, p]],
                                     buf.at[sl, pl.ds(pl.multiple_of(p * PAGE_ROWS, PAGE_ROWS), PAGE_ROWS), :],
                                     sem.at[sl])

    def fetch(bb, sl):
        def start(p, c):
            page_copy(bb, p, sl).start()
            return c
        lax.fori_loop(0, n_pages, start, 0)

    @pl.when(b == 0)
    def _prime():
        fetch(0, 0)

    @pl.when(b + 1 < nb)
    def _prefetch():
        fetch(b + 1, 1 - slot)

    def wait(p, c):
        page_copy(b, p, slot).wait()
        return c
    lax.fori_loop(0, n_pages, wait, 0)

    def to_rows(p, c):
        src = pl.multiple_of(p * PAGE_ROWS, PAGE_ROWS)
        dst = pl.ds(pl.multiple_of(p * PAGE_SIZE, PAGE_SIZE), PAGE_SIZE)
        tk[dst, :] = buf[slot, pl.ds(src, KV_WIDTH), :].T
        tv[dst, :] = buf[slot, pl.ds(src + KV_WIDTH, KV_WIDTH), :].T
        return c
    lax.fori_loop(0, n_pages, to_rows, 0)
    planes = (tk, tv)
    k, v = _compress_rows(lambda plane, t0: planes[plane][pl.ds(t0, nj, stride=SEL_BLOCK), :], nj, bd_ref, pe_ref)
    k = _group_rms(k, g128_ref[...], kg_ref[...]).astype(_BF)
    v = v.astype(_BF)
    s = _dot_nt(q_ref[0], k) + sb_ref[...]
    m = jnp.max(s, axis=-1, keepdims=True)
    e = jnp.exp(s - m)
    p = e / jnp.sum(e, axis=-1, keepdims=True)
    oc_ref[0] = _dot(p.astype(_BF), v)
    imp8 = p[:, 0:nj] + p[:, nj:]
    rows = [imp8[h:h + 1] for h in range(N_HEADS)]
    imp_g = [rows[g * GROUP] + rows[g * GROUP + 1] + rows[g * GROUP + 2] + rows[g * GROUP + 3] for g in range(N_KV)]
    imp = jnp.concatenate(imp_g + [jnp.full((8 - N_KV, nj), -1.0, _F32)], axis=0)
    lane = lax.broadcasted_iota(jnp.int32, (8, nj), 1)
    lane_f = lane.astype(_F32)
    v_c = jnp.where((lane >= 1) & (lane <= nj - 2), imp, -1.0)
    picks = jnp.zeros((8, nj), _F32)
    for i in range(N_PICK):
        mx = jnp.max(v_c, axis=-1, keepdims=True)
        first = jnp.min(jnp.where(v_c == mx, lane_f, 1e9), axis=-1, keepdims=True)
        picks = jnp.where(lane == i, first, picks)
        v_c = jnp.where(lane_f == first, -1.0, v_c)
    pick_ref[0] = picks.astype(jnp.int32)


def _sample_cmp(page_table, q8, cache, sbias, w):
    db, n_pages = page_table.shape
    nj = 2 * n_pages
    assert nj - 2 >= N_PICK
    full = lambda a: pl.BlockSpec(a.shape, lambda i, pt: (0,) * a.ndim)
    consts = (w['bd'], w['pe_t'], w['kc_gain'], w['g128'], sbias)
    grid_spec = pltpu.PrefetchScalarGridSpec(
        num_scalar_prefetch=1,
        grid=(db,),
        in_specs=[pl.BlockSpec((1, 8, LANES), lambda i, pt: (i, 0, 0)), pl.BlockSpec(memory_space=pl.ANY)]
        + [full(c) for c in consts],
        out_specs=[pl.BlockSpec((1, 8, LANES), lambda i, pt: (i, 0, 0)),
                   pl.BlockSpec((1, 8, nj), lambda i, pt: (i, 0, 0))],
        scratch_shapes=[pltpu.VMEM((2, n_pages * PAGE_ROWS, LANES), _F32),
                        pltpu.VMEM((n_pages * PAGE_SIZE, LANES), _F32), pltpu.VMEM((n_pages * PAGE_SIZE, LANES), _F32),
                        pltpu.SemaphoreType.DMA((2,))],
    )
    return pl.pallas_call(
        _sample_cmp_body,
        grid_spec=grid_spec,
        out_shape=[jax.ShapeDtypeStruct((db, 8, LANES), _F32), jax.ShapeDtypeStruct((db, 8, nj), jnp.int32)],
        compiler_params=pltpu.CompilerParams(dimension_semantics=("arbitrary",), vmem_limit_bytes=VMEM_LIMIT),
        name="sample_cmp",
    )(page_table, q8, cache, *consts)


def _sample_attn_body(pt_ref, pick_ref, q_ref, cache_ref, win_ref, nsel_ref, nwin_ref, ncol_ref, gate_ref, oc_ref,
                      bsel_ref, b0_ref, bwin_ref, o_ref, wout_ref, kb, sem):
    b = pl.program_id(0)
    nb = pl.num_programs(0)
    nj = 2 * pt_ref.shape[1]
    wlen = win_ref.shape[2]
    slot = b % 2

    def block_of(bb, g, s):
        if s == 0:
            return 0
        if s == N_SLOT - 1:
            return nj - 1
        return pick_ref[bb, g * N_PICK + (s - 1)]

    def tile_copy(bb, g, s, plane, sl):
        page = pt_ref[bb, block_of(bb, g, s) // 2]
        return pltpu.make_async_copy(cache_ref.at[page, plane * N_KV + g],
                                     kb.at[sl, g, plane, :, pl.ds(s * PAGE_SIZE, PAGE_SIZE)], sem.at[sl])

    def for_tiles(fn):
        for g in range(N_KV):
            for s in range(N_SLOT):
                for plane in range(2):
                    fn(g, s, plane)

    @pl.when(b == 0)
    def _prime():
        for_tiles(lambda g, s, plane: tile_copy(0, g, s, plane, 0).start())

    @pl.when(b + 1 < nb)
    def _prefetch():
        for_tiles(lambda g, s, plane: tile_copy(b + 1, g, s, plane, 1 - slot).start())

    q8 = q_ref[0]
    q8f = q8.astype(_F32)
    head_g = lax.broadcasted_iota(jnp.int32, (8, HEAD_DIM), 0) // GROUP
    own_half = lambda x: jnp.where(head_g == 0, x[:, 0:HEAD_DIM], x[:, HEAD_DIM:])
    b0 = b0_ref[:, 0:1]

    x = win_ref[0]
    lane_w = lax.broadcasted_iota(jnp.int32, x.shape, 1)
    wout_ref[0] = jnp.where(lane_w == wlen - 1, ncol_ref[0], pltpu.roll(x, wlen - 1, axis=1))
    sw = _dot(q8, x[0:KV_WIDTH, :].astype(_BF)) + bwin_ref[...]
    sw_new = jnp.sum(q8f * nwin_ref[0][:, 0:KV_WIDTH], axis=-1, keepdims=True) + b0
    mw = jnp.maximum(jnp.max(sw, axis=-1, keepdims=True), sw_new)
    ew = jnp.exp(sw - mw)
    ew_new = jnp.exp(sw_new - mw)
    o_w = _dot_nt(ew.astype(_BF), x[KV_WIDTH:, :].astype(_BF)) + ew_new * nwin_ref[0][:, KV_WIDTH:]
    o_w = own_half(o_w / (jnp.sum(ew, axis=-1, keepdims=True) + ew_new))

    for_tiles(lambda g, s, plane: tile_copy(b, g, s, plane, slot).wait())

    ss_new = jnp.sum(q8f * nsel_ref[0][:, 0:KV_WIDTH], axis=-1, keepdims=True) + b0
    lane_half = lax.broadcasted_iota(jnp.int32, (8, PAGE_SIZE), 1) // SEL_BLOCK
    o_sel = []
    for g in range(N_KV):
        pieces = []
        for s in range(N_SLOT):
            j = jnp.full((8, PAGE_SIZE), block_of(b, g, s), jnp.int32)
            if s == 0:
                tab = bsel_ref[0]
            elif s == N_SLOT - 1:
                tab = bsel_ref[2]
            else:
                tab = jnp.where(j == nj - 2, bsel_ref[1], bsel_ref[0])
            pieces.append(jnp.where(lane_half == j % 2, tab, NEG))
        ss = _dot(q8[:, g * HEAD_DIM:(g + 1) * HEAD_DIM], kb[slot, g, 0].astype(_BF)) + jnp.concatenate(pieces, axis=1)
        ms = jnp.maximum(jnp.max(ss, axis=-1, keepdims=True), ss_new)
        es = jnp.exp(ss - ms)
        es_new = jnp.exp(ss_new - ms)
        v_new = nsel_ref[0][:, KV_WIDTH + g * HEAD_DIM:KV_WIDTH + (g + 1) * HEAD_DIM]
        o_g = _dot_nt(es.astype(_BF), kb[slot, g, 1].astype(_BF)) + es_new * v_new
        o_sel.append(o_g / (jnp.sum(es, axis=-1, keepdims=True) + es_new))
    o_s = jnp.where(head_g == 0, o_sel[0], o_sel[1])
    gates = gate_ref[0]
    o_ref[0] = gates[:, 0:1] * own_half(oc_ref[0]) + gates[:, 1:2] * o_s + gates[:, 2:3] * o_w


def _sample_attn(page_table, picks, q8, cache, win, new_sel, new_win, gates8, o_c, tabs):
    db, n_pages = page_table.shape
    wlen = win.shape[2]
    full = lambda a: pl.BlockSpec(a.shape, lambda i, pt, pk: (0,) * a.ndim)
    per_b = lambda r, c: pl.BlockSpec((1, r, c), lambda i, pt, pk: (i, 0, 0))
    bsel, b0, bwin = tabs
    grid_spec = pltpu.PrefetchScalarGridSpec(
        num_scalar_prefetch=2,
        grid=(db,),
        in_specs=[per_b(8, LANES), pl.BlockSpec(memory_space=pl.ANY), per_b(2 * KV_WIDTH, wlen), per_b(1, 256),
                  per_b(1, 256), per_b(2 * KV_WIDTH, 1), per_b(8, LANES), per_b(8, LANES),
                  full(bsel), full(b0), full(bwin)],
        out_specs=[per_b(8, HEAD_DIM), per_b(2 * KV_WIDTH, wlen)],
        scratch_shapes=[pltpu.VMEM((2, N_KV, 2, HEAD_DIM, N_SLOT * PAGE_SIZE), _F32), pltpu.SemaphoreType.DMA((2,))],
    )
    return pl.pallas_call(
        _sample_attn_body,
        grid_spec=grid_spec,
        out_shape=[jax.ShapeDtypeStruct((db, 8, HEAD_DIM), _F32), jax.ShapeDtypeStruct((db, 2 * KV_WIDTH, wlen), _F32)],
        compiler_params=pltpu.CompilerParams(dimension_semantics=("arbitrary",), vmem_limit_bytes=VMEM_LIMIT),
        name="sample_attn",
    )(page_table, picks, q8, cache, win, new_sel, new_win, new_win[:, 0, :, None], gates8, o_c, bsel, b0, bwin)


def _rel_bucket(dist):
    n = jnp.maximum(dist, 0)
    max_exact = RP_BUCKETS // 2
    nf = jnp.maximum(n, 1).astype(_F32)
    large = max_exact + (jnp.log(nf / max_exact) / math.log(RP_MAX_DIST / max_exact)
                         * (RP_BUCKETS - max_exact)).astype(jnp.int32)
    large = jnp.minimum(large, RP_BUCKETS - 1)
    return jnp.where(n < max_exact, n, large)


def _bias_of(rel_bias, dist):
    onehot = _rel_bucket(dist)[..., None, None] == jnp.arange(RP_BUCKETS)[:, None]
    return jnp.sum(jnp.where(onehot, rel_bias, 0.0), axis=-2)


def _head_major(x, lead):
    n = x.shape[-2]
    x = x.reshape(lead + (QBLK, n, N_KV, 2, 2))
    nl = len(lead)
    x = jnp.transpose(x, tuple(range(nl)) + (nl + 2, nl + 4, nl + 3, nl, nl + 1))
    return x.reshape(lead + (N_KV, 2, 2 * QBLK, n))


def _prompt_tables(rel_bias, t_len):
    nq = t_len // QBLK
    ns = t_len // SEL_BLOCK
    ti = np.arange(QBLK)[:, None]
    ki = np.arange(QBLK)[None, :]
    n_idx = WINDOW // QBLK + 1
    dist = np.stack([i * QBLK + ti - ki for i in range(n_idx)])
    valid = dist >= 0
    valid[n_idx - 1] &= dist[n_idx - 1] < WINDOW
    tiles = jnp.where(valid[..., None], _bias_of(rel_bias, jnp.asarray(dist)), NEG)
    tiles = jnp.concatenate([tiles, jnp.full_like(tiles[:1], NEG)])
    tiles = _head_major(tiles, (n_idx + 1,))
    lane = np.arange(LANES)
    blk = np.where(lane < ns, 2 * lane, 2 * (lane - ns) + 1)
    dist_c = np.arange(t_len)[:, None] - (CMP_BLOCK * blk + CMP_BLOCK - 1)[None, :]
    valid_c = (dist_c >= 0) & (lane < 2 * ns)[None, :]
    cb = jnp.where(valid_c[..., None], _bias_of(rel_bias, jnp.asarray(dist_c)), NEG)
    cb = _head_major(cb.reshape(nq, QBLK, LANES, N_HEADS), (nq,))
    return tiles, cb


def _sample_tables(rel_bias, past_len, wlen):
    nj = past_len // SEL_BLOCK
    j = np.arange(nj)
    blk = np.concatenate([2 * j, 2 * j + 1])
    sbias = _bias_of(rel_bias, jnp.asarray(past_len - (CMP_BLOCK * blk + CMP_BLOCK - 1))).T
    pos = np.arange(PAGE_SIZE) % SEL_BLOCK
    bsel = jnp.stack([_bias_of(rel_bias, jnp.asarray(np.full(PAGE_SIZE, past_len))).T,
                      _bias_of(rel_bias, jnp.asarray(2 * SEL_BLOCK - pos)).T,
                      _bias_of(rel_bias, jnp.asarray(SEL_BLOCK - pos)).T])
    b0 = _bias_of(rel_bias, jnp.zeros((LANES,), jnp.int32)).T
    tok = np.arange(wlen)
    bwin = jnp.where((tok >= 1)[None, :], _bias_of(rel_bias, jnp.asarray(wlen - tok)).T, NEG)
    return sbias, (bsel, b0, bwin)


def _prep(norm_mix, w_in, q_norm, k_norm, cmp_pe, w_cmp, conv_w, out_norm, w_out, norm_mlp, w_up, w_down,
          norm_ple, w_ple_gate, w_ple_proj):
    w_in = w_in[0]
    o_kv = ATT_WIDTH
    o_g = o_kv + 6 * KV_WIDTH
    o_c = o_g + 3 * N_HEADS
    bd = jnp.einsum('gh,plde->lpgdhe', jnp.eye(N_KV, dtype=_F32), w_cmp[0]).reshape(CMP_BLOCK, 2, LANES, LANES)
    pe_t = jnp.transpose(jnp.tile(cmp_pe[0], (1, 1, N_KV)), (1, 0, 2))
    egate = np.zeros((LANES, 3 * ATT_WIDTH), np.float32)
    for br in range(3):
        for h in range(N_HEADS):
            egate[br * N_HEADS + h, br * ATT_WIDTH + h * HEAD_DIM:br * ATT_WIDTH + (h + 1) * HEAD_DIM] = 1.0
    ones = np.ones((HEAD_DIM, HEAD_DIM), np.float32)
    return {
        'norm_mix': norm_mix[0][None], 'wq': w_in[:, :o_kv].astype(_BF), 'wkv': w_in[:, o_kv:o_g].astype(_BF),
        'wg': jnp.pad(w_in[:, o_g:o_c], ((0, 0), (0, LANES - 3 * N_HEADS))).astype(_BF),
        'wc': w_in[:, o_c:].astype(_BF),
        'q_gain': jnp.tile(q_norm[0], N_HEADS)[None],
        'k_gain': jnp.stack([jnp.tile(k_norm[0, 1], N_KV), jnp.tile(k_norm[0, 2], N_KV)]),
        'kc_gain': jnp.tile(k_norm[0, 0], N_KV)[None],
        'g512': jnp.asarray(np.kron(np.eye(N_HEADS, dtype=np.float32), ones), _BF),
        'g128': jnp.asarray(np.kron(np.eye(N_KV, dtype=np.float32), ones), _BF),
        'bd': bd.astype(_BF), 'pe_t': pe_t,
        'egate': jnp.asarray(egate, _BF),
        'conv_w': conv_w[0], 'out_norm': out_norm[0][None], 'w_out': w_out[0].astype(_BF),
        'norm_mlp': norm_mlp[0][None], 'w_up': w_up[0].astype(_BF), 'w_down': w_down[0].astype(_BF),
        'norm_ple': norm_ple[0][None], 'w_ple_gate': w_ple_gate[0].astype(_BF),
        'w_ple_proj': w_ple_proj[0].astype(_BF),
    }


def kernel(x_prompt, x_sample, p_prompt, p_sample, cache_cmp_kv, cache_sel_kv, state_win_kv, state_conv, page_table, rel_bias, norm_mix, w_in, q_norm, k_norm, cmp_pe, w_cmp, conv_w, out_norm, w_out, norm_mlp, w_up, w_down, norm_ple, w_ple_gate, w_ple_proj):
    bp, tp, _ = x_prompt.shape
    db, ts, _ = x_sample.shape
    assert norm_mix.shape[0] == 1 and ts == 1 and tp >= WINDOW
    n_pages = page_table.shape[1]
    past_len = n_pages * PAGE_SIZE
    wlen = state_win_kv.shape[2]
    assert wlen == WINDOW and past_len >= 4 * SEL_BLOCK
    w = _prep(norm_mix, w_in, q_norm, k_norm, cmp_pe, w_cmp, conv_w, out_norm, w_out, norm_mlp, w_up, w_down,
              norm_ple, w_ple_gate, w_ple_proj)
    w['tiles'], cbias = _prompt_tables(rel_bias, tp)
    sbias, stabs = _sample_tables(rel_bias, past_len, wlen)
    kv6 = lambda a, b, t: a.reshape(1, b, t, 2, N_KV, HEAD_DIM)

    xp = x_prompt.reshape(bp * tp, D_MODEL)
    q, cmp_p, sel_p, win_p, gates, cb, u = _inproj(xp, w)
    seq = lambda a: a.reshape(bp, tp, a.shape[-1])
    kc = _compress_prompt(seq(cmp_p), w)
    o_att = _attn_prompt(seq(q), seq(sel_p), seq(win_p), seq(gates), kc, cbias, w)
    y_p = _tail(xp, o_att.reshape(bp * tp, ATT_WIDTH), cb, u, None, p_prompt[0].reshape(bp * tp, PLE_DIM), w, tp)

    xs = x_sample.reshape(db, D_MODEL)
    q_s, cmp_s, sel_s, win_s, gates_s, cb_s, u_s = _inproj(xs, w)
    qh = q_s.reshape(db, N_KV, GROUP, HEAD_DIM)
    zq = jnp.zeros_like(qh[:, 0])
    q8 = jnp.concatenate([jnp.concatenate([qh[:, 0], zq], axis=-1), jnp.concatenate([zq, qh[:, 1]], axis=-1)], axis=1)
    n_phys = cache_cmp_kv.shape[1]
    token_minor = lambda a: jnp.transpose(a[0], (0, 2, 3, 4, 1))
    o_c, picks = _sample_cmp(page_table, q8, token_minor(cache_cmp_kv).reshape(n_phys, PAGE_ROWS, PAGE_SIZE), sbias, w)
    picks = picks[:, :N_KV, :N_PICK].reshape(db, N_KV * N_PICK)
    gates8 = jnp.pad(jnp.transpose(gates_s[:, :3 * N_HEADS].reshape(db, 3, N_HEADS), (0, 2, 1)),
                     ((0, 0), (0, 0), (0, LANES - 3)))
    o8, win_new = _sample_attn(page_table, picks, q8,
                               token_minor(cache_sel_kv).reshape(n_phys, 2 * N_KV, HEAD_DIM, PAGE_SIZE),
                               token_minor(state_win_kv).reshape(db, 2 * KV_WIDTH, wlen),
                               sel_s[:, None, :], win_s[:, None, :], gates8, o_c, stabs)
    o_att_s = o8.reshape(db, ATT_WIDTH)
    win_new = jnp.transpose(win_new.reshape(db, 2, N_KV, HEAD_DIM, wlen), (0, 4, 1, 2, 3))
    u_prev = jnp.transpose(state_conv[0], (1, 0, 2))
    y_s = _tail(xs, o_att_s, cb_s, u_s, u_prev, p_sample[0].reshape(db, PLE_DIM), w, None)

    return (y_p.reshape(bp, tp, D_MODEL), y_s.reshape(db, 1, D_MODEL),
            kv6(cmp_p, bp, tp), kv6(sel_p, bp, tp), kv6(seq(win_p)[:, tp - WINDOW:], bp, WINDOW),
            seq(u)[:, tp - 2:][None],
            kv6(cmp_s, db, 1), kv6(sel_s, db, 1), win_new[None],
            jnp.concatenate([state_conv[0][:, 1:], u_s[:, None, :]], axis=1)[None])
```

```python
import functools
import math

import jax
import jax.numpy as jnp
import numpy as np
from jax import lax
from jax.experimental import pallas as pl
from jax.experimental.pallas import tpu as pltpu

D_MODEL = 1024
HEAD_DIM = 64
N_HEADS = 8
N_KV = 2
GROUP = N_HEADS // N_KV
ATT_WIDTH = N_HEADS * HEAD_DIM
KV_WIDTH = N_KV * HEAD_DIM
CONV_DIM = D_MODEL - ATT_WIDTH
PAGE_SIZE = 128
CMP_BLOCK = 32
SEL_BLOCK = 64
TOP_N = 8
WINDOW = 512
RP_BUCKETS = 32
RP_MAX_DIST = 128
D_FF = 4 * D_MODEL
PLE_DIM = 256
SCALE = HEAD_DIM ** -0.5
NEG = -1e30
EPS = 1e-6

LANES = 128
QBLK = 128
VMEM_LIMIT = 56 * 1024 * 1024

_BF = jnp.bfloat16
_F32 = jnp.float32


def _dot(a, b):
    return jnp.dot(a, b, preferred_element_type=_F32)


def _dot_nt(a, b):
    return lax.dot_general(a, b, (((1,), (1,)), ((), ())), preferred_element_type=_F32)


def _rms_rows(x, gain):
    return x * lax.rsqrt(jnp.mean(x * x, axis=-1, keepdims=True) + EPS) * gain


def _group_rms(z, gmat, gain):
    ssq = _dot((z * z).astype(_BF), gmat) * (1.0 / HEAD_DIM)
    return z * lax.rsqrt(ssq + EPS) * gain


def _inproj_body(x_ref, nm_ref, wq_ref, wkv_ref, wg_ref, wc_ref, qg_ref, kg_ref, g512_ref, g128_ref,
                 q_ref, cmp_ref, sel_ref, win_ref, gate_ref, cb_ref, u_ref):
    a = _rms_rows(x_ref[...], nm_ref[...]).astype(_BF)
    zq = _dot(a, wq_ref[...])
    q_ref[...] = (_group_rms(zq, g512_ref[...], qg_ref[...]) * SCALE).astype(_BF)
    zkv = _dot(a, wkv_ref[...])
    cmp_ref[...] = zkv[:, 0:256]
    g128 = g128_ref[...]
    sel_ref[:, 0:128] = _group_rms(zkv[:, 256:384], g128, kg_ref[0:1, :])
    sel_ref[:, 128:256] = zkv[:, 384:512]
    win_ref[:, 0:128] = _group_rms(zkv[:, 512:640], g128, kg_ref[1:2, :])
    win_ref[:, 128:256] = zkv[:, 640:768]
    gate_ref[...] = jax.nn.sigmoid(_dot(a, wg_ref[...]))
    zc = _dot(a, wc_ref[...])
    cb_ref[...] = zc[:, 0:512]
    u_ref[...] = zc[:, 512:1024] * zc[:, 1024:1536]


def _inproj(x, w):
    n = x.shape[0]
    tm = min(256, n)
    row = lambda c: pl.BlockSpec((tm, c), lambda i: (i, 0))
    full = lambda a: pl.BlockSpec(a.shape, lambda i: (0,) * a.ndim)
    consts = (w['norm_mix'], w['wq'], w['wkv'], w['wg'], w['wc'], w['q_gain'], w['k_gain'], w['g512'], w['g128'])
    return pl.pallas_call(
        _inproj_body,
        grid=(n // tm,),
        in_specs=[row(D_MODEL)] + [full(c) for c in consts],
        out_specs=[row(512), row(256), row(256), row(256), row(128), row(512), row(512)],
        out_shape=[jax.ShapeDtypeStruct((n, 512), _BF)] + [jax.ShapeDtypeStruct((n, 256), _F32)] * 3
        + [jax.ShapeDtypeStruct((n, 128), _F32)] + [jax.ShapeDtypeStruct((n, 512), _F32)] * 2,
        compiler_params=pltpu.CompilerParams(dimension_semantics=("arbitrary",), vmem_limit_bytes=VMEM_LIMIT),
        name="inproj",
    )(x, *consts)


FF_CHUNK = 1024


def _tail_body(halo, h_ref, o_ref, cb_ref, u_ref, up_ref, p_ref, cw_ref, on_ref, nmlp_ref, nple_ref,
               wout_hbm, wup_hbm, wdn_hbm, wgate_hbm, wproj_hbm, y_ref,
               uext_ref, wout_ref, wup_ref, wdn_ref, wgate_ref, wproj_ref, wsem):
    @pl.when(pl.program_id(0) == 0)
    def _load_weights():
        copies = [pltpu.make_async_copy(src, dst, wsem.at[i]) for i, (src, dst) in enumerate(
            ((wout_hbm, wout_ref), (wup_hbm, wup_ref), (wdn_hbm, wdn_ref), (wgate_hbm, wgate_ref),
             (wproj_hbm, wproj_ref)))]
        for c in copies:
            c.start()
        for c in copies:
            c.wait()

    tm = h_ref.shape[0]
    u = u_ref[...]
    if halo:
        first = (pl.program_id(0) % halo) == 0
        prev = jnp.where(first, 0.0, up_ref[...])
        uext_ref[0:8, :] = prev
        uext_ref[8:tm + 8, :] = u
        u2 = uext_ref[6:tm + 6, :]
        u1 = uext_ref[7:tm + 7, :]
    else:
        u2 = up_ref[0]
        u1 = up_ref[1]
    yc = cw_ref[0:1, :] * u2 + cw_ref[1:2, :] * u1 + cw_ref[2:3, :] * u
    mix_a = _rms_rows(o_ref[...], on_ref[:, 0:ATT_WIDTH]).astype(_BF)
    mix_c = _rms_rows(cb_ref[...] * yc, on_ref[:, ATT_WIDTH:]).astype(_BF)
    h = h_ref[...] + _dot(mix_a, wout_ref[0:ATT_WIDTH, :]) + _dot(mix_c, wout_ref[ATT_WIDTH:, :])
    a = _rms_rows(h, nmlp_ref[...]).astype(_BF)
    y_ref[...] = h
    for c in range(D_FF // FF_CHUNK):
        t = jnp.maximum(_dot(a, wup_ref[:, c * FF_CHUNK:(c + 1) * FF_CHUNK]), 0.0)
        y_ref[...] += _dot((t * t).astype(_BF), wdn_ref[c * FF_CHUNK:(c + 1) * FF_CHUNK, :])
    h = y_ref[...]
    a = _rms_rows(h, nple_ref[...]).astype(_BF)
    gate = jax.nn.sigmoid(_dot(a, wgate_ref[...]))
    y_ref[...] = h + gate * _dot(p_ref[...].astype(_BF), wproj_ref[...])


def _tail(h, o_att, cb, u, u_prev, p, w, seq_len):
    n = h.shape[0]
    tm = min(256, n)
    row = lambda c: pl.BlockSpec((tm, c), lambda i: (i, 0))
    const = lambda a: pl.BlockSpec(a.shape, lambda i: (0,) * a.ndim)
    if seq_len is not None:
        halo = seq_len // tm
        up_spec = pl.BlockSpec((8, CONV_DIM), lambda i: (jnp.maximum(i * (tm // 8) - 1, 0), 0))
        up = u
    else:
        halo = 0
        up_spec = pl.BlockSpec((2, tm, CONV_DIM), lambda i: (0, i, 0))
        up = u_prev
    consts = (w['conv_w'], w['out_norm'], w['norm_mlp'], w['norm_ple'])
    mats = (w['w_out'], w['w_up'], w['w_down'], w['w_ple_gate'], w['w_ple_proj'])
    return pl.pallas_call(
        functools.partial(_tail_body, halo),
        grid=(n // tm,),
        in_specs=[row(D_MODEL), row(ATT_WIDTH), row(CONV_DIM), row(CONV_DIM), up_spec, row(PLE_DIM)]
        + [const(c) for c in consts] + [pl.BlockSpec(memory_space=pl.ANY)] * len(mats),
        out_specs=row(D_MODEL),
        out_shape=jax.ShapeDtypeStruct((n, D_MODEL), _F32),
        scratch_shapes=[pltpu.VMEM((tm + 8, CONV_DIM), _F32)] + [pltpu.VMEM(m.shape, _BF) for m in mats]
        + [pltpu.SemaphoreType.DMA((len(mats),))],
        compiler_params=pltpu.CompilerParams(dimension_semantics=("arbitrary",), vmem_limit_bytes=VMEM_LIMIT),
        name="tail",
    )(h, o_att, cb, u, up, p, *consts, *mats)


def _split_heads(x, lane_lo):
    xr = pltpu.roll(x, HEAD_DIM, axis=1)
    zero = jnp.zeros_like(x)
    a = (jnp.where(lane_lo, x, zero), jnp.where(lane_lo, xr, zero))
    b = (jnp.where(lane_lo, zero, xr), jnp.where(lane_lo, zero, x))
    return a, b


def _compress_rows(load, ns, bd_ref, pe_ref):
    acc = [jnp.zeros((2 * ns, LANES), _F32), jnp.zeros((2 * ns, LANES), _F32)]
    for l in range(CMP_BLOCK):
        for plane in range(2):
            x = jnp.concatenate([load(plane, l), load(plane, CMP_BLOCK + l)], axis=0)
            x = x + pe_ref[l, plane:plane + 1, :]
            acc[plane] = acc[plane] + _dot(x.astype(_BF), bd_ref[l, plane])
    return acc


def _compress_prompt_body(slab_ref, bd_ref, pe_ref, kg_ref, g128_ref, kA_ref, kB_ref, vA_ref, vB_ref):
    ns = slab_ref.shape[1] // (2 * SEL_BLOCK)
    k, v = _compress_rows(lambda plane, t0: slab_ref[0, pl.ds(2 * t0 + plane, ns, stride=2 * SEL_BLOCK), :],
                          ns, bd_ref, pe_ref)
    k = _group_rms(k, g128_ref[...], kg_ref[...])
    lane_lo = lax.broadcasted_iota(jnp.int32, k.shape, 1) < HEAD_DIM
    ka, kb = _split_heads(k, lane_lo)
    va, vb = _split_heads(v, lane_lo)
    pad = jnp.zeros((LANES - 2 * ns, LANES), _BF)
    for g in range(N_KV):
        for ref, val in ((kA_ref, ka[g]), (kB_ref, kb[g]), (vA_ref, va[g]), (vB_ref, vb[g])):
            ref[0, g, 0:2 * ns, :] = val.astype(_BF)
            if 2 * ns < LANES:
                ref[0, g, 2 * ns:, :] = pad


def _compress_prompt(slab, w):
    b, t, _ = slab.shape
    assert t % QBLK == 0 and t // CMP_BLOCK <= LANES
    slab = slab.reshape(b, 2 * t, LANES)
    full = lambda a: pl.BlockSpec(a.shape, lambda i: (0,) * a.ndim)
    consts = (w['bd'], w['pe_t'], w['kc_gain'], w['g128'])
    out = jax.ShapeDtypeStruct((b, N_KV, LANES, LANES), _BF)
    return pl.pallas_call(
        _compress_prompt_body,
        grid=(b,),
        in_specs=[pl.BlockSpec((1, 2 * t, LANES), lambda i: (i, 0, 0))] + [full(c) for c in consts],
        out_specs=[pl.BlockSpec((1, N_KV, LANES, LANES), lambda i: (i, 0, 0, 0))] * 4,
        out_shape=[out] * 4,
        compiler_params=pltpu.CompilerParams(dimension_semantics=("arbitrary",), vmem_limit_bytes=VMEM_LIMIT),
        name="compress_prompt",
    )(slab, *consts)


BUILD_ROWS = 256


def _top_extra(impb, cand, lane_f, n_extra):
    v = jnp.where(cand, impb, -1.0)
    picked = jnp.zeros(impb.shape, _F32)
    for _ in range(n_extra):
        mx = jnp.max(v, axis=-1, keepdims=True)
        first = jnp.min(jnp.where(v == mx, lane_f, 1e9), axis=-1, keepdims=True)
        hit = lane_f == first
        picked = jnp.where(hit, 1.0, picked)
        v = jnp.where(hit, -1.0, v)
    return picked


def _attn_prompt_body(q_ref, ks_ref, kw_ref, gate_ref, kcA_ref, kcB_ref, vcA_ref, vcB_ref, cbias_ref, tiles_ref,
                      eg_ref, o_ref, ksA, ksB, vsA, vsB, kwA, kwB, vwA, vwB, s_scr, p_scr, m_scr, acc_scr):
    qb = pl.program_id(1)
    t_len = ks_ref.shape[1]
    n_sel = t_len // SEL_BLOCK

    @pl.when(qb == 0)
    def _build():
        def chunk(c, carry):
            r0 = pl.multiple_of(c * BUILD_ROWS, BUILD_ROWS)
            rows = pl.ds(r0, BUILD_ROWS)
            lane = lax.broadcasted_iota(jnp.int32, (BUILD_ROWS, LANES), 1)
            blk = (r0 + lax.broadcasted_iota(jnp.int32, (BUILD_ROWS, LANES), 0)) // SEL_BLOCK
            lane_lo = lane < HEAD_DIM
            oh_hi = jnp.where(lane == blk + HEAD_DIM, 1.0, 0.0)
            oh_lo = jnp.where(lane == blk, 1.0, 0.0)
            one_hi = jnp.where(lane == HEAD_DIM, 1.0, 0.0)
            one_lo = jnp.where(lane == 0, 1.0, 0.0)
            for src, k_a, k_b, v_a, v_b, onehot in ((ks_ref, ksA, ksB, vsA, vsB, True),
                                                    (kw_ref, kwA, kwB, vwA, vwB, False)):
                ka, kb = _split_heads(src[0, rows, 0:128], lane_lo)
                va, vb = _split_heads(src[0, rows, 128:256], lane_lo)
                for g in range(N_KV):
                    if onehot:
                        k_a[g, rows, :] = jnp.where(lane_lo, ka[g], oh_hi).astype(_BF)
                        k_b[g, rows, :] = jnp.where(lane_lo, oh_lo, kb[g]).astype(_BF)
                    else:
                        k_a[g, rows, :] = ka[g].astype(_BF)
                        k_b[g, rows, :] = kb[g].astype(_BF)
                    v_a[g, rows, :] = jnp.where(lane_lo, va[g], one_hi).astype(_BF)
                    v_b[g, rows, :] = jnp.where(lane_lo, one_lo, vb[g]).astype(_BF)
            return carry
        lax.fori_loop(0, t_len // BUILD_ROWS, chunk, 0)
        p_scr[...] = jnp.zeros(p_scr.shape, _BF)

    lane2 = lax.broadcasted_iota(jnp.int32, (2 * QBLK, LANES), 1)
    lo2 = lane2 < HEAD_DIM
    lane1 = lax.broadcasted_iota(jnp.int32, (QBLK, LANES), 1)
    lane1_f = lane1.astype(_F32)
    tpos = qb * QBLK + lax.broadcasted_iota(jnp.int32, (QBLK, LANES), 0)
    cur = tpos // SEL_BLOCK
    forced = (lane1 == 0) | (lane1 == cur) | (lane1 == cur - 1)
    cand = (lane1 >= 1) & (lane1 <= cur - 2)
    few = cur <= TOP_N - 1

    gates = gate_ref[0]
    g_hi = gates.astype(_BF)
    g_lo = (gates - g_hi.astype(_F32)).astype(_BF)
    gexp = _dot(g_hi, eg_ref[...]) + _dot(g_lo, eg_ref[...])

    def normalize(acc, x):
        l = jnp.sum(jnp.where(lane2 == (HEAD_DIM if x == 0 else 0), acc, 0.0), axis=-1, keepdims=True)
        keep = lo2 if x == 0 else jnp.logical_not(lo2)
        return jnp.where(keep, acc / l, 0.0)

    o_cmp, q_plain, q_aug = [], [], []
    for g in range(N_KV):
        qs = jnp.concatenate([q_ref[0, :, (2 * g) * LANES:(2 * g + 1) * LANES],
                              q_ref[0, :, (2 * g + 1) * LANES:(2 * g + 2) * LANES]], axis=0)
        zero = jnp.zeros_like(qs)
        q_a = jnp.where(lo2, qs, zero)
        q_b = jnp.where(lo2, zero, qs)

        def cmp_probs(qx, k_ref, x):
            s = _dot_nt(qx, k_ref[0, g]) + cbias_ref[0, g, x]
            m = jnp.max(s, axis=-1, keepdims=True)
            e = jnp.where(s > 0.5 * NEG, jnp.exp(s - m), 0.0)
            l = jnp.sum(e, axis=-1, keepdims=True)
            return e / jnp.where(l > 0.0, l, 1.0)
        p_a = cmp_probs(q_a, kcA_ref, 0)
        p_b = cmp_probs(q_b, kcB_ref, 1)
        o_c = _dot(p_a.astype(_BF), vcA_ref[0, g]) + _dot(p_b.astype(_BF), vcB_ref[0, g])
        imp = p_a[0:QBLK] + p_a[QBLK:] + p_b[0:QBLK] + p_b[QBLK:]
        impb = imp + pltpu.roll(imp, LANES - n_sel, axis=1)

        picked = lax.cond(qb * QBLK >= TOP_N * SEL_BLOCK,
                          lambda: _top_extra(impb, cand, lane1_f, TOP_N - 3),
                          lambda: jnp.zeros((QBLK, LANES), _F32))
        chosen = forced | (few & (lane1 <= cur)) | ((picked > 0.5) & jnp.logical_not(few))
        sb_lo = jnp.where(chosen | (lane1 >= n_sel), 0.0, NEG)
        sb_hi = pltpu.roll(sb_lo, HEAD_DIM, axis=1)
        sb_lo2 = jnp.concatenate([sb_lo, sb_lo], axis=0).astype(_BF)
        sb_hi2 = jnp.concatenate([sb_hi, sb_hi], axis=0).astype(_BF)
        o_cmp.append(o_c)
        q_plain += [q_a, q_b]
        q_aug += [jnp.where(lo2, qs, sb_hi2), jnp.where(lo2, sb_lo2, qs)]

    chains = [(g, x) for g in range(N_KV) for x in range(2)]
    k_sel, v_sel, k_win, v_win = (ksA, ksB), (vsA, vsB), (kwA, kwB), (vwA, vwB)
    n_chunk = t_len // QBLK
    n_win = WINDOW // QBLK + 1

    w_first = jnp.maximum(qb - (n_win - 1), 0)
    w_rows = pl.ds(pl.multiple_of(w_first * QBLK, QBLK), n_win * QBLK)
    o_win = []
    for ch, (g, x) in enumerate(chains):
        s = _dot_nt(q_plain[ch], k_win[x][g, w_rows, :])
        parts = []
        for i in range(n_win):
            idx = qb - w_first - i
            parts.append(s[:, i * QBLK:(i + 1) * QBLK] + tiles_ref[jnp.where(idx < 0, n_win, idx), g, x])
        m = parts[0]
        for part in parts[1:]:
            m = jnp.maximum(m, part)
        m = jnp.max(m, axis=-1, keepdims=True)
        p = jnp.concatenate([jnp.exp(part - m).astype(_BF) for part in parts], axis=1)
        o_win.append(normalize(_dot(p, v_win[x][g, w_rows, :]), x))

    n_quart = 4
    qw = t_len // n_quart
    for qi in range(n_quart):
        @pl.when(qb * QBLK >= qi * qw)
        def _scores(qi=qi):
            for ch, (g, x) in enumerate(chains):
                s_scr[ch, :, qi * qw:(qi + 1) * qw] = _dot_nt(q_aug[ch], k_sel[x][g, qi * qw:(qi + 1) * qw, :])

    def biased(ch, c):
        g, x = chains[ch]
        return s_scr[ch, :, c * QBLK:(c + 1) * QBLK] + tiles_ref[jnp.minimum(qb - c, 2), g, x]

    m_scr[...] = jnp.full(m_scr.shape, NEG, _F32)
    for c in range(n_chunk):
        @pl.when(c <= qb)
        def _run_max(c=c):
            for ch in range(len(chains)):
                m_scr[ch] = jnp.maximum(m_scr[ch], biased(ch, c))
    m_sel = [jnp.max(m_scr[ch], axis=-1, keepdims=True) for ch in range(len(chains))]

    for c in range(n_chunk):
        @pl.when(c <= qb)
        def _probs(c=c):
            for ch in range(len(chains)):
                p_scr[ch, :, c * QBLK:(c + 1) * QBLK] = jnp.exp(biased(ch, c) - m_sel[ch]).astype(_BF)

    acc_scr[...] = jnp.zeros(acc_scr.shape, _F32)
    for qi in range(n_quart):
        @pl.when(qb * QBLK >= qi * qw)
        def _weighted(qi=qi):
            for ch, (g, x) in enumerate(chains):
                acc_scr[ch] += _dot(p_scr[ch, :, qi * qw:(qi + 1) * qw], v_sel[x][g, qi * qw:(qi + 1) * qw, :])

    for g in range(N_KV):
        o_c = o_cmp[g]
        o_s = normalize(acc_scr[2 * g], 0) + normalize(acc_scr[2 * g + 1], 1)
        o_w = o_win[2 * g] + o_win[2 * g + 1]
        for pr in range(2):
            rows = slice(pr * QBLK, (pr + 1) * QBLK)
            col = (2 * g + pr) * LANES
            o_ref[0, :, col:col + LANES] = (gexp[:, col:col + LANES] * o_c[rows]
                                            + gexp[:, ATT_WIDTH + col:ATT_WIDTH + col + LANES] * o_s[rows]
                                            + gexp[:, 2 * ATT_WIDTH + col:2 * ATT_WIDTH + col + LANES] * o_w[rows])


def _attn_prompt(q, sel, win, gates, kc, cbias, w):
    b, t, _ = sel.shape
    nq = t // QBLK
    assert t // SEL_BLOCK <= CMP_BLOCK and t % (4 * QBLK) == 0 and t >= WINDOW + QBLK
    kcA, kcB, vcA, vcB = kc
    full = lambda a: pl.BlockSpec(a.shape, lambda i, j: (0,) * a.ndim)
    slab = pl.BlockSpec((1, t, 256), lambda i, j: (i, 0, 0))
    kcs = pl.BlockSpec((1, N_KV, LANES, LANES), lambda i, j: (i, 0, 0, 0))
    n_chain = 2 * N_KV
    scratch = [pltpu.VMEM((N_KV, t, LANES), _BF)] * 8 + [
        pltpu.VMEM((n_chain, 2 * QBLK, t), _F32), pltpu.VMEM((n_chain, 2 * QBLK, t), _BF),
        pltpu.VMEM((n_chain, 2 * QBLK, LANES), _F32), pltpu.VMEM((n_chain, 2 * QBLK, LANES), _F32)]
    return pl.pallas_call(
        _attn_prompt_body,
        grid=(b, nq),
        in_specs=[pl.BlockSpec((1, QBLK, ATT_WIDTH), lambda i, j: (i, j, 0)), slab, slab,
                  pl.BlockSpec((1, QBLK, LANES), lambda i, j: (i, j, 0)), kcs, kcs, kcs, kcs,
                  pl.BlockSpec((1, N_KV, 2, 2 * QBLK, LANES), lambda i, j: (j, 0, 0, 0, 0)),
                  full(w['tiles']), full(w['egate'])],
        out_specs=pl.BlockSpec((1, QBLK, ATT_WIDTH), lambda i, j: (i, j, 0)),
        out_shape=jax.ShapeDtypeStruct((b, t, ATT_WIDTH), _F32),
        scratch_shapes=scratch,
        compiler_params=pltpu.CompilerParams(dimension_semantics=("arbitrary", "arbitrary"),
                                             vmem_limit_bytes=VMEM_LIMIT),
        name="attn_prompt",
    )(q, sel, win, gates, kcA, kcB, vcA, vcB, cbias, w['tiles'], w['egate'])


N_PICK = TOP_N - 3
N_SLOT = TOP_N - 1
PAGE_ROWS = 2 * KV_WIDTH


PAIR_TOKENS = 2 * PAGE_SIZE


def _sample_cmp_body(pt_ref, q_ref, cache_ref, bd_ref, pet_ref, perm_ref, kg_ref, g128_ref, sb_ref, oc_ref, pick_ref,
                     buf, rows_scr, sem):
    b = pl.program_id(0)
    nb = pl.num_programs(0)
    n_pairs = pt_ref.shape[1] // 2
    nc = 8 * n_pairs
    slot = b % 2

    def page_copy(bb, pair, half, sl):
        return pltpu.make_async_copy(cache_ref.at[pt_ref[bb, 2 * pair + half]],
                                     buf.at[sl, pair, :, :, half * PAGE_SIZE:(half + 1) * PAGE_SIZE], sem.at[sl])

    def for_pages(fn):
        def step(pair, c):
            fn(pair, 0)
            fn(pair, 1)
            return c
        lax.fori_loop(0, n_pairs, step, 0)

    @pl.when(b == 0)
    def _prime():
        for_pages(lambda pair, half: page_copy(0, pair, half, 0).start())

    @pl.when(b + 1 < nb)
    def _prefetch():
        for_pages(lambda pair, half: page_copy(b + 1, pair, half, 1 - slot).start())

    for_pages(lambda pair, half: page_copy(b, pair, half, slot).wait())

    def regroup(pair, c):
        for plane in range(2):
            x = (buf[slot, pair, plane] + pet_ref[plane]).astype(_BF)
            rows_scr[plane, pair] = _dot_nt(perm_ref[...], x)
        return c
    lax.fori_loop(0, n_pairs, regroup, 0, unroll=4)

    acc = [jnp.zeros((nc, LANES), _F32), jnp.zeros((nc, LANES), _F32)]
    for l in range(CMP_BLOCK):
        for plane in range(2):
            x = rows_scr[plane, :, 8 * l:8 * (l + 1), :].reshape(nc, LANES)
            acc[plane] = acc[plane] + _dot(x.astype(_BF), bd_ref[l, plane])
    k = _group_rms(acc[0], g128_ref[...], kg_ref[...]).astype(_BF)
    v = acc[1].astype(_BF)
    s = _dot_nt(q_ref[0], k) + sb_ref[...]
    m = jnp.max(s, axis=-1, keepdims=True)
    e = jnp.exp(s - m)
    p = e / jnp.sum(e, axis=-1, keepdims=True)
    oc_ref[0] = _dot(p.astype(_BF), v)
    imp8 = p + pltpu.roll(p, nc - 1, axis=1)
    rows = [imp8[h:h + 1] for h in range(N_HEADS)]
    imp_g = [rows[g * GROUP] + rows[g * GROUP + 1] + rows[g * GROUP + 2] + rows[g * GROUP + 3] for g in range(N_KV)]
    imp = jnp.concatenate(imp_g + [jnp.full((8 - N_KV, nc), -1.0, _F32)], axis=0)
    lane = lax.broadcasted_iota(jnp.int32, (8, nc), 1)
    lane_f = lane.astype(_F32)
    v_c = jnp.where((lane % 2 == 0) & (lane >= 2) & (lane <= nc - 4), imp, -1.0)
    picks = jnp.zeros((8, nc), _F32)
    for i in range(N_PICK):
        mx = jnp.max(v_c, axis=-1, keepdims=True)
        first = jnp.min(jnp.where(v_c == mx, lane_f, 1e9), axis=-1, keepdims=True)
        picks = jnp.where(lane == i, first * 0.5, picks)
        v_c = jnp.where(lane_f == first, -1.0, v_c)
    pick_ref[0] = picks.astype(jnp.int32)


def _sample_cmp(page_table, q8, cache, sbias, w):
    db, n_pages = page_table.shape
    nc = 4 * n_pages
    assert n_pages % 2 == 0 and nc // 2 - 2 >= N_PICK
    full = lambda a: pl.BlockSpec(a.shape, lambda i, pt: (0,) * a.ndim)
    consts = (w['bd'], w['pe_tok'], w['perm'], w['kc_gain'], w['g128'], sbias)
    grid_spec = pltpu.PrefetchScalarGridSpec(
        num_scalar_prefetch=1,
        grid=(db,),
        in_specs=[pl.BlockSpec((1, 8, LANES), lambda i, pt: (i, 0, 0)), pl.BlockSpec(memory_space=pl.ANY)]
        + [full(c) for c in consts],
        out_specs=[pl.BlockSpec((1, 8, LANES), lambda i, pt: (i, 0, 0)),
                   pl.BlockSpec((1, 8, nc), lambda i, pt: (i, 0, 0))],
        scratch_shapes=[pltpu.VMEM((2, n_pages // 2, 2, KV_WIDTH, PAIR_TOKENS), _F32),
                        pltpu.VMEM((2, n_pages // 2, PAIR_TOKENS, KV_WIDTH), _F32),
                        pltpu.SemaphoreType.DMA((2,))],
    )
    return pl.pallas_call(
        _sample_cmp_body,
        grid_spec=grid_spec,
        out_shape=[jax.ShapeDtypeStruct((db, 8, LANES), _F32), jax.ShapeDtypeStruct((db, 8, nc), jnp.int32)],
        compiler_params=pltpu.CompilerParams(dimension_semantics=("arbitrary",), vmem_limit_bytes=VMEM_LIMIT),
        name="sample_cmp",
    )(page_table, q8, cache, *consts)


def _sample_attn_body(pt_ref, pick_ref, q_ref, cache_ref, win_ref, nsel_ref, nwin_ref, ncol_ref, gate_ref, oc_ref,
                      bsel_ref, b0_ref, bwin_ref, o_ref, wout_ref, kb, sem):
    b = pl.program_id(0)
    nb = pl.num_programs(0)
    nj = 2 * pt_ref.shape[1]
    wlen = win_ref.shape[2]
    slot = b % 2

    def block_of(bb, g, s):
        if s == 0:
            return 0
        if s == N_SLOT - 1:
            return nj - 1
        return pick_ref[bb, g * N_PICK + (s - 1)]

    def tile_copy(bb, g, s, plane, sl):
        page = pt_ref[bb, block_of(bb, g, s) // 2]
        return pltpu.make_async_copy(cache_ref.at[page, plane * N_KV + g],
                                     kb.at[sl, g, plane, :, pl.ds(s * PAGE_SIZE, PAGE_SIZE)], sem.at[sl])

    def for_tiles(fn):
        for g in range(N_KV):
            for s in range(N_SLOT):
                for plane in range(2):
                    fn(g, s, plane)

    @pl.when(b == 0)
    def _prime():
        for_tiles(lambda g, s, plane: tile_copy(0, g, s, plane, 0).start())

    @pl.when(b + 1 < nb)
    def _prefetch():
        for_tiles(lambda g, s, plane: tile_copy(b + 1, g, s, plane, 1 - slot).start())

    q8 = q_ref[0]
    q8f = q8.astype(_F32)
    head_g = lax.broadcasted_iota(jnp.int32, (8, HEAD_DIM), 0) // GROUP
    own_half = lambda x: jnp.where(head_g == 0, x[:, 0:HEAD_DIM], x[:, HEAD_DIM:])
    b0 = b0_ref[:, 0:1]

    x = win_ref[0]
    lane_w = lax.broadcasted_iota(jnp.int32, x.shape, 1)
    wout_ref[0] = jnp.where(lane_w == wlen - 1, ncol_ref[0], pltpu.roll(x, wlen - 1, axis=1))
    sw = _dot(q8, x[0:KV_WIDTH, :].astype(_BF)) + bwin_ref[...]
    sw_new = jnp.sum(q8f * nwin_ref[0][:, 0:KV_WIDTH], axis=-1, keepdims=True) + b0
    mw = jnp.maximum(jnp.max(sw, axis=-1, keepdims=True), sw_new)
    ew = jnp.exp(sw - mw)
    ew_new = jnp.exp(sw_new - mw)
    o_w = _dot_nt(ew.astype(_BF), x[KV_WIDTH:, :].astype(_BF)) + ew_new * nwin_ref[0][:, KV_WIDTH:]
    o_w = own_half(o_w / (jnp.sum(ew, axis=-1, keepdims=True) + ew_new))

    for_tiles(lambda g, s, plane: tile_copy(b, g, s, plane, slot).wait())

    ss_new = jnp.sum(q8f * nsel_ref[0][:, 0:KV_WIDTH], axis=-1, keepdims=True) + b0
    lane_half = lax.broadcasted_iota(jnp.int32, (8, PAGE_SIZE), 1) // SEL_BLOCK
    o_sel = []
    for g in range(N_KV):
        pieces = []
        for s in range(N_SLOT):
            j = jnp.full((8, PAGE_SIZE), block_of(b, g, s), jnp.int32)
            if s == 0:
                tab = bsel_ref[0]
            elif s == N_SLOT - 1:
                tab = bsel_ref[2]
            else:
                tab = jnp.where(j == nj - 2, bsel_ref[1], bsel_ref[0])
            pieces.append(jnp.where(lane_half == j % 2, tab, NEG))
        ss = _dot(q8[:, g * HEAD_DIM:(g + 1) * HEAD_DIM], kb[slot, g, 0].astype(_BF)) + jnp.concatenate(pieces, axis=1)
        ms = jnp.maximum(jnp.max(ss, axis=-1, keepdims=True), ss_new)
        es = jnp.exp(ss - ms)
        es_new = jnp.exp(ss_new - ms)
        v_new = nsel_ref[0][:, KV_WIDTH + g * HEAD_DIM:KV_WIDTH + (g + 1) * HEAD_DIM]
        o_g = _dot_nt(es.astype(_BF), kb[slot, g, 1].astype(_BF)) + es_new * v_new
        o_sel.append(o_g / (jnp.sum(es, axis=-1, keepdims=True) + es_new))
    o_s = jnp.where(head_g == 0, o_sel[0], o_sel[1])
    gates = gate_ref[0]
    o_ref[0] = gates[:, 0:1] * own_half(oc_ref[0]) + gates[:, 1:2] * o_s + gates[:, 2:3] * o_w


def _sample_attn(page_table, picks, q8, cache, win, new_sel, new_win, gates8, o_c, tabs):
    db, n_pages = page_table.shape
    wlen = win.shape[2]
    full = lambda a: pl.BlockSpec(a.shape, lambda i, pt, pk: (0,) * a.ndim)
    per_b = lambda r, c: pl.BlockSpec((1, r, c), lambda i, pt, pk: (i, 0, 0))
    bsel, b0, bwin = tabs
    grid_spec = pltpu.PrefetchScalarGridSpec(
        num_scalar_prefetch=2,
        grid=(db,),
        in_specs=[per_b(8, LANES), pl.BlockSpec(memory_space=pl.ANY), per_b(2 * KV_WIDTH, wlen), per_b(1, 256),
                  per_b(1, 256), per_b(2 * KV_WIDTH, 1), per_b(8, LANES), per_b(8, LANES),
                  full(bsel), full(b0), full(bwin)],
        out_specs=[per_b(8, HEAD_DIM), per_b(2 * KV_WIDTH, wlen)],
        scratch_shapes=[pltpu.VMEM((2, N_KV, 2, HEAD_DIM, N_SLOT * PAGE_SIZE), _F32), pltpu.SemaphoreType.DMA((2,))],
    )
    return pl.pallas_call(
        _sample_attn_body,
        grid_spec=grid_spec,
        out_shape=[jax.ShapeDtypeStruct((db, 8, HEAD_DIM), _F32), jax.ShapeDtypeStruct((db, 2 * KV_WIDTH, wlen), _F32)],
        compiler_params=pltpu.CompilerParams(dimension_semantics=("arbitrary",), vmem_limit_bytes=VMEM_LIMIT),
        name="sample_attn",
    )(page_table, picks, q8, cache, win, new_sel, new_win, new_win[:, 0, :, None], gates8, o_c, bsel, b0, bwin)


def _rel_bucket(dist):
    n = jnp.maximum(dist, 0)
    max_exact = RP_BUCKETS // 2
    nf = jnp.maximum(n, 1).astype(_F32)
    large = max_exact + (jnp.log(nf / max_exact) / math.log(RP_MAX_DIST / max_exact)
                         * (RP_BUCKETS - max_exact)).astype(jnp.int32)
    large = jnp.minimum(large, RP_BUCKETS - 1)
    return jnp.where(n < max_exact, n, large)


def _bias_of(rel_bias, dist):
    onehot = _rel_bucket(dist)[..., None, None] == jnp.arange(RP_BUCKETS)[:, None]
    return jnp.sum(jnp.where(onehot, rel_bias, 0.0), axis=-2)


def _head_major(x, lead):
    n = x.shape[-2]
    x = x.reshape(lead + (QBLK, n, N_KV, 2, 2))
    nl = len(lead)
    x = jnp.transpose(x, tuple(range(nl)) + (nl + 2, nl + 4, nl + 3, nl, nl + 1))
    return x.reshape(lead + (N_KV, 2, 2 * QBLK, n))


def _prompt_tables(rel_bias, t_len):
    nq = t_len // QBLK
    ns = t_len // SEL_BLOCK
    ti = np.arange(QBLK)[:, None]
    ki = np.arange(QBLK)[None, :]
    n_idx = WINDOW // QBLK + 1
    dist = np.stack([i * QBLK + ti - ki for i in range(n_idx)])
    valid = dist >= 0
    valid[n_idx - 1] &= dist[n_idx - 1] < WINDOW
    tiles = jnp.where(valid[..., None], _bias_of(rel_bias, jnp.asarray(dist)), NEG)
    tiles = jnp.concatenate([tiles, jnp.full_like(tiles[:1], NEG)])
    tiles = _head_major(tiles, (n_idx + 1,))
    lane = np.arange(LANES)
    blk = np.where(lane < ns, 2 * lane, 2 * (lane - ns) + 1)
    dist_c = np.arange(t_len)[:, None] - (CMP_BLOCK * blk + CMP_BLOCK - 1)[None, :]
    valid_c = (dist_c >= 0) & (lane < 2 * ns)[None, :]
    cb = jnp.where(valid_c[..., None], _bias_of(rel_bias, jnp.asarray(dist_c)), NEG)
    cb = _head_major(cb.reshape(nq, QBLK, LANES, N_HEADS), (nq,))
    return tiles, cb


def _sample_tables(rel_bias, past_len, wlen):
    blk = np.arange(past_len // CMP_BLOCK)
    sbias = _bias_of(rel_bias, jnp.asarray(past_len - (CMP_BLOCK * blk + CMP_BLOCK - 1))).T
    pos = np.arange(PAGE_SIZE) % SEL_BLOCK
    bsel = jnp.stack([_bias_of(rel_bias, jnp.asarray(np.full(PAGE_SIZE, past_len))).T,
                      _bias_of(rel_bias, jnp.asarray(2 * SEL_BLOCK - pos)).T,
                      _bias_of(rel_bias, jnp.asarray(SEL_BLOCK - pos)).T])
    b0 = _bias_of(rel_bias, jnp.zeros((LANES,), jnp.int32)).T
    tok = np.arange(wlen)
    bwin = jnp.where((tok >= 1)[None, :], _bias_of(rel_bias, jnp.asarray(wlen - tok)).T, NEG)
    return sbias, (bsel, b0, bwin)


def _prep(norm_mix, w_in, q_norm, k_norm, cmp_pe, w_cmp, conv_w, out_norm, w_out, norm_mlp, w_up, w_down,
          norm_ple, w_ple_gate, w_ple_proj):
    w_in = w_in[0]
    o_kv = ATT_WIDTH
    o_g = o_kv + 6 * KV_WIDTH
    o_c = o_g + 3 * N_HEADS
    bd = jnp.einsum('gh,plde->lpgdhe', jnp.eye(N_KV, dtype=_F32), w_cmp[0]).reshape(CMP_BLOCK, 2, LANES, LANES)
    pe_t = jnp.transpose(jnp.tile(cmp_pe[0], (1, 1, N_KV)), (1, 0, 2))
    egate = np.zeros((LANES, 3 * ATT_WIDTH), np.float32)
    for br in range(3):
        for h in range(N_HEADS):
            egate[br * N_HEADS + h, br * ATT_WIDTH + h * HEAD_DIM:br * ATT_WIDTH + (h + 1) * HEAD_DIM] = 1.0
    ones = np.ones((HEAD_DIM, HEAD_DIM), np.float32)
    perm = np.zeros((PAIR_TOKENS, PAIR_TOKENS), np.float32)
    tok = np.arange(PAIR_TOKENS)
    perm[(tok % CMP_BLOCK) * (PAIR_TOKENS // CMP_BLOCK) + tok // CMP_BLOCK, tok] = 1.0
    pe_tok = jnp.tile(jnp.transpose(cmp_pe[0], (0, 2, 1)), (1, N_KV, PAIR_TOKENS // CMP_BLOCK))
    return {
        'perm': jnp.asarray(perm, _BF), 'pe_tok': pe_tok,
        'norm_mix': norm_mix[0][None], 'wq': w_in[:, :o_kv].astype(_BF), 'wkv': w_in[:, o_kv:o_g].astype(_BF),
        'wg': jnp.pad(w_in[:, o_g:o_c], ((0, 0), (0, LANES - 3 * N_HEADS))).astype(_BF),
        'wc': w_in[:, o_c:].astype(_BF),
        'q_gain': jnp.tile(q_norm[0], N_HEADS)[None],
        'k_gain': jnp.stack([jnp.tile(k_norm[0, 1], N_KV), jnp.tile(k_norm[0, 2], N_KV)]),
        'kc_gain': jnp.tile(k_norm[0, 0], N_KV)[None],
        'g512': jnp.asarray(np.kron(np.eye(N_HEADS, dtype=np.float32), ones), _BF),
        'g128': jnp.asarray(np.kron(np.eye(N_KV, dtype=np.float32), ones), _BF),
        'bd': bd.astype(_BF), 'pe_t': pe_t,
        'egate': jnp.asarray(egate, _BF),
        'conv_w': conv_w[0], 'out_norm': out_norm[0][None], 'w_out': w_out[0].astype(_BF),
        'norm_mlp': norm_mlp[0][None], 'w_up': w_up[0].astype(_BF), 'w_down': w_down[0].astype(_BF),
        'norm_ple': norm_ple[0][None], 'w_ple_gate': w_ple_gate[0].astype(_BF),
        'w_ple_proj': w_ple_proj[0].astype(_BF),
    }


def kernel(x_prompt, x_sample, p_prompt, p_sample, cache_cmp_kv, cache_sel_kv, state_win_kv, state_conv, page_table, rel_bias, norm_mix, w_in, q_norm, k_norm, cmp_pe, w_cmp, conv_w, out_norm, w_out, norm_mlp, w_up, w_down, norm_ple, w_ple_gate, w_ple_proj):
    bp, tp, _ = x_prompt.shape
    db, ts, _ = x_sample.shape
    assert norm_mix.shape[0] == 1 and ts == 1 and tp >= WINDOW
    n_pages = page_table.shape[1]
    past_len = n_pages * PAGE_SIZE
    wlen = state_win_kv.shape[2]
    assert wlen == WINDOW and past_len >= 4 * SEL_BLOCK
    w = _prep(norm_mix, w_in, q_norm, k_norm, cmp_pe, w_cmp, conv_w, out_norm, w_out, norm_mlp, w_up, w_down,
              norm_ple, w_ple_gate, w_ple_proj)
    w['tiles'], cbias = _prompt_tables(rel_bias, tp)
    sbias, stabs = _sample_tables(rel_bias, past_len, wlen)
    kv6 = lambda a, b, t: a.reshape(1, b, t, 2, N_KV, HEAD_DIM)

    xp = x_prompt.reshape(bp * tp, D_MODEL)
    q, cmp_p, sel_p, win_p, gates, cb, u = _inproj(xp, w)
    seq = lambda a: a.reshape(bp, tp, a.shape[-1])
    kc = _compress_prompt(seq(cmp_p), w)
    o_att = _attn_prompt(seq(q), seq(sel_p), seq(win_p), seq(gates), kc, cbias, w)
    y_p = _tail(xp, o_att.reshape(bp * tp, ATT_WIDTH), cb, u, None, p_prompt[0].reshape(bp * tp, PLE_DIM), w, tp)

    xs = x_sample.reshape(db, D_MODEL)
    q_s, cmp_s, sel_s, win_s, gates_s, cb_s, u_s = _inproj(xs, w)
    qh = q_s.reshape(db, N_KV, GROUP, HEAD_DIM)
    zq = jnp.zeros_like(qh[:, 0])
    q8 = jnp.concatenate([jnp.concatenate([qh[:, 0], zq], axis=-1), jnp.concatenate([zq, qh[:, 1]], axis=-1)], axis=1)
    n_phys = cache_cmp_kv.shape[1]
    token_minor = lambda a: jnp.transpose(a[0], (0, 2, 3, 4, 1))
    o_c, picks = _sample_cmp(page_table, q8, token_minor(cache_cmp_kv).reshape(n_phys, 2, KV_WIDTH, PAGE_SIZE), sbias, w)
    picks = picks[:, :N_KV, :N_PICK].reshape(db, N_KV * N_PICK)
    gates8 = jnp.pad(jnp.transpose(gates_s[:, :3 * N_HEADS].reshape(db, 3, N_HEADS), (0, 2, 1)),
                     ((0, 0), (0, 0), (0, LANES - 3)))
    o8, win_new = _sample_attn(page_table, picks, q8,
                               token_minor(cache_sel_kv).reshape(n_phys, 2 * N_KV, HEAD_DIM, PAGE_SIZE),
                               token_minor(state_win_kv).reshape(db, 2 * KV_WIDTH, wlen),
                               sel_s[:, None, :], win_s[:, None, :], gates8, o_c, stabs)
    o_att_s = o8.reshape(db, ATT_WIDTH)
    win_new = jnp.transpose(win_new.reshape(db, 2, N_KV, HEAD_DIM, wlen), (0, 4, 1, 2, 3))
    u_prev = jnp.transpose(state_conv[0], (1, 0, 2))
    y_s = _tail(xs, o_att_s, cb_s, u_s, u_prev, p_sample[0].reshape(db, PLE_DIM), w, None)

    return (y_p.reshape(bp, tp, D_MODEL), y_s.reshape(db, 1, D_MODEL),
            kv6(cmp_p, bp, tp), kv6(sel_p, bp, tp), kv6(seq(win_p)[:, tp - WINDOW:], bp, WINDOW),
            seq(u)[:, tp - 2:][None],
            kv6(cmp_s, db, 1), kv6(sel_s, db, 1), win_new[None],
            jnp.concatenate([state_conv[0][:, 1:], u_s[:, None, :]], axis=1)[None])
```

```python
import functools
import math

import jax
import jax.numpy as jnp
import numpy as np
from jax import lax
from jax.experimental import pallas as pl
from jax.experimental.pallas import tpu as pltpu

D_MODEL = 1024
HEAD_DIM = 64
N_HEADS = 8
N_KV = 2
GROUP = N_HEADS // N_KV
ATT_WIDTH = N_HEADS * HEAD_DIM
KV_WIDTH = N_KV * HEAD_DIM
CONV_DIM = D_MODEL - ATT_WIDTH
PAGE_SIZE = 128
CMP_BLOCK = 32
SEL_BLOCK = 64
TOP_N = 8
WINDOW = 512
RP_BUCKETS = 32
RP_MAX_DIST = 128
D_FF = 4 * D_MODEL
PLE_DIM = 256
SCALE = HEAD_DIM ** -0.5
NEG = -1e30
EPS = 1e-6

LANES = 128
QBLK = 128
VMEM_LIMIT = 56 * 1024 * 1024

_BF = jnp.bfloat16
_F32 = jnp.float32


def _dot(a, b):
    return jnp.dot(a, b, preferred_element_type=_F32)


def _dot_nt(a, b):
    return lax.dot_general(a, b, (((1,), (1,)), ((), ())), preferred_element_type=_F32)


def _rms_rows(x, gain):
    return x * lax.rsqrt(jnp.mean(x * x, axis=-1, keepdims=True) + EPS) * gain


def _group_rms(z, gmat, gain):
    ssq = _dot((z * z).astype(_BF), gmat) * (1.0 / HEAD_DIM)
    return z * lax.rsqrt(ssq + EPS) * gain


def _inproj_body(x_ref, nm_ref, wq_ref, wkv_ref, wg_ref, wc_ref, qg_ref, kg_ref, g512_ref, g128_ref,
                 q_ref, cmp_ref, sel_ref, win_ref, gate_ref, cb_ref, u_ref, *kv_t_refs):
    a = _rms_rows(x_ref[...], nm_ref[...]).astype(_BF)
    zq = _dot(a, wq_ref[...])
    q_ref[...] = (_group_rms(zq, g512_ref[...], qg_ref[...]) * SCALE).astype(_BF)
    zkv = _dot(a, wkv_ref[...])
    g128 = g128_ref[...]
    halves = ((zkv[:, 0:128], zkv[:, 128:256]),
              (_group_rms(zkv[:, 256:384], g128, kg_ref[0:1, :]), zkv[:, 384:512]),
              (_group_rms(zkv[:, 512:640], g128, kg_ref[1:2, :]), zkv[:, 640:768]))
    for i, (ref, (k, v)) in enumerate(zip((cmp_ref, sel_ref, win_ref), halves)):
        ref[:, 0:128] = k
        ref[:, 128:256] = v
        if kv_t_refs:
            kv_t_refs[i][0, 0:128, :] = k.T
            kv_t_refs[i][0, 128:256, :] = v.T
    gate_ref[...] = jax.nn.sigmoid(_dot(a, wg_ref[...]))
    zc = _dot(a, wc_ref[...])
    cb_ref[...] = zc[:, 0:512]
    u_ref[...] = zc[:, 512:1024] * zc[:, 1024:1536]


def _inproj(x, w, seq_len=None):
    n = x.shape[0]
    tm = min(256, n)
    row = lambda c: pl.BlockSpec((tm, c), lambda i: (i, 0))
    full = lambda a: pl.BlockSpec(a.shape, lambda i: (0,) * a.ndim)
    consts = (w['norm_mix'], w['wq'], w['wkv'], w['wg'], w['wc'], w['q_gain'], w['k_gain'], w['g512'], w['g128'])
    out_specs = [row(512), row(256), row(256), row(256), row(128), row(512), row(512)]
    out_shape = ([jax.ShapeDtypeStruct((n, 512), _BF)] + [jax.ShapeDtypeStruct((n, 256), _F32)] * 3
                 + [jax.ShapeDtypeStruct((n, 128), _F32)] + [jax.ShapeDtypeStruct((n, 512), _F32)] * 2)
    if seq_len is not None:
        nt = seq_len // tm
        lead = nt - WINDOW // tm
        kv_t = pl.BlockSpec((1, 256, tm), lambda i: (i // nt, 0, i % nt))
        win_t = pl.BlockSpec((1, 256, tm), lambda i: (i // nt, 0, jnp.maximum(i % nt - lead, 0)))
        out_specs += [kv_t, kv_t, win_t]
        out_shape += [jax.ShapeDtypeStruct((n // seq_len, 256, seq_len), _F32)] * 2 + [
            jax.ShapeDtypeStruct((n // seq_len, 256, WINDOW), _F32)]
    return pl.pallas_call(
        _inproj_body,
        grid=(n // tm,),
        in_specs=[row(D_MODEL)] + [full(c) for c in consts],
        out_specs=out_specs,
        out_shape=out_shape,
        compiler_params=pltpu.CompilerParams(dimension_semantics=("arbitrary",), vmem_limit_bytes=VMEM_LIMIT),
        name="inproj",
    )(x, *consts)


FF_CHUNK = 1024


def _tail_body(halo, h_ref, o_ref, cb_ref, u_ref, up_ref, p_ref, cw_ref, on_ref, nmlp_ref, nple_ref,
               wout_hbm, wup_hbm, wdn_hbm, wgate_hbm, wproj_hbm, y_ref,
               uext_ref, wout_ref, wup_ref, wdn_ref, wgate_ref, wproj_ref, wsem):
    @pl.when(pl.program_id(0) == 0)
    def _load_weights():
        copies = [pltpu.make_async_copy(src, dst, wsem.at[i]) for i, (src, dst) in enumerate(
            ((wout_hbm, wout_ref), (wup_hbm, wup_ref), (wdn_hbm, wdn_ref), (wgate_hbm, wgate_ref),
             (wproj_hbm, wproj_ref)))]
        for c in copies:
            c.start()
        for c in copies:
            c.wait()

    tm = h_ref.shape[0]
    u = u_ref[...]
    if halo:
        first = (pl.program_id(0) % halo) == 0
        prev = jnp.where(first, 0.0, up_ref[...])
        uext_ref[0:8, :] = prev
        uext_ref[8:tm + 8, :] = u
        u2 = uext_ref[6:tm + 6, :]
        u1 = uext_ref[7:tm + 7, :]
    else:
        u2 = up_ref[0]
        u1 = up_ref[1]
    yc = cw_ref[0:1, :] * u2 + cw_ref[1:2, :] * u1 + cw_ref[2:3, :] * u
    mix_a = _rms_rows(o_ref[...], on_ref[:, 0:ATT_WIDTH]).astype(_BF)
    mix_c = _rms_rows(cb_ref[...] * yc, on_ref[:, ATT_WIDTH:]).astype(_BF)
    h = h_ref[...] + _dot(mix_a, wout_ref[0:ATT_WIDTH, :]) + _dot(mix_c, wout_ref[ATT_WIDTH:, :])
    a = _rms_rows(h, nmlp_ref[...]).astype(_BF)
    y_ref[...] = h
    for c in range(D_FF // FF_CHUNK):
        t = jnp.maximum(_dot(a, wup_ref[:, c * FF_CHUNK:(c + 1) * FF_CHUNK]), 0.0)
        y_ref[...] += _dot((t * t).astype(_BF), wdn_ref[c * FF_CHUNK:(c + 1) * FF_CHUNK, :])
    h = y_ref[...]
    a = _rms_rows(h, nple_ref[...]).astype(_BF)
    gate = jax.nn.sigmoid(_dot(a, wgate_ref[...]))
    y_ref[...] = h + gate * _dot(p_ref[...].astype(_BF), wproj_ref[...])


def _tail(h, o_att, cb, u, u_prev, p, w, seq_len):
    n = h.shape[0]
    tm = min(256, n)
    row = lambda c: pl.BlockSpec((tm, c), lambda i: (i, 0))
    const = lambda a: pl.BlockSpec(a.shape, lambda i: (0,) * a.ndim)
    if seq_len is not None:
        halo = seq_len // tm
        up_spec = pl.BlockSpec((8, CONV_DIM), lambda i: (jnp.maximum(i * (tm // 8) - 1, 0), 0))
        up = u
    else:
        halo = 0
        up_spec = pl.BlockSpec((2, tm, CONV_DIM), lambda i: (0, i, 0))
        up = u_prev
    consts = (w['conv_w'], w['out_norm'], w['norm_mlp'], w['norm_ple'])
    mats = (w['w_out'], w['w_up'], w['w_down'], w['w_ple_gate'], w['w_ple_proj'])
    return pl.pallas_call(
        functools.partial(_tail_body, halo),
        grid=(n // tm,),
        in_specs=[row(D_MODEL), row(ATT_WIDTH), row(CONV_DIM), row(CONV_DIM), up_spec, row(PLE_DIM)]
        + [const(c) for c in consts] + [pl.BlockSpec(memory_space=pl.ANY)] * len(mats),
        out_specs=row(D_MODEL),
        out_shape=jax.ShapeDtypeStruct((n, D_MODEL), _F32),
        scratch_shapes=[pltpu.VMEM((tm + 8, CONV_DIM), _F32)] + [pltpu.VMEM(m.shape, _BF) for m in mats]
        + [pltpu.SemaphoreType.DMA((len(mats),))],
        compiler_params=pltpu.CompilerParams(dimension_semantics=("arbitrary",), vmem_limit_bytes=VMEM_LIMIT),
        name="tail",
    )(h, o_att, cb, u, up, p, *consts, *mats)


def _split_heads(x, lane_lo):
    xr = pltpu.roll(x, HEAD_DIM, axis=1)
    zero = jnp.zeros_like(x)
    a = (jnp.where(lane_lo, x, zero), jnp.where(lane_lo, xr, zero))
    b = (jnp.where(lane_lo, zero, xr), jnp.where(lane_lo, zero, x))
    return a, b


def _compress_rows(load, ns, bd_ref, pe_ref):
    acc = [jnp.zeros((2 * ns, LANES), _F32), jnp.zeros((2 * ns, LANES), _F32)]
    for l in range(CMP_BLOCK):
        for plane in range(2):
            x = jnp.concatenate([load(plane, l), load(plane, CMP_BLOCK + l)], axis=0)
            x = x + pe_ref[l, plane:plane + 1, :]
            acc[plane] = acc[plane] + _dot(x.astype(_BF), bd_ref[l, plane])
    return acc


def _compress_prompt_body(slab_ref, bd_ref, pe_ref, kg_ref, g128_ref, kA_ref, kB_ref, vA_ref, vB_ref):
    ns = slab_ref.shape[1] // (2 * SEL_BLOCK)
    k, v = _compress_rows(lambda plane, t0: slab_ref[0, pl.ds(2 * t0 + plane, ns, stride=2 * SEL_BLOCK), :],
                          ns, bd_ref, pe_ref)
    k = _group_rms(k, g128_ref[...], kg_ref[...])
    lane_lo = lax.broadcasted_iota(jnp.int32, k.shape, 1) < HEAD_DIM
    ka, kb = _split_heads(k, lane_lo)
    va, vb = _split_heads(v, lane_lo)
    pad = jnp.zeros((LANES - 2 * ns, LANES), _BF)
    for g in range(N_KV):
        for ref, val in ((kA_ref, ka[g]), (kB_ref, kb[g]), (vA_ref, va[g]), (vB_ref, vb[g])):
            ref[0, g, 0:2 * ns, :] = val.astype(_BF)
            if 2 * ns < LANES:
                ref[0, g, 2 * ns:, :] = pad


def _compress_prompt(slab, w):
    b, t, _ = slab.shape
    assert t % QBLK == 0 and t // CMP_BLOCK <= LANES
    slab = slab.reshape(b, 2 * t, LANES)
    full = lambda a: pl.BlockSpec(a.shape, lambda i: (0,) * a.ndim)
    consts = (w['bd'], w['pe_t'], w['kc_gain'], w['g128'])
    out = jax.ShapeDtypeStruct((b, N_KV, LANES, LANES), _BF)
    return pl.pallas_call(
        _compress_prompt_body,
        grid=(b,),
        in_specs=[pl.BlockSpec((1, 2 * t, LANES), lambda i: (i, 0, 0))] + [full(c) for c in consts],
        out_specs=[pl.BlockSpec((1, N_KV, LANES, LANES), lambda i: (i, 0, 0, 0))] * 4,
        out_shape=[out] * 4,
        compiler_params=pltpu.CompilerParams(dimension_semantics=("arbitrary",), vmem_limit_bytes=VMEM_LIMIT),
        name="compress_prompt",
    )(slab, *consts)


BUILD_ROWS = 256


def _top_extra(impb, cand, lane_f, n_extra):
    v = jnp.where(cand, impb, -1.0)
    picked = jnp.zeros(impb.shape, _F32)
    for _ in range(n_extra):
        mx = jnp.max(v, axis=-1, keepdims=True)
        first = jnp.min(jnp.where(v == mx, lane_f, 1e9), axis=-1, keepdims=True)
        hit = lane_f == first
        picked = jnp.where(hit, 1.0, picked)
        v = jnp.where(hit, -1.0, v)
    return picked


def _attn_prompt_body(q_ref, ks_ref, kw_ref, gate_ref, kcA_ref, kcB_ref, vcA_ref, vcB_ref, cbias_ref, tiles_ref,
                      eg_ref, o_ref, ksA, ksB, vsA, vsB, kwA, kwB, vwA, vwB, s_scr, m_scr, acc_scr):
    qb = pl.program_id(1)
    t_len = ks_ref.shape[1]
    n_sel = t_len // SEL_BLOCK

    @pl.when(qb == 0)
    def _build():
        def chunk(c, carry):
            r0 = pl.multiple_of(c * BUILD_ROWS, BUILD_ROWS)
            rows = pl.ds(r0, BUILD_ROWS)
            lane = lax.broadcasted_iota(jnp.int32, (BUILD_ROWS, LANES), 1)
            blk = (r0 + lax.broadcasted_iota(jnp.int32, (BUILD_ROWS, LANES), 0)) // SEL_BLOCK
            lane_lo = lane < HEAD_DIM
            oh_hi = jnp.where(lane == blk + HEAD_DIM, 1.0, 0.0)
            oh_lo = jnp.where(lane == blk, 1.0, 0.0)
            one_hi = jnp.where(lane == HEAD_DIM, 1.0, 0.0)
            one_lo = jnp.where(lane == 0, 1.0, 0.0)
            for src, k_a, k_b, v_a, v_b, onehot in ((ks_ref, ksA, ksB, vsA, vsB, True),
                                                    (kw_ref, kwA, kwB, vwA, vwB, False)):
                ka, kb = _split_heads(src[0, rows, 0:128], lane_lo)
                va, vb = _split_heads(src[0, rows, 128:256], lane_lo)
                for g in range(N_KV):
                    if onehot:
                        k_a[g, rows, :] = jnp.where(lane_lo, ka[g], oh_hi).astype(_BF)
                        k_b[g, rows, :] = jnp.where(lane_lo, oh_lo, kb[g]).astype(_BF)
                    else:
                        k_a[g, rows, :] = ka[g].astype(_BF)
                        k_b[g, rows, :] = kb[g].astype(_BF)
                    v_a[g, rows, :] = jnp.where(lane_lo, va[g], one_hi).astype(_BF)
                    v_b[g, rows, :] = jnp.where(lane_lo, one_lo, vb[g]).astype(_BF)
            return carry
        lax.fori_loop(0, t_len // BUILD_ROWS, chunk, 0)

    lane2 = lax.broadcasted_iota(jnp.int32, (2 * QBLK, LANES), 1)
    lo2 = lane2 < HEAD_DIM
    lane1 = lax.broadcasted_iota(jnp.int32, (QBLK, LANES), 1)
    lane1_f = lane1.astype(_F32)
    tpos = qb * QBLK + lax.broadcasted_iota(jnp.int32, (QBLK, LANES), 0)
    cur = tpos // SEL_BLOCK
    forced = (lane1 == 0) | (lane1 == cur) | (lane1 == cur - 1)
    cand = (lane1 >= 1) & (lane1 <= cur - 2)
    few = cur <= TOP_N - 1

    gates = gate_ref[0]
    g_hi = gates.astype(_BF)
    g_lo = (gates - g_hi.astype(_F32)).astype(_BF)
    gexp = _dot(g_hi, eg_ref[...]) + _dot(g_lo, eg_ref[...])

    def normalize(acc, x):
        l = jnp.sum(jnp.where(lane2 == (HEAD_DIM if x == 0 else 0), acc, 0.0), axis=-1, keepdims=True)
        keep = lo2 if x == 0 else jnp.logical_not(lo2)
        return jnp.where(keep, acc / l, 0.0)

    o_cmp, q_plain, q_pair, importance = [], [], [], []
    for g in range(N_KV):
        qs = jnp.concatenate([q_ref[0, :, (2 * g) * LANES:(2 * g + 1) * LANES],
                              q_ref[0, :, (2 * g + 1) * LANES:(2 * g + 2) * LANES]], axis=0)
        zero = jnp.zeros_like(qs)
        q_a = jnp.where(lo2, qs, zero)
        q_b = jnp.where(lo2, zero, qs)
        q_pair.append(qs)

        def cmp_probs(qx, k_ref, x):
            s = _dot_nt(qx, k_ref[0, g]) + cbias_ref[0, g, x]
            m = jnp.max(s, axis=-1, keepdims=True)
            e = jnp.where(s > 0.5 * NEG, jnp.exp(s - m), 0.0)
            l = jnp.sum(e, axis=-1, keepdims=True)
            return e / jnp.where(l > 0.0, l, 1.0)
        p_a = cmp_probs(q_a, kcA_ref, 0)
        p_b = cmp_probs(q_b, kcB_ref, 1)
        o_c = _dot(p_a.astype(_BF), vcA_ref[0, g]) + _dot(p_b.astype(_BF), vcB_ref[0, g])
        imp = p_a[0:QBLK] + p_a[QBLK:] + p_b[0:QBLK] + p_b[QBLK:]
        importance.append(imp + pltpu.roll(imp, LANES - n_sel, axis=1))
        o_cmp.append(o_c)
        q_plain += [q_a, q_b]

    two = lambda a: jnp.concatenate([a, a], axis=0)
    picked = lax.cond(qb * QBLK >= TOP_N * SEL_BLOCK,
                      lambda: _top_extra(jnp.concatenate(importance, axis=0), two(cand), two(lane1_f), TOP_N - 3),
                      lambda: jnp.zeros((2 * QBLK, LANES), _F32))
    q_aug = []
    for g in range(N_KV):
        chosen = forced | (few & (lane1 <= cur)) | ((picked[g * QBLK:(g + 1) * QBLK] > 0.5) & jnp.logical_not(few))
        sb_lo = jnp.where(chosen | (lane1 >= n_sel), 0.0, NEG)
        sb_hi = pltpu.roll(sb_lo, HEAD_DIM, axis=1)
        q_aug += [jnp.where(lo2, q_pair[g], two(sb_hi).astype(_BF)), jnp.where(lo2, two(sb_lo).astype(_BF), q_pair[g])]

    chains = [(g, x) for g in range(N_KV) for x in range(2)]
    k_sel, v_sel, k_win, v_win = (ksA, ksB), (vsA, vsB), (kwA, kwB), (vwA, vwB)
    n_chunk = t_len // QBLK
    n_win = WINDOW // QBLK + 1

    w_first = jnp.maximum(qb - (n_win - 1), 0)
    w_rows = pl.ds(pl.multiple_of(w_first * QBLK, QBLK), n_win * QBLK)
    o_win = []
    for ch, (g, x) in enumerate(chains):
        s = _dot_nt(q_plain[ch], k_win[x][g, w_rows, :])
        parts = []
        for i in range(n_win):
            idx = qb - w_first - i
            parts.append(s[:, i * QBLK:(i + 1) * QBLK] + tiles_ref[jnp.where(idx < 0, n_win, idx), g, x])
        m = parts[0]
        for part in parts[1:]:
            m = jnp.maximum(m, part)
        m = jnp.max(m, axis=-1, keepdims=True)
        p = jnp.concatenate([jnp.exp(part - m).astype(_BF) for part in parts], axis=1)
        o_win.append(normalize(_dot(p, v_win[x][g, w_rows, :]), x))

    n_quart = 4
    qw = t_len // n_quart
    per_q = qw // QBLK
    m_scr[...] = jnp.full(m_scr.shape, NEG, _F32)
    for qi in range(n_quart):
        @pl.when(qb * QBLK >= qi * qw)
        def _scores(qi=qi):
            for ch, (g, x) in enumerate(chains):
                s = _dot_nt(q_aug[ch], k_sel[x][g, qi * qw:(qi + 1) * qw, :])
                m = m_scr[ch]
                for ci in range(per_q):
                    c = qi * per_q + ci
                    idx = qb - c
                    sb = s[:, ci * QBLK:(ci + 1) * QBLK] + tiles_ref[jnp.where(idx < 0, n_win, jnp.minimum(idx, 2)), g, x]
                    s_scr[ch, :, c * QBLK:(c + 1) * QBLK] = sb
                    m = jnp.maximum(m, sb)
                m_scr[ch] = m
    m_sel = [jnp.max(m_scr[ch], axis=-1, keepdims=True) for ch in range(len(chains))]

    acc_scr[...] = jnp.zeros(acc_scr.shape, _F32)
    for qi in range(n_quart):
        @pl.when(qb * QBLK >= qi * qw)
        def _weighted(qi=qi):
            for ch, (g, x) in enumerate(chains):
                p = jnp.exp(s_scr[ch, :, qi * qw:(qi + 1) * qw] - m_sel[ch]).astype(_BF)
                acc_scr[ch] += _dot(p, v_sel[x][g, qi * qw:(qi + 1) * qw, :])

    for g in range(N_KV):
        o_c = o_cmp[g]
        o_s = normalize(acc_scr[2 * g], 0) + normalize(acc_scr[2 * g + 1], 1)
        o_w = o_win[2 * g] + o_win[2 * g + 1]
        for pr in range(2):
            rows = slice(pr * QBLK, (pr + 1) * QBLK)
            col = (2 * g + pr) * LANES
            o_ref[0, :, col:col + LANES] = (gexp[:, col:col + LANES] * o_c[rows]
                                            + gexp[:, ATT_WIDTH + col:ATT_WIDTH + col + LANES] * o_s[rows]
                                            + gexp[:, 2 * ATT_WIDTH + col:2 * ATT_WIDTH + col + LANES] * o_w[rows])


def _attn_prompt(q, sel, win, gates, kc, cbias, w):
    b, t, _ = sel.shape
    nq = t // QBLK
    assert t // SEL_BLOCK <= CMP_BLOCK and t % (4 * QBLK) == 0 and t >= WINDOW + QBLK
    kcA, kcB, vcA, vcB = kc
    full = lambda a: pl.BlockSpec(a.shape, lambda i, j: (0,) * a.ndim)
    slab = pl.BlockSpec((1, t, 256), lambda i, j: (i, 0, 0))
    kcs = pl.BlockSpec((1, N_KV, LANES, LANES), lambda i, j: (i, 0, 0, 0))
    n_chain = 2 * N_KV
    scratch = [pltpu.VMEM((N_KV, t, LANES), _BF)] * 8 + [
        pltpu.VMEM((n_chain, 2 * QBLK, t), _F32),
        pltpu.VMEM((n_chain, 2 * QBLK, LANES), _F32), pltpu.VMEM((n_chain, 2 * QBLK, LANES), _F32)]
    return pl.pallas_call(
        _attn_prompt_body,
        grid=(b, nq),
        in_specs=[pl.BlockSpec((1, QBLK, ATT_WIDTH), lambda i, j: (i, j, 0)), slab, slab,
                  pl.BlockSpec((1, QBLK, LANES), lambda i, j: (i, j, 0)), kcs, kcs, kcs, kcs,
                  pl.BlockSpec((1, N_KV, 2, 2 * QBLK, LANES), lambda i, j: (j, 0, 0, 0, 0)),
                  full(w['tiles']), full(w['egate'])],
        out_specs=pl.BlockSpec((1, QBLK, ATT_WIDTH), lambda i, j: (i, j, 0)),
        out_shape=jax.ShapeDtypeStruct((b, t, ATT_WIDTH), _F32),
        scratch_shapes=scratch,
        compiler_params=pltpu.CompilerParams(dimension_semantics=("arbitrary", "arbitrary"),
                                             vmem_limit_bytes=VMEM_LIMIT),
        name="attn_prompt",
    )(q, sel, win, gates, kcA, kcB, vcA, vcB, cbias, w['tiles'], w['egate'])


N_PICK = TOP_N - 3
N_SLOT = TOP_N - 1
PAGE_ROWS = 2 * KV_WIDTH


PAIR_TOKENS = 2 * PAGE_SIZE


def _sample_cmp_body(pt_ref, q_ref, cache_ref, bd_ref, pet_ref, perm_ref, kg_ref, g128_ref, sb_ref, oc_ref, pick_ref,
                     buf, rows_scr, sem):
    b = pl.program_id(0)
    nb = pl.num_programs(0)
    n_pairs = pt_ref.shape[1] // 2
    nc = 8 * n_pairs
    slot = b % 2

    def page_copy(bb, pair, half, sl):
        return pltpu.make_async_copy(cache_ref.at[pt_ref[bb, 2 * pair + half]],
                                     buf.at[sl, pair, :, :, half * PAGE_SIZE:(half + 1) * PAGE_SIZE], sem.at[sl])

    def for_pages(fn):
        def step(pair, c):
            fn(pair, 0)
            fn(pair, 1)
            return c
        lax.fori_loop(0, n_pairs, step, 0)

    @pl.when(b == 0)
    def _prime():
        for_pages(lambda pair, half: page_copy(0, pair, half, 0).start())

    @pl.when(b + 1 < nb)
    def _prefetch():
        for_pages(lambda pair, half: page_copy(b + 1, pair, half, 1 - slot).start())

    for_pages(lambda pair, half: page_copy(b, pair, half, slot).wait())

    def regroup(pair, c):
        for plane in range(2):
            x = (buf[slot, pair, plane] + pet_ref[plane]).astype(_BF)
            rows_scr[plane, pair] = _dot_nt(perm_ref[...], x)
        return c
    lax.fori_loop(0, n_pairs, regroup, 0, unroll=4)

    acc = [jnp.zeros((nc, LANES), _F32), jnp.zeros((nc, LANES), _F32)]
    for l in range(CMP_BLOCK):
        for plane in range(2):
            x = rows_scr[plane, :, 8 * l:8 * (l + 1), :].reshape(nc, LANES)
            acc[plane] = acc[plane] + _dot(x.astype(_BF), bd_ref[l, plane])
    k = _group_rms(acc[0], g128_ref[...], kg_ref[...]).astype(_BF)
    v = acc[1].astype(_BF)
    s = _dot_nt(q_ref[0], k) + sb_ref[...]
    m = jnp.max(s, axis=-1, keepdims=True)
    e = jnp.exp(s - m)
    p = e / jnp.sum(e, axis=-1, keepdims=True)
    oc_ref[0] = _dot(p.astype(_BF), v)
    imp8 = p + pltpu.roll(p, nc - 1, axis=1)
    rows = [imp8[h:h + 1] for h in range(N_HEADS)]
    imp_g = [rows[g * GROUP] + rows[g * GROUP + 1] + rows[g * GROUP + 2] + rows[g * GROUP + 3] for g in range(N_KV)]
    imp = jnp.concatenate(imp_g + [jnp.full((8 - N_KV, nc), -1.0, _F32)], axis=0)
    lane = lax.broadcasted_iota(jnp.int32, (8, nc), 1)
    lane_f = lane.astype(_F32)
    v_c = jnp.where((lane % 2 == 0) & (lane >= 2) & (lane <= nc - 4), imp, -1.0)
    picks = jnp.zeros((8, nc), _F32)
    for i in range(N_PICK):
        mx = jnp.max(v_c, axis=-1, keepdims=True)
        first = jnp.min(jnp.where(v_c == mx, lane_f, 1e9), axis=-1, keepdims=True)
        picks = jnp.where(lane == i, first * 0.5, picks)
        v_c = jnp.where(lane_f == first, -1.0, v_c)
    pick_ref[0] = picks.astype(jnp.int32)


def _sample_cmp(page_table, q8, cache, sbias, w):
    db, n_pages = page_table.shape
    nc = 4 * n_pages
    assert n_pages % 2 == 0 and nc // 2 - 2 >= N_PICK
    full = lambda a: pl.BlockSpec(a.shape, lambda i, pt: (0,) * a.ndim)
    consts = (w['bd'], w['pe_tok'], w['perm'], w['kc_gain'], w['g128'], sbias)
    grid_spec = pltpu.PrefetchScalarGridSpec(
        num_scalar_prefetch=1,
        grid=(db,),
        in_specs=[pl.BlockSpec((1, 8, LANES), lambda i, pt: (i, 0, 0)), pl.BlockSpec(memory_space=pl.ANY)]
        + [full(c) for c in consts],
        out_specs=[pl.BlockSpec((1, 8, LANES), lambda i, pt: (i, 0, 0)),
                   pl.BlockSpec((1, 8, nc), lambda i, pt: (i, 0, 0))],
        scratch_shapes=[pltpu.VMEM((2, n_pages // 2, 2, KV_WIDTH, PAIR_TOKENS), _F32),
                        pltpu.VMEM((2, n_pages // 2, PAIR_TOKENS, KV_WIDTH), _F32),
                        pltpu.SemaphoreType.DMA((2,))],
    )
    return pl.pallas_call(
        _sample_cmp_body,
        grid_spec=grid_spec,
        out_shape=[jax.ShapeDtypeStruct((db, 8, LANES), _F32), jax.ShapeDtypeStruct((db, 8, nc), jnp.int32)],
        compiler_params=pltpu.CompilerParams(dimension_semantics=("arbitrary",), vmem_limit_bytes=VMEM_LIMIT),
        name="sample_cmp",
    )(page_table, q8, cache, *consts)


def _sample_attn_body(pt_ref, pick_ref, q_ref, cache_ref, win_ref, nsel_ref, nwin_ref, ncol_ref, gate_ref, oc_ref,
                      bsel_ref, b0_ref, bwin_ref, o_ref, wout_ref, kb, sem):
    b = pl.program_id(0)
    nb = pl.num_programs(0)
    nj = 2 * pt_ref.shape[1]
    wlen = win_ref.shape[2]
    slot = b % 2

    def block_of(bb, g, s):
        if s == 0:
            return 0
        if s == N_SLOT - 1:
            return nj - 1
        return pick_ref[bb, g * N_PICK + (s - 1)]

    def tile_copy(bb, g, s, plane, sl):
        page = pt_ref[bb, block_of(bb, g, s) // 2]
        return pltpu.make_async_copy(cache_ref.at[page, plane * N_KV + g],
                                     kb.at[sl, g, plane, :, pl.ds(s * PAGE_SIZE, PAGE_SIZE)], sem.at[sl])

    def for_tiles(fn):
        for g in range(N_KV):
            for s in range(N_SLOT):
                for plane in range(2):
                    fn(g, s, plane)

    @pl.when(b == 0)
    def _prime():
        for_tiles(lambda g, s, plane: tile_copy(0, g, s, plane, 0).start())

    @pl.when(b + 1 < nb)
    def _prefetch():
        for_tiles(lambda g, s, plane: tile_copy(b + 1, g, s, plane, 1 - slot).start())

    q8 = q_ref[0]
    q8f = q8.astype(_F32)
    head_g = lax.broadcasted_iota(jnp.int32, (8, HEAD_DIM), 0) // GROUP
    own_half = lambda x: jnp.where(head_g == 0, x[:, 0:HEAD_DIM], x[:, HEAD_DIM:])
    b0 = b0_ref[:, 0:1]

    x = win_ref[0]
    lane_w = lax.broadcasted_iota(jnp.int32, x.shape, 1)
    wout_ref[0] = jnp.where(lane_w == wlen - 1, ncol_ref[0], pltpu.roll(x, wlen - 1, axis=1))
    sw = _dot(q8, x[0:KV_WIDTH, :].astype(_BF)) + bwin_ref[...]
    sw_new = jnp.sum(q8f * nwin_ref[0][:, 0:KV_WIDTH], axis=-1, keepdims=True) + b0
    mw = jnp.maximum(jnp.max(sw, axis=-1, keepdims=True), sw_new)
    ew = jnp.exp(sw - mw)
    ew_new = jnp.exp(sw_new - mw)
    o_w = _dot_nt(ew.astype(_BF), x[KV_WIDTH:, :].astype(_BF)) + ew_new * nwin_ref[0][:, KV_WIDTH:]
    o_w = own_half(o_w / (jnp.sum(ew, axis=-1, keepdims=True) + ew_new))

    for_tiles(lambda g, s, plane: tile_copy(b, g, s, plane, slot).wait())

    ss_new = jnp.sum(q8f * nsel_ref[0][:, 0:KV_WIDTH], axis=-1, keepdims=True) + b0
    lane_half = lax.broadcasted_iota(jnp.int32, (8, PAGE_SIZE), 1) // SEL_BLOCK
    o_sel = []
    for g in range(N_KV):
        pieces = []
        for s in range(N_SLOT):
            j = jnp.full((8, PAGE_SIZE), block_of(b, g, s), jnp.int32)
            if s == 0:
                tab = bsel_ref[0]
            elif s == N_SLOT - 1:
                tab = bsel_ref[2]
            else:
                tab = jnp.where(j == nj - 2, bsel_ref[1], bsel_ref[0])
            pieces.append(jnp.where(lane_half == j % 2, tab, NEG))
        ss = _dot(q8[:, g * HEAD_DIM:(g + 1) * HEAD_DIM], kb[slot, g, 0].astype(_BF)) + jnp.concatenate(pieces, axis=1)
        ms = jnp.maximum(jnp.max(ss, axis=-1, keepdims=True), ss_new)
        es = jnp.exp(ss - ms)
        es_new = jnp.exp(ss_new - ms)
        v_new = nsel_ref[0][:, KV_WIDTH + g * HEAD_DIM:KV_WIDTH + (g + 1) * HEAD_DIM]
        o_g = _dot_nt(es.astype(_BF), kb[slot, g, 1].astype(_BF)) + es_new * v_new
        o_sel.append(o_g / (jnp.sum(es, axis=-1, keepdims=True) + es_new))
    o_s = jnp.where(head_g == 0, o_sel[0], o_sel[1])
    gates = gate_ref[0]
    o_ref[0] = gates[:, 0:1] * own_half(oc_ref[0]) + gates[:, 1:2] * o_s + gates[:, 2:3] * o_w


def _sample_attn(page_table, picks, q8, cache, win, new_sel, new_win, gates8, o_c, tabs):
    db, n_pages = page_table.shape
    wlen = win.shape[2]
    full = lambda a: pl.BlockSpec(a.shape, lambda i, pt, pk: (0,) * a.ndim)
    per_b = lambda r, c: pl.BlockSpec((1, r, c), lambda i, pt, pk: (i, 0, 0))
    bsel, b0, bwin = tabs
    grid_spec = pltpu.PrefetchScalarGridSpec(
        num_scalar_prefetch=2,
        grid=(db,),
        in_specs=[per_b(8, LANES), pl.BlockSpec(memory_space=pl.ANY), per_b(2 * KV_WIDTH, wlen), per_b(1, 256),
                  per_b(1, 256), per_b(2 * KV_WIDTH, 1), per_b(8, LANES), per_b(8, LANES),
                  full(bsel), full(b0), full(bwin)],
        out_specs=[per_b(8, HEAD_DIM), per_b(2 * KV_WIDTH, wlen)],
        scratch_shapes=[pltpu.VMEM((2, N_KV, 2, HEAD_DIM, N_SLOT * PAGE_SIZE), _F32), pltpu.SemaphoreType.DMA((2,))],
    )
    return pl.pallas_call(
        _sample_attn_body,
        grid_spec=grid_spec,
        out_shape=[jax.ShapeDtypeStruct((db, 8, HEAD_DIM), _F32), jax.ShapeDtypeStruct((db, 2 * KV_WIDTH, wlen), _F32)],
        compiler_params=pltpu.CompilerParams(dimension_semantics=("arbitrary",), vmem_limit_bytes=VMEM_LIMIT),
        name="sample_attn",
    )(page_table, picks, q8, cache, win, new_sel, new_win, new_win[:, 0, :, None], gates8, o_c, bsel, b0, bwin)


def _rel_bucket(dist):
    n = np.maximum(np.asarray(dist), 0)
    max_exact = RP_BUCKETS // 2
    nf = np.maximum(n, 1).astype(np.float32)
    large = max_exact + (np.log(nf / np.float32(max_exact)) / np.float32(math.log(RP_MAX_DIST / max_exact))
                         * np.float32(RP_BUCKETS - max_exact)).astype(np.int32)
    large = np.minimum(large, RP_BUCKETS - 1)
    return np.where(n < max_exact, n, large)


def _bias_of(rel_bias, dist):
    onehot = _rel_bucket(dist)[..., None, None] == np.arange(RP_BUCKETS)[:, None]
    return jnp.sum(jnp.where(onehot, rel_bias, 0.0), axis=-2)


def _head_major(x, lead):
    n = x.shape[-2]
    x = x.reshape(lead + (QBLK, n, N_KV, 2, 2))
    nl = len(lead)
    x = jnp.transpose(x, tuple(range(nl)) + (nl + 2, nl + 4, nl + 3, nl, nl + 1))
    return x.reshape(lead + (N_KV, 2, 2 * QBLK, n))


def _prompt_tables(rel_bias, t_len):
    nq = t_len // QBLK
    ns = t_len // SEL_BLOCK
    ti = np.arange(QBLK)[:, None]
    ki = np.arange(QBLK)[None, :]
    n_idx = WINDOW // QBLK + 1
    pad = QBLK - 1
    bvec = _bias_of(rel_bias, np.arange(-pad, t_len + pad))
    wins = jnp.stack([bvec[i * QBLK:i * QBLK + 2 * QBLK - 1][::-1] for i in range(n_idx)])
    skew = jnp.tile(wins, (1, QBLK + 1, 1))[:, :QBLK * 2 * QBLK].reshape(n_idx, QBLK, 2 * QBLK, N_HEADS)
    tiles = skew[:, ::-1, :QBLK]
    dist = np.stack([i * QBLK + ti - ki for i in range(n_idx)])
    valid = dist >= 0
    valid[n_idx - 1] &= dist[n_idx - 1] < WINDOW
    tiles = jnp.where(valid[..., None], tiles, NEG)
    tiles = jnp.concatenate([tiles, jnp.full_like(tiles[:1], NEG)])
    tiles = _head_major(tiles, (n_idx + 1,))
    shift = CMP_BLOCK * ns * 2
    bcmp = _bias_of(rel_bias, np.arange(-shift, t_len))
    cols = [bcmp[shift - (CMP_BLOCK * blk + CMP_BLOCK - 1):][:t_len]
            for blk in list(range(0, 2 * ns, 2)) + list(range(1, 2 * ns, 2))]
    cb = jnp.stack(cols, axis=1)
    cb = jnp.pad(cb, ((0, 0), (0, LANES - 2 * ns), (0, 0)))
    lane = np.arange(LANES)
    blk = np.where(lane < ns, 2 * lane, 2 * (lane - ns) + 1)
    dist_c = np.arange(t_len)[:, None] - (CMP_BLOCK * blk + CMP_BLOCK - 1)[None, :]
    valid_c = (dist_c >= 0) & (lane < 2 * ns)[None, :]
    cb = jnp.where(valid_c[..., None], cb, NEG)
    cb = _head_major(cb.reshape(nq, QBLK, LANES, N_HEADS), (nq,))
    return tiles, cb


def _sample_tables(rel_bias, past_len, wlen):
    blk = np.arange(past_len // CMP_BLOCK)
    sbias = _bias_of(rel_bias, past_len - (CMP_BLOCK * blk + CMP_BLOCK - 1)).T
    pos = np.arange(PAGE_SIZE) % SEL_BLOCK
    bsel = jnp.stack([_bias_of(rel_bias, np.full(PAGE_SIZE, past_len)).T,
                      _bias_of(rel_bias, 2 * SEL_BLOCK - pos).T,
                      _bias_of(rel_bias, SEL_BLOCK - pos).T])
    b0 = _bias_of(rel_bias, np.zeros((LANES,), np.int32)).T
    tok = np.arange(wlen)
    bwin = jnp.where((tok >= 1)[None, :], _bias_of(rel_bias, wlen - tok).T, NEG)
    return sbias, (bsel, b0, bwin)


def _prep(norm_mix, w_in, q_norm, k_norm, cmp_pe, w_cmp, conv_w, out_norm, w_out, norm_mlp, w_up, w_down,
          norm_ple, w_ple_gate, w_ple_proj):
    w_in = w_in[0]
    o_kv = ATT_WIDTH
    o_g = o_kv + 6 * KV_WIDTH
    o_c = o_g + 3 * N_HEADS
    bd = jnp.einsum('gh,plde->lpgdhe', jnp.eye(N_KV, dtype=_F32), w_cmp[0]).reshape(CMP_BLOCK, 2, LANES, LANES)
    pe_t = jnp.transpose(jnp.tile(cmp_pe[0], (1, 1, N_KV)), (1, 0, 2))
    egate = np.zeros((LANES, 3 * ATT_WIDTH), np.float32)
    for br in range(3):
        for h in range(N_HEADS):
            egate[br * N_HEADS + h, br * ATT_WIDTH + h * HEAD_DIM:br * ATT_WIDTH + (h + 1) * HEAD_DIM] = 1.0
    ones = np.ones((HEAD_DIM, HEAD_DIM), np.float32)
    perm = np.zeros((PAIR_TOKENS, PAIR_TOKENS), np.float32)
    tok = np.arange(PAIR_TOKENS)
    perm[(tok % CMP_BLOCK) * (PAIR_TOKENS // CMP_BLOCK) + tok // CMP_BLOCK, tok] = 1.0
    pe_tok = jnp.tile(jnp.transpose(cmp_pe[0], (0, 2, 1)), (1, N_KV, PAIR_TOKENS // CMP_BLOCK))
    return {
        'perm': jnp.asarray(perm, _BF), 'pe_tok': pe_tok,
        'norm_mix': norm_mix[0][None], 'wq': w_in[:, :o_kv].astype(_BF), 'wkv': w_in[:, o_kv:o_g].astype(_BF),
        'wg': jnp.pad(w_in[:, o_g:o_c], ((0, 0), (0, LANES - 3 * N_HEADS))).astype(_BF),
        'wc': w_in[:, o_c:].astype(_BF),
        'q_gain': jnp.tile(q_norm[0], N_HEADS)[None],
        'k_gain': jnp.stack([jnp.tile(k_norm[0, 1], N_KV), jnp.tile(k_norm[0, 2], N_KV)]),
        'kc_gain': jnp.tile(k_norm[0, 0], N_KV)[None],
        'g512': jnp.asarray(np.kron(np.eye(N_HEADS, dtype=np.float32), ones), _BF),
        'g128': jnp.asarray(np.kron(np.eye(N_KV, dtype=np.float32), ones), _BF),
        'bd': bd.astype(_BF), 'pe_t': pe_t,
        'egate': jnp.asarray(egate, _BF),
        'conv_w': conv_w[0], 'out_norm': out_norm[0][None], 'w_out': w_out[0].astype(_BF),
        'norm_mlp': norm_mlp[0][None], 'w_up': w_up[0].astype(_BF), 'w_down': w_down[0].astype(_BF),
        'norm_ple': norm_ple[0][None], 'w_ple_gate': w_ple_gate[0].astype(_BF),
        'w_ple_proj': w_ple_proj[0].astype(_BF),
    }


def kernel(x_prompt, x_sample, p_prompt, p_sample, cache_cmp_kv, cache_sel_kv, state_win_kv, state_conv, page_table, rel_bias, norm_mix, w_in, q_norm, k_norm, cmp_pe, w_cmp, conv_w, out_norm, w_out, norm_mlp, w_up, w_down, norm_ple, w_ple_gate, w_ple_proj):
    bp, tp, _ = x_prompt.shape
    db, ts, _ = x_sample.shape
    assert norm_mix.shape[0] == 1 and ts == 1 and tp >= WINDOW
    n_pages = page_table.shape[1]
    past_len = n_pages * PAGE_SIZE
    wlen = state_win_kv.shape[2]
    assert wlen == WINDOW and past_len >= 4 * SEL_BLOCK
    w = _prep(norm_mix, w_in, q_norm, k_norm, cmp_pe, w_cmp, conv_w, out_norm, w_out, norm_mlp, w_up, w_down,
              norm_ple, w_ple_gate, w_ple_proj)
    w['tiles'], cbias = _prompt_tables(rel_bias, tp)
    sbias, stabs = _sample_tables(rel_bias, past_len, wlen)
    kv6 = lambda a, b, t: a.reshape(1, b, t, 2, N_KV, HEAD_DIM)

    xp = x_prompt.reshape(bp * tp, D_MODEL)
    q, cmp_p, sel_p, win_p, gates, cb, u, cmp_t, sel_t, win_t = _inproj(xp, w, tp)
    seq = lambda a: a.reshape(bp, tp, a.shape[-1])
    kv6_t = lambda a: jnp.transpose(a.reshape(bp, 2, N_KV, HEAD_DIM, a.shape[-1]), (0, 4, 1, 2, 3))[None]
    kc = _compress_prompt(seq(cmp_p), w)
    o_att = _attn_prompt(seq(q), seq(sel_p), seq(win_p), seq(gates), kc, cbias, w)
    y_p = _tail(xp, o_att.reshape(bp * tp, ATT_WIDTH), cb, u, None, p_prompt[0].reshape(bp * tp, PLE_DIM), w, tp)

    xs = x_sample.reshape(db, D_MODEL)
    q_s, cmp_s, sel_s, win_s, gates_s, cb_s, u_s = _inproj(xs, w)
    qh = q_s.reshape(db, N_KV, GROUP, HEAD_DIM)
    zq = jnp.zeros_like(qh[:, 0])
    q8 = jnp.concatenate([jnp.concatenate([qh[:, 0], zq], axis=-1), jnp.concatenate([zq, qh[:, 1]], axis=-1)], axis=1)
    n_phys = cache_cmp_kv.shape[1]
    token_minor = lambda a: jnp.transpose(a[0], (0, 2, 3, 4, 1))
    o_c, picks = _sample_cmp(page_table, q8, token_minor(cache_cmp_kv).reshape(n_phys, 2, KV_WIDTH, PAGE_SIZE), sbias, w)
    picks = picks[:, :N_KV, :N_PICK].reshape(db, N_KV * N_PICK)
    gates8 = jnp.pad(jnp.transpose(gates_s[:, :3 * N_HEADS].reshape(db, 3, N_HEADS), (0, 2, 1)),
                     ((0, 0), (0, 0), (0, LANES - 3)))
    o8, win_new = _sample_attn(page_table, picks, q8,
                               token_minor(cache_sel_kv).reshape(n_phys, 2 * N_KV, HEAD_DIM, PAGE_SIZE),
                               token_minor(state_win_kv).reshape(db, 2 * KV_WIDTH, wlen),
                               sel_s[:, None, :], win_s[:, None, :], gates8, o_c, stabs)
    o_att_s = o8.reshape(db, ATT_WIDTH)
    win_new = jnp.transpose(win_new.reshape(db, 2, N_KV, HEAD_DIM, wlen), (0, 4, 1, 2, 3))
    u_prev = jnp.transpose(state_conv[0], (1, 0, 2))
    y_s = _tail(xs, o_att_s, cb_s, u_s, u_prev, p_sample[0].reshape(db, PLE_DIM), w, None)

    return (y_p.reshape(bp, tp, D_MODEL), y_s.reshape(db, 1, D_MODEL),
            kv6_t(cmp_t), kv6_t(sel_t), kv6_t(win_t),
            seq(u)[:, tp - 2:][None],
            kv6(cmp_s, db, 1), kv6(sel_s, db, 1), win_new[None],
            jnp.concatenate([state_conv[0][:, 1:], u_s[:, None, :]], axis=1)[None])
```

```python
import functools
import math

import jax
import jax.numpy as jnp
import numpy as np
from jax import lax
from jax.experimental import pallas as pl
from jax.experimental.pallas import tpu as pltpu

D_MODEL = 1024
HEAD_DIM = 64
N_HEADS = 8
N_KV = 2
GROUP = N_HEADS // N_KV
ATT_WIDTH = N_HEADS * HEAD_DIM
KV_WIDTH = N_KV * HEAD_DIM
CONV_DIM = D_MODEL - ATT_WIDTH
PAGE_SIZE = 128
CMP_BLOCK = 32
SEL_BLOCK = 64
TOP_N = 8
WINDOW = 512
RP_BUCKETS = 32
RP_MAX_DIST = 128
D_FF = 4 * D_MODEL
PLE_DIM = 256
SCALE = HEAD_DIM ** -0.5
NEG = -1e30
EPS = 1e-6

LANES = 128
QBLK = 128
VMEM_LIMIT = 56 * 1024 * 1024

_BF = jnp.bfloat16
_F32 = jnp.float32


def _dot(a, b):
    return jnp.dot(a, b, preferred_element_type=_F32)


def _dot_nt(a, b):
    return lax.dot_general(a, b, (((1,), (1,)), ((), ())), preferred_element_type=_F32)


def _rms_rows(x, gain):
    return x * lax.rsqrt(jnp.mean(x * x, axis=-1, keepdims=True) + EPS) * gain


def _group_rms(z, gmat, gain):
    ssq = _dot((z * z).astype(_BF), gmat) * (1.0 / HEAD_DIM)
    return z * lax.rsqrt(ssq + EPS) * gain


def _inproj_body(x_ref, nm_ref, wq_ref, wkv_ref, wg_ref, wc_ref, qg_ref, kg_ref, g512_ref, g128_ref,
                 q_ref, cmp_ref, sel_ref, win_ref, gate_ref, cb_ref, u_ref, *kv_t_refs):
    a = _rms_rows(x_ref[...], nm_ref[...]).astype(_BF)
    zq = _dot(a, wq_ref[...])
    q_ref[...] = (_group_rms(zq, g512_ref[...], qg_ref[...]) * SCALE).astype(_BF)
    zkv = _dot(a, wkv_ref[...])
    g128 = g128_ref[...]
    halves = ((zkv[:, 0:128], zkv[:, 128:256]),
              (_group_rms(zkv[:, 256:384], g128, kg_ref[0:1, :]), zkv[:, 384:512]),
              (_group_rms(zkv[:, 512:640], g128, kg_ref[1:2, :]), zkv[:, 640:768]))
    for i, (ref, (k, v)) in enumerate(zip((cmp_ref, sel_ref, win_ref), halves)):
        ref[:, 0:128] = k
        ref[:, 128:256] = v
        if kv_t_refs:
            kv_t_refs[i][0, 0:128, :] = k.T
            kv_t_refs[i][0, 128:256, :] = v.T
    gate_ref[...] = jax.nn.sigmoid(_dot(a, wg_ref[...]))
    zc = _dot(a, wc_ref[...])
    cb_ref[...] = zc[:, 0:512]
    u_ref[...] = zc[:, 512:1024] * zc[:, 1024:1536]


def _inproj(x, w, seq_len=None):
    n = x.shape[0]
    tm = min(256, n)
    row = lambda c: pl.BlockSpec((tm, c), lambda i: (i, 0))
    full = lambda a: pl.BlockSpec(a.shape, lambda i: (0,) * a.ndim)
    consts = (w['norm_mix'], w['wq'], w['wkv'], w['wg'], w['wc'], w['q_gain'], w['k_gain'], w['g512'], w['g128'])
    out_specs = [row(512), row(256), row(256), row(256), row(128), row(512), row(512)]
    out_shape = ([jax.ShapeDtypeStruct((n, 512), _BF)] + [jax.ShapeDtypeStruct((n, 256), _F32)] * 3
                 + [jax.ShapeDtypeStruct((n, 128), _F32)] + [jax.ShapeDtypeStruct((n, 512), _F32)] * 2)
    if seq_len is not None:
        nt = seq_len // tm
        lead = nt - WINDOW // tm
        kv_t = pl.BlockSpec((1, 256, tm), lambda i: (i // nt, 0, i % nt))
        win_t = pl.BlockSpec((1, 256, tm), lambda i: (i // nt, 0, jnp.maximum(i % nt - lead, 0)))
        out_specs += [kv_t, kv_t, win_t]
        out_shape += [jax.ShapeDtypeStruct((n // seq_len, 256, seq_len), _F32)] * 2 + [
            jax.ShapeDtypeStruct((n // seq_len, 256, WINDOW), _F32)]
    return pl.pallas_call(
        _inproj_body,
        grid=(n // tm,),
        in_specs=[row(D_MODEL)] + [full(c) for c in consts],
        out_specs=out_specs,
        out_shape=out_shape,
        compiler_params=pltpu.CompilerParams(dimension_semantics=("arbitrary",), vmem_limit_bytes=VMEM_LIMIT),
        name="inproj",
    )(x, *consts)


FF_CHUNK = 1024


def _tail_body(halo, h_ref, o_ref, cb_ref, u_ref, up_ref, p_ref, cw_ref, on_ref, nmlp_ref, nple_ref,
               wout_hbm, wup_hbm, wdn_hbm, wgate_hbm, wproj_hbm, y_ref,
               uext_ref, wout_ref, wup_ref, wdn_ref, wgate_ref, wproj_ref, wsem):
    @pl.when(pl.program_id(0) == 0)
    def _load_weights():
        copies = [pltpu.make_async_copy(src, dst, wsem.at[i]) for i, (src, dst) in enumerate(
            ((wout_hbm, wout_ref), (wup_hbm, wup_ref), (wdn_hbm, wdn_ref), (wgate_hbm, wgate_ref),
             (wproj_hbm, wproj_ref)))]
        for c in copies:
            c.start()
        for c in copies:
            c.wait()

    tm = h_ref.shape[0]
    u = u_ref[...]
    if halo:
        first = (pl.program_id(0) % halo) == 0
        prev = jnp.where(first, 0.0, up_ref[...])
        uext_ref[0:8, :] = prev
        uext_ref[8:tm + 8, :] = u
        u2 = uext_ref[6:tm + 6, :]
        u1 = uext_ref[7:tm + 7, :]
    else:
        u2 = up_ref[0]
        u1 = up_ref[1]
    yc = cw_ref[0:1, :] * u2 + cw_ref[1:2, :] * u1 + cw_ref[2:3, :] * u
    mix_a = _rms_rows(o_ref[...], on_ref[:, 0:ATT_WIDTH]).astype(_BF)
    mix_c = _rms_rows(cb_ref[...] * yc, on_ref[:, ATT_WIDTH:]).astype(_BF)
    h = h_ref[...] + _dot(mix_a, wout_ref[0:ATT_WIDTH, :]) + _dot(mix_c, wout_ref[ATT_WIDTH:, :])
    a = _rms_rows(h, nmlp_ref[...]).astype(_BF)
    y_ref[...] = h
    for c in range(D_FF // FF_CHUNK):
        t = jnp.maximum(_dot(a, wup_ref[:, c * FF_CHUNK:(c + 1) * FF_CHUNK]), 0.0)
        y_ref[...] += _dot((t * t).astype(_BF), wdn_ref[c * FF_CHUNK:(c + 1) * FF_CHUNK, :])
    h = y_ref[...]
    a = _rms_rows(h, nple_ref[...]).astype(_BF)
    gate = jax.nn.sigmoid(_dot(a, wgate_ref[...]))
    y_ref[...] = h + gate * _dot(p_ref[...].astype(_BF), wproj_ref[...])


def _tail(h, o_att, cb, u, u_prev, p, w, seq_len):
    n = h.shape[0]
    tm = min(256, n)
    row = lambda c: pl.BlockSpec((tm, c), lambda i: (i, 0))
    const = lambda a: pl.BlockSpec(a.shape, lambda i: (0,) * a.ndim)
    if seq_len is not None:
        halo = seq_len // tm
        up_spec = pl.BlockSpec((8, CONV_DIM), lambda i: (jnp.maximum(i * (tm // 8) - 1, 0), 0))
        up = u
    else:
        halo = 0
        up_spec = pl.BlockSpec((2, tm, CONV_DIM), lambda i: (0, i, 0))
        up = u_prev
    consts = (w['conv_w'], w['out_norm'], w['norm_mlp'], w['norm_ple'])
    mats = (w['w_out'], w['w_up'], w['w_down'], w['w_ple_gate'], w['w_ple_proj'])
    return pl.pallas_call(
        functools.partial(_tail_body, halo),
        grid=(n // tm,),
        in_specs=[row(D_MODEL), row(ATT_WIDTH), row(CONV_DIM), row(CONV_DIM), up_spec, row(PLE_DIM)]
        + [const(c) for c in consts] + [pl.BlockSpec(memory_space=pl.ANY)] * len(mats),
        out_specs=row(D_MODEL),
        out_shape=jax.ShapeDtypeStruct((n, D_MODEL), _F32),
        scratch_shapes=[pltpu.VMEM((tm + 8, CONV_DIM), _F32)] + [pltpu.VMEM(m.shape, _BF) for m in mats]
        + [pltpu.SemaphoreType.DMA((len(mats),))],
        compiler_params=pltpu.CompilerParams(dimension_semantics=("arbitrary",), vmem_limit_bytes=VMEM_LIMIT),
        name="tail",
    )(h, o_att, cb, u, up, p, *consts, *mats)


def _split_heads(x, lane_lo):
    xr = pltpu.roll(x, HEAD_DIM, axis=1)
    zero = jnp.zeros_like(x)
    a = (jnp.where(lane_lo, x, zero), jnp.where(lane_lo, xr, zero))
    b = (jnp.where(lane_lo, zero, xr), jnp.where(lane_lo, zero, x))
    return a, b


def _compress_rows(load, ns, bd_ref, pe_ref):
    acc = [jnp.zeros((2 * ns, LANES), _F32), jnp.zeros((2 * ns, LANES), _F32)]
    for l in range(CMP_BLOCK):
        for plane in range(2):
            x = jnp.concatenate([load(plane, l), load(plane, CMP_BLOCK + l)], axis=0)
            x = x + pe_ref[l, plane:plane + 1, :]
            acc[plane] = acc[plane] + _dot(x.astype(_BF), bd_ref[l, plane])
    return acc


def _compress_prompt_body(slab_ref, bd_ref, pe_ref, kg_ref, g128_ref, kA_ref, kB_ref, vA_ref, vB_ref):
    ns = slab_ref.shape[1] // (2 * SEL_BLOCK)
    k, v = _compress_rows(lambda plane, t0: slab_ref[0, pl.ds(2 * t0 + plane, ns, stride=2 * SEL_BLOCK), :],
                          ns, bd_ref, pe_ref)
    k = _group_rms(k, g128_ref[...], kg_ref[...])
    lane_lo = lax.broadcasted_iota(jnp.int32, k.shape, 1) < HEAD_DIM
    ka, kb = _split_heads(k, lane_lo)
    va, vb = _split_heads(v, lane_lo)
    pad = jnp.zeros((LANES - 2 * ns, LANES), _BF)
    for g in range(N_KV):
        for ref, val in ((kA_ref, ka[g]), (kB_ref, kb[g]), (vA_ref, va[g]), (vB_ref, vb[g])):
            ref[0, g, 0:2 * ns, :] = val.astype(_BF)
            if 2 * ns < LANES:
                ref[0, g, 2 * ns:, :] = pad


def _compress_prompt(slab, w):
    b, t, _ = slab.shape
    assert t % QBLK == 0 and t // CMP_BLOCK <= LANES
    slab = slab.reshape(b, 2 * t, LANES)
    full = lambda a: pl.BlockSpec(a.shape, lambda i: (0,) * a.ndim)
    consts = (w['bd'], w['pe_t'], w['kc_gain'], w['g128'])
    out = jax.ShapeDtypeStruct((b, N_KV, LANES, LANES), _BF)
    return pl.pallas_call(
        _compress_prompt_body,
        grid=(b,),
        in_specs=[pl.BlockSpec((1, 2 * t, LANES), lambda i: (i, 0, 0))] + [full(c) for c in consts],
        out_specs=[pl.BlockSpec((1, N_KV, LANES, LANES), lambda i: (i, 0, 0, 0))] * 4,
        out_shape=[out] * 4,
        compiler_params=pltpu.CompilerParams(dimension_semantics=("arbitrary",), vmem_limit_bytes=VMEM_LIMIT),
        name="compress_prompt",
    )(slab, *consts)


BUILD_ROWS = 256
QSTEP = 2


def _top_extra(impb, cand, lane_f, n_extra):
    v = jnp.where(cand, impb, -1.0)
    picked = jnp.zeros(impb.shape, _F32)
    for _ in range(n_extra):
        mx = jnp.max(v, axis=-1, keepdims=True)
        first = jnp.min(jnp.where(v == mx, lane_f, 1e9), axis=-1, keepdims=True)
        hit = lane_f == first
        picked = jnp.where(hit, 1.0, picked)
        v = jnp.where(hit, -1.0, v)
    return picked


def _attn_prompt_body(q_ref, ks_ref, kw_ref, gate_ref, kcA_ref, kcB_ref, vcA_ref, vcB_ref, cbias_ref, tiles_ref,
                      eg_ref, o_ref, ksA, ksB, vsA, vsB, kwA, kwB, vwA, vwB, s_scr, m_scr, acc_scr):
    step = pl.program_id(1)
    t_len = ks_ref.shape[1]
    n_sel = t_len // SEL_BLOCK

    @pl.when(step == 0)
    def _build():
        def chunk(c, carry):
            r0 = pl.multiple_of(c * BUILD_ROWS, BUILD_ROWS)
            rows = pl.ds(r0, BUILD_ROWS)
            lane = lax.broadcasted_iota(jnp.int32, (BUILD_ROWS, LANES), 1)
            blk = (r0 + lax.broadcasted_iota(jnp.int32, (BUILD_ROWS, LANES), 0)) // SEL_BLOCK
            lane_lo = lane < HEAD_DIM
            oh_hi = jnp.where(lane == blk + HEAD_DIM, 1.0, 0.0)
            oh_lo = jnp.where(lane == blk, 1.0, 0.0)
            one_hi = jnp.where(lane == HEAD_DIM, 1.0, 0.0)
            one_lo = jnp.where(lane == 0, 1.0, 0.0)
            for src, k_a, k_b, v_a, v_b, onehot in ((ks_ref, ksA, ksB, vsA, vsB, True),
                                                    (kw_ref, kwA, kwB, vwA, vwB, False)):
                ka, kb = _split_heads(src[0, rows, 0:128], lane_lo)
                va, vb = _split_heads(src[0, rows, 128:256], lane_lo)
                for g in range(N_KV):
                    if onehot:
                        k_a[g, rows, :] = jnp.where(lane_lo, ka[g], oh_hi).astype(_BF)
                        k_b[g, rows, :] = jnp.where(lane_lo, oh_lo, kb[g]).astype(_BF)
                    else:
                        k_a[g, rows, :] = ka[g].astype(_BF)
                        k_b[g, rows, :] = kb[g].astype(_BF)
                    v_a[g, rows, :] = jnp.where(lane_lo, va[g], one_hi).astype(_BF)
                    v_b[g, rows, :] = jnp.where(lane_lo, one_lo, vb[g]).astype(_BF)
            return carry
        lax.fori_loop(0, t_len // BUILD_ROWS, chunk, 0)

    lane2 = lax.broadcasted_iota(jnp.int32, (2 * QBLK, LANES), 1)
    lo2 = lane2 < HEAD_DIM
    lane1 = lax.broadcasted_iota(jnp.int32, (QBLK, LANES), 1)
    lane1_f = lane1.astype(_F32)
    row1 = lax.broadcasted_iota(jnp.int32, (QBLK, LANES), 0)
    halves = range(QSTEP)
    qbs = [step * QSTEP + h for h in halves]
    qb_last = qbs[-1]
    rows_of = [slice(h * QBLK, (h + 1) * QBLK) for h in halves]
    curs = [(qb * QBLK + row1) // SEL_BLOCK for qb in qbs]

    gates = gate_ref[0]
    g_hi = gates.astype(_BF)
    g_lo = (gates - g_hi.astype(_F32)).astype(_BF)
    gexp = _dot(g_hi, eg_ref[...]) + _dot(g_lo, eg_ref[...])

    def normalize(acc, x):
        l = jnp.sum(jnp.where(lane2 == (HEAD_DIM if x == 0 else 0), acc, 0.0), axis=-1, keepdims=True)
        keep = lo2 if x == 0 else jnp.logical_not(lo2)
        return jnp.where(keep, acc / l, 0.0)

    chains = [(h, g, x) for h in halves for g in range(N_KV) for x in range(2)]
    o_cmp, q_plain, q_pair, importance = [], [], [], []
    for h in halves:
        for g in range(N_KV):
            qs = jnp.concatenate([q_ref[0, rows_of[h], (2 * g) * LANES:(2 * g + 1) * LANES],
                                  q_ref[0, rows_of[h], (2 * g + 1) * LANES:(2 * g + 2) * LANES]], axis=0)
            zero = jnp.zeros_like(qs)
            q_a = jnp.where(lo2, qs, zero)
            q_b = jnp.where(lo2, zero, qs)
            q_pair.append(qs)

            def cmp_probs(qx, k_ref, x):
                s = _dot_nt(qx, k_ref[0, g]) + cbias_ref[h, g, x]
                m = jnp.max(s, axis=-1, keepdims=True)
                e = jnp.where(s > 0.5 * NEG, jnp.exp(s - m), 0.0)
                l = jnp.sum(e, axis=-1, keepdims=True)
                return e / jnp.where(l > 0.0, l, 1.0)
            p_a = cmp_probs(q_a, kcA_ref, 0)
            p_b = cmp_probs(q_b, kcB_ref, 1)
            o_cmp.append(_dot(p_a.astype(_BF), vcA_ref[0, g]) + _dot(p_b.astype(_BF), vcB_ref[0, g]))
            imp = p_a[0:QBLK] + p_a[QBLK:] + p_b[0:QBLK] + p_b[QBLK:]
            importance.append(imp + pltpu.roll(imp, LANES - n_sel, axis=1))
            q_plain += [q_a, q_b]

    two = lambda a: jnp.concatenate([a, a], axis=0)
    cands = jnp.concatenate([two((lane1 >= 1) & (lane1 <= curs[h] - 2)) for h in halves], axis=0)
    picked = lax.cond(qb_last * QBLK >= TOP_N * SEL_BLOCK,
                      lambda: _top_extra(jnp.concatenate(importance, axis=0), cands,
                                         jnp.concatenate([lane1_f] * (2 * QSTEP), axis=0), TOP_N - 3),
                      lambda: jnp.zeros((2 * QSTEP * QBLK, LANES), _F32))
    q_aug = []
    for h in halves:
        cur = curs[h]
        forced = (lane1 == 0) | (lane1 == cur) | (lane1 == cur - 1)
        few = cur <= TOP_N - 1
        for g in range(N_KV):
            i = h * N_KV + g
            chosen = forced | (few & (lane1 <= cur)) | ((picked[i * QBLK:(i + 1) * QBLK] > 0.5) & jnp.logical_not(few))
            sb_lo = jnp.where(chosen | (lane1 >= n_sel), 0.0, NEG)
            sb_hi = pltpu.roll(sb_lo, HEAD_DIM, axis=1)
            q_aug += [jnp.where(lo2, q_pair[i], two(sb_hi).astype(_BF)),
                      jnp.where(lo2, two(sb_lo).astype(_BF), q_pair[i])]

    k_sel, v_sel, k_win, v_win = (ksA, ksB), (vsA, vsB), (kwA, kwB), (vwA, vwB)
    n_win = WINDOW // QBLK + 1

    o_win = []
    for ch, (h, g, x) in enumerate(chains):
        w_first = jnp.maximum(qbs[h] - (n_win - 1), 0)
        w_rows = pl.ds(pl.multiple_of(w_first * QBLK, QBLK), n_win * QBLK)
        s = _dot_nt(q_plain[ch], k_win[x][g, w_rows, :])
        parts = []
        for i in range(n_win):
            idx = qbs[h] - w_first - i
            parts.append(s[:, i * QBLK:(i + 1) * QBLK] + tiles_ref[jnp.where(idx < 0, n_win, idx), g, x])
        m = parts[0]
        for part in parts[1:]:
            m = jnp.maximum(m, part)
        m = jnp.max(m, axis=-1, keepdims=True)
        p = jnp.concatenate([jnp.exp(part - m).astype(_BF) for part in parts], axis=1)
        o_win.append(normalize(_dot(p, v_win[x][g, w_rows, :]), x))

    n_quart = 4
    qw = t_len // n_quart
    per_q = qw // QBLK
    m_scr[...] = jnp.full(m_scr.shape, NEG, _F32)
    for qi in range(n_quart):
        @pl.when(qb_last * QBLK >= qi * qw)
        def _scores(qi=qi):
            for ch, (h, g, x) in enumerate(chains):
                s = _dot_nt(q_aug[ch], k_sel[x][g, qi * qw:(qi + 1) * qw, :])
                m = m_scr[ch]
                for ci in range(per_q):
                    c = qi * per_q + ci
                    idx = qbs[h] - c
                    sb = s[:, ci * QBLK:(ci + 1) * QBLK] + tiles_ref[jnp.where(idx < 0, n_win, jnp.minimum(idx, 2)), g, x]
                    s_scr[ch, :, c * QBLK:(c + 1) * QBLK] = sb
                    m = jnp.maximum(m, sb)
                m_scr[ch] = m
    m_sel = [jnp.max(m_scr[ch], axis=-1, keepdims=True) for ch in range(len(chains))]

    acc_scr[...] = jnp.zeros(acc_scr.shape, _F32)
    for qi in range(n_quart):
        @pl.when(qb_last * QBLK >= qi * qw)
        def _weighted(qi=qi):
            for ch, (h, g, x) in enumerate(chains):
                p = jnp.exp(s_scr[ch, :, qi * qw:(qi + 1) * qw] - m_sel[ch]).astype(_BF)
                acc_scr[ch] += _dot(p, v_sel[x][g, qi * qw:(qi + 1) * qw, :])

    for h in halves:
        for g in range(N_KV):
            i = h * N_KV + g
            o_c = o_cmp[i]
            o_s = normalize(acc_scr[2 * i], 0) + normalize(acc_scr[2 * i + 1], 1)
            o_w = o_win[2 * i] + o_win[2 * i + 1]
            for pr in range(2):
                rows = slice(pr * QBLK, (pr + 1) * QBLK)
                col = (2 * g + pr) * LANES
                gx = lambda br: gexp[rows_of[h], br * ATT_WIDTH + col:br * ATT_WIDTH + col + LANES]
                o_ref[0, rows_of[h], col:col + LANES] = gx(0) * o_c[rows] + gx(1) * o_s[rows] + gx(2) * o_w[rows]


def _attn_prompt(q, sel, win, gates, kc, cbias, w):
    b, t, _ = sel.shape
    nq = t // QBLK
    assert t // SEL_BLOCK <= CMP_BLOCK and t % (4 * QBLK) == 0 and t >= WINDOW + QBLK and nq % QSTEP == 0
    kcA, kcB, vcA, vcB = kc
    full = lambda a: pl.BlockSpec(a.shape, lambda i, j: (0,) * a.ndim)
    slab = pl.BlockSpec((1, t, 256), lambda i, j: (i, 0, 0))
    kcs = pl.BlockSpec((1, N_KV, LANES, LANES), lambda i, j: (i, 0, 0, 0))
    n_chain = QSTEP * 2 * N_KV
    q_rows = QSTEP * QBLK
    scratch = [pltpu.VMEM((N_KV, t, LANES), _BF)] * 8 + [
        pltpu.VMEM((n_chain, 2 * QBLK, t), _F32),
        pltpu.VMEM((n_chain, 2 * QBLK, LANES), _F32), pltpu.VMEM((n_chain, 2 * QBLK, LANES), _F32)]
    return pl.pallas_call(
        _attn_prompt_body,
        grid=(b, nq // QSTEP),
        in_specs=[pl.BlockSpec((1, q_rows, ATT_WIDTH), lambda i, j: (i, j, 0)), slab, slab,
                  pl.BlockSpec((1, q_rows, LANES), lambda i, j: (i, j, 0)), kcs, kcs, kcs, kcs,
                  pl.BlockSpec((QSTEP, N_KV, 2, 2 * QBLK, LANES), lambda i, j: (j, 0, 0, 0, 0)),
                  full(w['tiles']), full(w['egate'])],
        out_specs=pl.BlockSpec((1, q_rows, ATT_WIDTH), lambda i, j: (i, j, 0)),
        out_shape=jax.ShapeDtypeStruct((b, t, ATT_WIDTH), _F32),
        scratch_shapes=scratch,
        compiler_params=pltpu.CompilerParams(dimension_semantics=("arbitrary", "arbitrary"),
                                             vmem_limit_bytes=VMEM_LIMIT),
        name="attn_prompt",
    )(q, sel, win, gates, kcA, kcB, vcA, vcB, cbias, w['tiles'], w['egate'])


N_PICK = TOP_N - 3
N_SLOT = TOP_N - 1
PAGE_ROWS = 2 * KV_WIDTH


PAIR_TOKENS = 2 * PAGE_SIZE


def _sample_cmp_body(pt_ref, q_ref, cache_ref, bd_ref, pet_ref, perm_ref, kg_ref, g128_ref, sb_ref, oc_ref, pick_ref,
                     buf, rows_scr, sem):
    b = pl.program_id(0)
    nb = pl.num_programs(0)
    n_pairs = pt_ref.shape[1] // 2
    nc = 8 * n_pairs
    slot = b % 2

    def page_copy(bb, pair, half, sl):
        return pltpu.make_async_copy(cache_ref.at[pt_ref[bb, 2 * pair + half]],
                                     buf.at[sl, pair, :, half * PAGE_SIZE:(half + 1) * PAGE_SIZE], sem.at[sl])

    def for_pages(fn):
        def step(pair, c):
            fn(pair, 0)
            fn(pair, 1)
            return c
        lax.fori_loop(0, n_pairs, step, 0)

    @pl.when(b == 0)
    def _prime():
        for_pages(lambda pair, half: page_copy(0, pair, half, 0).start())

    @pl.when(b + 1 < nb)
    def _prefetch():
        for_pages(lambda pair, half: page_copy(b + 1, pair, half, 1 - slot).start())

    for_pages(lambda pair, half: page_copy(b, pair, half, slot).wait())

    def regroup(pair, c):
        x = (buf[slot, pair] + pet_ref[...]).astype(_BF)
        rows_scr[pair] = _dot_nt(perm_ref[...], x)
        return c
    lax.fori_loop(0, n_pairs, regroup, 0, unroll=4)

    acc = jnp.zeros((nc, 2 * KV_WIDTH), _F32)
    for l in range(CMP_BLOCK):
        x = rows_scr[:, 8 * l:8 * (l + 1), :].reshape(nc, 2 * KV_WIDTH)
        acc = acc + _dot(x.astype(_BF), bd_ref[l])
    k = _group_rms(acc[:, 0:KV_WIDTH], g128_ref[...], kg_ref[...]).astype(_BF)
    v = acc[:, KV_WIDTH:].astype(_BF)
    s = _dot_nt(q_ref[0], k) + sb_ref[...]
    m = jnp.max(s, axis=-1, keepdims=True)
    e = jnp.exp(s - m)
    p = e / jnp.sum(e, axis=-1, keepdims=True)
    oc_ref[0] = _dot(p.astype(_BF), v)
    imp8 = p + pltpu.roll(p, nc - 1, axis=1)
    rows = [imp8[h:h + 1] for h in range(N_HEADS)]
    imp_g = [rows[g * GROUP] + rows[g * GROUP + 1] + rows[g * GROUP + 2] + rows[g * GROUP + 3] for g in range(N_KV)]
    imp = jnp.concatenate(imp_g + [jnp.full((8 - N_KV, nc), -1.0, _F32)], axis=0)
    lane = lax.broadcasted_iota(jnp.int32, (8, nc), 1)
    lane_f = lane.astype(_F32)
    v_c = jnp.where((lane % 2 == 0) & (lane >= 2) & (lane <= nc - 4), imp, -1.0)
    picks = jnp.zeros((8, nc), _F32)
    for i in range(N_PICK):
        mx = jnp.max(v_c, axis=-1, keepdims=True)
        first = jnp.min(jnp.where(v_c == mx, lane_f, 1e9), axis=-1, keepdims=True)
        picks = jnp.where(lane == i, first * 0.5, picks)
        v_c = jnp.where(lane_f == first, -1.0, v_c)
    pick_ref[0] = picks.astype(jnp.int32)


def _sample_cmp(page_table, q8, cache, sbias, w):
    db, n_pages = page_table.shape
    nc = 4 * n_pages
    assert n_pages % 2 == 0 and nc // 2 - 2 >= N_PICK
    full = lambda a: pl.BlockSpec(a.shape, lambda i, pt: (0,) * a.ndim)
    consts = (w['bd_kv'], w['pe_tok'], w['perm'], w['kc_gain'], w['g128'], sbias)
    grid_spec = pltpu.PrefetchScalarGridSpec(
        num_scalar_prefetch=1,
        grid=(db,),
        in_specs=[pl.BlockSpec((1, 8, LANES), lambda i, pt: (i, 0, 0)), pl.BlockSpec(memory_space=pl.ANY)]
        + [full(c) for c in consts],
        out_specs=[pl.BlockSpec((1, 8, LANES), lambda i, pt: (i, 0, 0)),
                   pl.BlockSpec((1, 8, nc), lambda i, pt: (i, 0, 0))],
        scratch_shapes=[pltpu.VMEM((2, n_pages // 2, 2 * KV_WIDTH, PAIR_TOKENS), _F32),
                        pltpu.VMEM((n_pages // 2, PAIR_TOKENS, 2 * KV_WIDTH), _F32),
                        pltpu.SemaphoreType.DMA((2,))],
    )
    return pl.pallas_call(
        _sample_cmp_body,
        grid_spec=grid_spec,
        out_shape=[jax.ShapeDtypeStruct((db, 8, LANES), _F32), jax.ShapeDtypeStruct((db, 8, nc), jnp.int32)],
        compiler_params=pltpu.CompilerParams(dimension_semantics=("arbitrary",), vmem_limit_bytes=VMEM_LIMIT),
        name="sample_cmp",
    )(page_table, q8, cache, *consts)


def _sample_attn_body(pt_ref, pick_ref, q_ref, cache_ref, win_ref, nsel_ref, nwin_ref, ncol_ref, gate_ref, oc_ref,
                      bsel_ref, b0_ref, bwin_ref, o_ref, wout_ref, kb, sem):
    b = pl.program_id(0)
    nb = pl.num_programs(0)
    nj = 2 * pt_ref.shape[1]
    wlen = win_ref.shape[2]
    slot = b % 2

    def block_of(bb, g, s):
        if s == 0:
            return 0
        if s == N_SLOT - 1:
            return nj - 1
        return pick_ref[bb, g * N_PICK + (s - 1)]

    def tile_copy(bb, g, s, plane, sl):
        page = pt_ref[bb, block_of(bb, g, s) // 2]
        return pltpu.make_async_copy(cache_ref.at[page, plane * N_KV + g],
                                     kb.at[sl, g, plane, :, pl.ds(s * PAGE_SIZE, PAGE_SIZE)], sem.at[sl])

    def for_tiles(fn):
        for g in range(N_KV):
            for s in range(N_SLOT):
                for plane in range(2):
                    fn(g, s, plane)

    @pl.when(b == 0)
    def _prime():
        for_tiles(lambda g, s, plane: tile_copy(0, g, s, plane, 0).start())

    @pl.when(b + 1 < nb)
    def _prefetch():
        for_tiles(lambda g, s, plane: tile_copy(b + 1, g, s, plane, 1 - slot).start())

    q8 = q_ref[0]
    q8f = q8.astype(_F32)
    head_g = lax.broadcasted_iota(jnp.int32, (8, HEAD_DIM), 0) // GROUP
    own_half = lambda x: jnp.where(head_g == 0, x[:, 0:HEAD_DIM], x[:, HEAD_DIM:])
    b0 = b0_ref[:, 0:1]

    x = win_ref[0]
    lane_w = lax.broadcasted_iota(jnp.int32, x.shape, 1)
    wout_ref[0] = jnp.where(lane_w == wlen - 1, ncol_ref[0], pltpu.roll(x, wlen - 1, axis=1))
    sw = _dot(q8, x[0:KV_WIDTH, :].astype(_BF)) + bwin_ref[...]
    sw_new = jnp.sum(q8f * nwin_ref[0][:, 0:KV_WIDTH], axis=-1, keepdims=True) + b0
    mw = jnp.maximum(jnp.max(sw, axis=-1, keepdims=True), sw_new)
    ew = jnp.exp(sw - mw)
    ew_new = jnp.exp(sw_new - mw)
    o_w = _dot_nt(ew.astype(_BF), x[KV_WIDTH:, :].astype(_BF)) + ew_new * nwin_ref[0][:, KV_WIDTH:]
    o_w = own_half(o_w / (jnp.sum(ew, axis=-1, keepdims=True) + ew_new))

    for_tiles(lambda g, s, plane: tile_copy(b, g, s, plane, slot).wait())

    ss_new = jnp.sum(q8f * nsel_ref[0][:, 0:KV_WIDTH], axis=-1, keepdims=True) + b0
    lane_half = lax.broadcasted_iota(jnp.int32, (8, PAGE_SIZE), 1) // SEL_BLOCK
    o_sel = []
    for g in range(N_KV):
        pieces = []
        for s in range(N_SLOT):
            j = jnp.full((8, PAGE_SIZE), block_of(b, g, s), jnp.int32)
            if s == 0:
                tab = bsel_ref[0]
            elif s == N_SLOT - 1:
                tab = bsel_ref[2]
            else:
                tab = jnp.where(j == nj - 2, bsel_ref[1], bsel_ref[0])
            pieces.append(jnp.where(lane_half == j % 2, tab, NEG))
        ss = _dot(q8[:, g * HEAD_DIM:(g + 1) * HEAD_DIM], kb[slot, g, 0].astype(_BF)) + jnp.concatenate(pieces, axis=1)
        ms = jnp.maximum(jnp.max(ss, axis=-1, keepdims=True), ss_new)
        es = jnp.exp(ss - ms)
        es_new = jnp.exp(ss_new - ms)
        v_new = nsel_ref[0][:, KV_WIDTH + g * HEAD_DIM:KV_WIDTH + (g + 1) * HEAD_DIM]
        o_g = _dot_nt(es.astype(_BF), kb[slot, g, 1].astype(_BF)) + es_new * v_new
        o_sel.append(o_g / (jnp.sum(es, axis=-1, keepdims=True) + es_new))
    o_s = jnp.where(head_g == 0, o_sel[0], o_sel[1])
    gates = gate_ref[0]
    o_ref[0] = gates[:, 0:1] * own_half(oc_ref[0]) + gates[:, 1:2] * o_s + gates[:, 2:3] * o_w


def _sample_attn(page_table, picks, q8, cache, win, new_sel, new_win, gates8, o_c, tabs):
    db, n_pages = page_table.shape
    wlen = win.shape[2]
    full = lambda a: pl.BlockSpec(a.shape, lambda i, pt, pk: (0,) * a.ndim)
    per_b = lambda r, c: pl.BlockSpec((1, r, c), lambda i, pt, pk: (i, 0, 0))
    bsel, b0, bwin = tabs
    grid_spec = pltpu.PrefetchScalarGridSpec(
        num_scalar_prefetch=2,
        grid=(db,),
        in_specs=[per_b(8, LANES), pl.BlockSpec(memory_space=pl.ANY), per_b(2 * KV_WIDTH, wlen), per_b(1, 256),
                  per_b(1, 256), per_b(2 * KV_WIDTH, 1), per_b(8, LANES), per_b(8, LANES),
                  full(bsel), full(b0), full(bwin)],
        out_specs=[per_b(8, HEAD_DIM), per_b(2 * KV_WIDTH, wlen)],
        scratch_shapes=[pltpu.VMEM((2, N_KV, 2, HEAD_DIM, N_SLOT * PAGE_SIZE), _F32), pltpu.SemaphoreType.DMA((2,))],
    )
    return pl.pallas_call(
        _sample_attn_body,
        grid_spec=grid_spec,
        out_shape=[jax.ShapeDtypeStruct((db, 8, HEAD_DIM), _F32), jax.ShapeDtypeStruct((db, 2 * KV_WIDTH, wlen), _F32)],
        compiler_params=pltpu.CompilerParams(dimension_semantics=("arbitrary",), vmem_limit_bytes=VMEM_LIMIT),
        name="sample_attn",
    )(page_table, picks, q8, cache, win, new_sel, new_win, new_win[:, 0, :, None], gates8, o_c, bsel, b0, bwin)


def _rel_bucket(dist):
    n = np.maximum(np.asarray(dist), 0)
    max_exact = RP_BUCKETS // 2
    nf = np.maximum(n, 1).astype(np.float32)
    large = max_exact + (np.log(nf / np.float32(max_exact)) / np.float32(math.log(RP_MAX_DIST / max_exact))
                         * np.float32(RP_BUCKETS - max_exact)).astype(np.int32)
    large = np.minimum(large, RP_BUCKETS - 1)
    return np.where(n < max_exact, n, large)


def _bias_of(rel_bias, dist):
    onehot = _rel_bucket(dist)[..., None, None] == np.arange(RP_BUCKETS)[:, None]
    return jnp.sum(jnp.where(onehot, rel_bias, 0.0), axis=-2)


def _head_major(x, lead):
    n = x.shape[-2]
    x = x.reshape(lead + (QBLK, n, N_KV, 2, 2))
    nl = len(lead)
    x = jnp.transpose(x, tuple(range(nl)) + (nl + 2, nl + 4, nl + 3, nl, nl + 1))
    return x.reshape(lead + (N_KV, 2, 2 * QBLK, n))


def _prompt_tables(rel_bias, t_len):
    nq = t_len // QBLK
    ns = t_len // SEL_BLOCK
    ti = np.arange(QBLK)[:, None]
    ki = np.arange(QBLK)[None, :]
    n_idx = WINDOW // QBLK + 1
    pad = QBLK - 1
    bvec = _bias_of(rel_bias, np.arange(-pad, t_len + pad))
    wins = jnp.stack([bvec[i * QBLK:i * QBLK + 2 * QBLK - 1][::-1] for i in range(n_idx)])
    skew = jnp.tile(wins, (1, QBLK + 1, 1))[:, :QBLK * 2 * QBLK].reshape(n_idx, QBLK, 2 * QBLK, N_HEADS)
    tiles = skew[:, ::-1, :QBLK]
    dist = np.stack([i * QBLK + ti - ki for i in range(n_idx)])
    valid = dist >= 0
    valid[n_idx - 1] &= dist[n_idx - 1] < WINDOW
    tiles = jnp.where(valid[..., None], tiles, NEG)
    tiles = jnp.concatenate([tiles, jnp.full_like(tiles[:1], NEG)])
    tiles = _head_major(tiles, (n_idx + 1,))
    shift = CMP_BLOCK * ns * 2
    bcmp = _bias_of(rel_bias, np.arange(-shift, t_len))
    cols = [bcmp[shift - (CMP_BLOCK * blk + CMP_BLOCK - 1):][:t_len]
            for blk in list(range(0, 2 * ns, 2)) + list(range(1, 2 * ns, 2))]
    cb = jnp.stack(cols, axis=1)
    cb = jnp.pad(cb, ((0, 0), (0, LANES - 2 * ns), (0, 0)))
    lane = np.arange(LANES)
    blk = np.where(lane < ns, 2 * lane, 2 * (lane - ns) + 1)
    dist_c = np.arange(t_len)[:, None] - (CMP_BLOCK * blk + CMP_BLOCK - 1)[None, :]
    valid_c = (dist_c >= 0) & (lane < 2 * ns)[None, :]
    cb = jnp.where(valid_c[..., None], cb, NEG)
    cb = _head_major(cb.reshape(nq, QBLK, LANES, N_HEADS), (nq,))
    return tiles, cb


def _sample_tables(rel_bias, past_len, wlen):
    blk = np.arange(past_len // CMP_BLOCK)
    sbias = _bias_of(rel_bias, past_len - (CMP_BLOCK * blk + CMP_BLOCK - 1)).T
    pos = np.arange(PAGE_SIZE) % SEL_BLOCK
    bsel = jnp.stack([_bias_of(rel_bias, np.full(PAGE_SIZE, past_len)).T,
                      _bias_of(rel_bias, 2 * SEL_BLOCK - pos).T,
                      _bias_of(rel_bias, SEL_BLOCK - pos).T])
    b0 = _bias_of(rel_bias, np.zeros((LANES,), np.int32)).T
    tok = np.arange(wlen)
    bwin = jnp.where((tok >= 1)[None, :], _bias_of(rel_bias, wlen - tok).T, NEG)
    return sbias, (bsel, b0, bwin)


def _prep(norm_mix, w_in, q_norm, k_norm, cmp_pe, w_cmp, conv_w, out_norm, w_out, norm_mlp, w_up, w_down,
          norm_ple, w_ple_gate, w_ple_proj):
    w_in = w_in[0]
    o_kv = ATT_WIDTH
    o_g = o_kv + 6 * KV_WIDTH
    o_c = o_g + 3 * N_HEADS
    bd = jnp.einsum('gh,plde->lpgdhe', jnp.eye(N_KV, dtype=_F32), w_cmp[0]).reshape(CMP_BLOCK, 2, LANES, LANES)
    pe_t = jnp.transpose(jnp.tile(cmp_pe[0], (1, 1, N_KV)), (1, 0, 2))
    egate = np.zeros((LANES, 3 * ATT_WIDTH), np.float32)
    for br in range(3):
        for h in range(N_HEADS):
            egate[br * N_HEADS + h, br * ATT_WIDTH + h * HEAD_DIM:br * ATT_WIDTH + (h + 1) * HEAD_DIM] = 1.0
    ones = np.ones((HEAD_DIM, HEAD_DIM), np.float32)
    perm = np.zeros((PAIR_TOKENS, PAIR_TOKENS), np.float32)
    tok = np.arange(PAIR_TOKENS)
    perm[(tok % CMP_BLOCK) * (PAIR_TOKENS // CMP_BLOCK) + tok // CMP_BLOCK, tok] = 1.0
    pe_tok = jnp.tile(jnp.transpose(cmp_pe[0], (0, 2, 1)), (1, N_KV, PAIR_TOKENS // CMP_BLOCK))
    pe_tok = pe_tok.reshape(2 * KV_WIDTH, PAIR_TOKENS)
    zero = jnp.zeros_like(bd[:, 0])
    bd_kv = jnp.concatenate([jnp.concatenate([bd[:, 0], zero], axis=2),
                             jnp.concatenate([zero, bd[:, 1]], axis=2)], axis=1)
    return {
        'perm': jnp.asarray(perm, _BF), 'pe_tok': pe_tok, 'bd_kv': bd_kv.astype(_BF),
        'norm_mix': norm_mix[0][None], 'wq': w_in[:, :o_kv].astype(_BF), 'wkv': w_in[:, o_kv:o_g].astype(_BF),
        'wg': jnp.pad(w_in[:, o_g:o_c], ((0, 0), (0, LANES - 3 * N_HEADS))).astype(_BF),
        'wc': w_in[:, o_c:].astype(_BF),
        'q_gain': jnp.tile(q_norm[0], N_HEADS)[None],
        'k_gain': jnp.stack([jnp.tile(k_norm[0, 1], N_KV), jnp.tile(k_norm[0, 2], N_KV)]),
        'kc_gain': jnp.tile(k_norm[0, 0], N_KV)[None],
        'g512': jnp.asarray(np.kron(np.eye(N_HEADS, dtype=np.float32), ones), _BF),
        'g128': jnp.asarray(np.kron(np.eye(N_KV, dtype=np.float32), ones), _BF),
        'bd': bd.astype(_BF), 'pe_t': pe_t,
        'egate': jnp.asarray(egate, _BF),
        'conv_w': conv_w[0], 'out_norm': out_norm[0][None], 'w_out': w_out[0].astype(_BF),
        'norm_mlp': norm_mlp[0][None], 'w_up': w_up[0].astype(_BF), 'w_down': w_down[0].astype(_BF),
        'norm_ple': norm_ple[0][None], 'w_ple_gate': w_ple_gate[0].astype(_BF),
        'w_ple_proj': w_ple_proj[0].astype(_BF),
    }


def kernel(x_prompt, x_sample, p_prompt, p_sample, cache_cmp_kv, cache_sel_kv, state_win_kv, state_conv, page_table, rel_bias, norm_mix, w_in, q_norm, k_norm, cmp_pe, w_cmp, conv_w, out_norm, w_out, norm_mlp, w_up, w_down, norm_ple, w_ple_gate, w_ple_proj):
    bp, tp, _ = x_prompt.shape
    db, ts, _ = x_sample.shape
    assert norm_mix.shape[0] == 1 and ts == 1 and tp >= WINDOW
    n_pages = page_table.shape[1]
    past_len = n_pages * PAGE_SIZE
    wlen = state_win_kv.shape[2]
    assert wlen == WINDOW and past_len >= 4 * SEL_BLOCK
    w = _prep(norm_mix, w_in, q_norm, k_norm, cmp_pe, w_cmp, conv_w, out_norm, w_out, norm_mlp, w_up, w_down,
              norm_ple, w_ple_gate, w_ple_proj)
    w['tiles'], cbias = _prompt_tables(rel_bias, tp)
    sbias, stabs = _sample_tables(rel_bias, past_len, wlen)
    kv6 = lambda a, b, t: a.reshape(1, b, t, 2, N_KV, HEAD_DIM)

    xp = x_prompt.reshape(bp * tp, D_MODEL)
    q, cmp_p, sel_p, win_p, gates, cb, u, cmp_t, sel_t, win_t = _inproj(xp, w, tp)
    seq = lambda a: a.reshape(bp, tp, a.shape[-1])
    kv6_t = lambda a: jnp.transpose(a.reshape(bp, 2, N_KV, HEAD_DIM, a.shape[-1]), (0, 4, 1, 2, 3))[None]
    kc = _compress_prompt(seq(cmp_p), w)
    o_att = _attn_prompt(seq(q), seq(sel_p), seq(win_p), seq(gates), kc, cbias, w)
    y_p = _tail(xp, o_att.reshape(bp * tp, ATT_WIDTH), cb, u, None, p_prompt[0].reshape(bp * tp, PLE_DIM), w, tp)

    xs = x_sample.reshape(db, D_MODEL)
    q_s, cmp_s, sel_s, win_s, gates_s, cb_s, u_s = _inproj(xs, w)
    qh = q_s.reshape(db, N_KV, GROUP, HEAD_DIM)
    zq = jnp.zeros_like(qh[:, 0])
    q8 = jnp.concatenate([jnp.concatenate([qh[:, 0], zq], axis=-1), jnp.concatenate([zq, qh[:, 1]], axis=-1)], axis=1)
    n_phys = cache_cmp_kv.shape[1]
    token_minor = lambda a: jnp.transpose(a[0], (0, 2, 3, 4, 1))
    o_c, picks = _sample_cmp(page_table, q8, token_minor(cache_cmp_kv).reshape(n_phys, 2 * KV_WIDTH, PAGE_SIZE), sbias, w)
    picks = picks[:, :N_KV, :N_PICK].reshape(db, N_KV * N_PICK)
    gates8 = jnp.pad(jnp.transpose(gates_s[:, :3 * N_HEADS].reshape(db, 3, N_HEADS), (0, 2, 1)),
                     ((0, 0), (0, 0), (0, LANES - 3)))
    o8, win_new = _sample_attn(page_table, picks, q8,
                               token_minor(cache_sel_kv).reshape(n_phys, 2 * N_KV, HEAD_DIM, PAGE_SIZE),
                               token_minor(state_win_kv).reshape(db, 2 * KV_WIDTH, wlen),
                               sel_s[:, None, :], win_s[:, None, :], gates8, o_c, stabs)
    o_att_s = o8.reshape(db, ATT_WIDTH)
    win_new = jnp.transpose(win_new.reshape(db, 2, N_KV, HEAD_DIM, wlen), (0, 4, 1, 2, 3))
    u_prev = jnp.transpose(state_conv[0], (1, 0, 2))
    y_s = _tail(xs, o_att_s, cb_s, u_s, u_prev, p_sample[0].reshape(db, PLE_DIM), w, None)

    return (y_p.reshape(bp, tp, D_MODEL), y_s.reshape(db, 1, D_MODEL),
            kv6_t(cmp_t), kv6_t(sel_t), kv6_t(win_t),
            seq(u)[:, tp - 2:][None],
            kv6(cmp_s, db, 1), kv6(sel_s, db, 1), win_new[None],
            jnp.concatenate([state_conv[0][:, 1:], u_s[:, None, :]], axis=1)[None])
```

```python
import functools
import math

import jax
import jax.numpy as jnp
import numpy as np
from jax import lax
from jax.experimental import pallas as pl
from jax.experimental.pallas import tpu as pltpu

D_MODEL = 1024
HEAD_DIM = 64
N_HEADS = 8
N_KV = 2
GROUP = N_HEADS // N_KV
ATT_WIDTH = N_HEADS * HEAD_DIM
KV_WIDTH = N_KV * HEAD_DIM
CONV_DIM = D_MODEL - ATT_WIDTH
PAGE_SIZE = 128
CMP_BLOCK = 32
SEL_BLOCK = 64
TOP_N = 8
WINDOW = 512
RP_BUCKETS = 32
RP_MAX_DIST = 128
D_FF = 4 * D_MODEL
PLE_DIM = 256
SCALE = HEAD_DIM ** -0.5
NEG = -1e30
EPS = 1e-6
LOG2E = 1.4426950408889634

LANES = 128
QBLK = 128
ROW_TILE = 512
VMEM_LIMIT = 56 * 1024 * 1024

_BF = jnp.bfloat16
_F32 = jnp.float32


def _dot(a, b):
    return jnp.dot(a, b, preferred_element_type=_F32)


def _dot_nt(a, b):
    return lax.dot_general(a, b, (((1,), (1,)), ((), ())), preferred_element_type=_F32)


def _rms_rows(x, gain):
    return x * lax.rsqrt(jnp.mean(x * x, axis=-1, keepdims=True) + EPS) * gain


def _group_rms(z, gmat, gain):
    ssq = _dot((z * z).astype(_BF), gmat) * (1.0 / HEAD_DIM)
    return z * lax.rsqrt(ssq + EPS) * gain


def _inproj_body(x_ref, nm_ref, wq_ref, wkv_ref, wg_ref, wc_ref, qg_ref, kg_ref, g512_ref, g128_ref,
                 q_ref, cmp_ref, sel_ref, win_ref, gate_ref, cb_ref, u_ref, *kv_t_refs):
    a = _rms_rows(x_ref[...], nm_ref[...]).astype(_BF)
    zq = _dot(a, wq_ref[...])
    qn = _group_rms(zq, g512_ref[...], qg_ref[...]) * SCALE
    q_ref[:, 0:ATT_WIDTH] = qn.astype(_BF)
    q_ref[:, ATT_WIDTH:] = (qn * LOG2E).astype(_BF)
    zkv = _dot(a, wkv_ref[...])
    g128 = g128_ref[...]
    halves = ((zkv[:, 0:128], zkv[:, 128:256]),
              (_group_rms(zkv[:, 256:384], g128, kg_ref[0:1, :]), zkv[:, 384:512]),
              (_group_rms(zkv[:, 512:640], g128, kg_ref[1:2, :]), zkv[:, 640:768]))
    for i, (ref, (k, v)) in enumerate(zip((cmp_ref, sel_ref, win_ref), halves)):
        ref[:, 0:128] = k
        ref[:, 128:256] = v
        if kv_t_refs:
            kv_t_refs[i][0, 0:128, :] = k.T
            kv_t_refs[i][0, 128:256, :] = v.T
    gate_ref[...] = jax.nn.sigmoid(_dot(a, wg_ref[...]))
    zc = _dot(a, wc_ref[...])
    cb_ref[...] = zc[:, 0:512]
    u_ref[...] = zc[:, 512:1024] * zc[:, 1024:1536]


def _inproj(x, w, seq_len=None):
    n = x.shape[0]
    tm = min(ROW_TILE, n)
    row = lambda c: pl.BlockSpec((tm, c), lambda i: (i, 0))
    full = lambda a: pl.BlockSpec(a.shape, lambda i: (0,) * a.ndim)
    consts = (w['norm_mix'], w['wq'], w['wkv'], w['wg'], w['wc'], w['q_gain'], w['k_gain'], w['g512'], w['g128'])
    out_specs = [row(2 * ATT_WIDTH), row(256), row(256), row(256), row(128), row(512), row(512)]
    out_shape = ([jax.ShapeDtypeStruct((n, 2 * ATT_WIDTH), _BF)] + [jax.ShapeDtypeStruct((n, 256), _F32)] * 3
                 + [jax.ShapeDtypeStruct((n, 128), _F32)] + [jax.ShapeDtypeStruct((n, 512), _F32)] * 2)
    if seq_len is not None:
        nt = seq_len // tm
        lead = nt - WINDOW // tm
        kv_t = pl.BlockSpec((1, 256, tm), lambda i: (i // nt, 0, i % nt))
        win_t = pl.BlockSpec((1, 256, tm), lambda i: (i // nt, 0, jnp.maximum(i % nt - lead, 0)))
        out_specs += [kv_t, kv_t, win_t]
        out_shape += [jax.ShapeDtypeStruct((n // seq_len, 256, seq_len), _F32)] * 2 + [
            jax.ShapeDtypeStruct((n // seq_len, 256, WINDOW), _F32)]
    return pl.pallas_call(
        _inproj_body,
        grid=(n // tm,),
        in_specs=[row(D_MODEL)] + [full(c) for c in consts],
        out_specs=out_specs,
        out_shape=out_shape,
        compiler_params=pltpu.CompilerParams(dimension_semantics=("arbitrary",), vmem_limit_bytes=VMEM_LIMIT),
        name="inproj",
    )(x, *consts)


FF_CHUNK = 1024


def _tail_body(halo, h_ref, o_ref, cb_ref, u_ref, up_ref, p_ref, cw_ref, on_ref, nmlp_ref, nple_ref,
               wout_hbm, wup_hbm, wdn_hbm, wgate_hbm, wproj_hbm, y_ref,
               uext_ref, wout_ref, wup_ref, wdn_ref, wgate_ref, wproj_ref, wsem):
    @pl.when(pl.program_id(0) == 0)
    def _load_weights():
        copies = [pltpu.make_async_copy(src, dst, wsem.at[i]) for i, (src, dst) in enumerate(
            ((wout_hbm, wout_ref), (wup_hbm, wup_ref), (wdn_hbm, wdn_ref), (wgate_hbm, wgate_ref),
             (wproj_hbm, wproj_ref)))]
        for c in copies:
            c.start()
        for c in copies:
            c.wait()

    tm = h_ref.shape[0]
    u = u_ref[...]
    if halo:
        first = (pl.program_id(0) % halo) == 0
        prev = jnp.where(first, 0.0, up_ref[...])
        uext_ref[0:8, :] = prev
        uext_ref[8:tm + 8, :] = u
        u2 = uext_ref[6:tm + 6, :]
        u1 = uext_ref[7:tm + 7, :]
    else:
        u2 = up_ref[0]
        u1 = up_ref[1]
    yc = cw_ref[0:1, :] * u2 + cw_ref[1:2, :] * u1 + cw_ref[2:3, :] * u
    mix_a = _rms_rows(o_ref[...], on_ref[:, 0:ATT_WIDTH]).astype(_BF)
    mix_c = _rms_rows(cb_ref[...] * yc, on_ref[:, ATT_WIDTH:]).astype(_BF)
    h = h_ref[...] + _dot(mix_a, wout_ref[0:ATT_WIDTH, :]) + _dot(mix_c, wout_ref[ATT_WIDTH:, :])
    a = _rms_rows(h, nmlp_ref[...]).astype(_BF)
    y_ref[...] = h
    for c in range(D_FF // FF_CHUNK):
        t = jnp.maximum(_dot(a, wup_ref[:, c * FF_CHUNK:(c + 1) * FF_CHUNK]), 0.0)
        y_ref[...] += _dot((t * t).astype(_BF), wdn_ref[c * FF_CHUNK:(c + 1) * FF_CHUNK, :])
    h = y_ref[...]
    a = _rms_rows(h, nple_ref[...]).astype(_BF)
    gate = jax.nn.sigmoid(_dot(a, wgate_ref[...]))
    y_ref[...] = h + gate * _dot(p_ref[...].astype(_BF), wproj_ref[...])


def _tail(h, o_att, cb, u, u_prev, p, w, seq_len):
    n = h.shape[0]
    tm = min(ROW_TILE, n)
    row = lambda c: pl.BlockSpec((tm, c), lambda i: (i, 0))
    const = lambda a: pl.BlockSpec(a.shape, lambda i: (0,) * a.ndim)
    if seq_len is not None:
        halo = seq_len // tm
        up_spec = pl.BlockSpec((8, CONV_DIM), lambda i: (jnp.maximum(i * (tm // 8) - 1, 0), 0))
        up = u
    else:
        halo = 0
        up_spec = pl.BlockSpec((2, tm, CONV_DIM), lambda i: (0, i, 0))
        up = u_prev
    consts = (w['conv_w'], w['out_norm'], w['norm_mlp'], w['norm_ple'])
    mats = (w['w_out'], w['w_up'], w['w_down'], w['w_ple_gate'], w['w_ple_proj'])
    return pl.pallas_call(
        functools.partial(_tail_body, halo),
        grid=(n // tm,),
        in_specs=[row(D_MODEL), row(ATT_WIDTH), row(CONV_DIM), row(CONV_DIM), up_spec, row(PLE_DIM)]
        + [const(c) for c in consts] + [pl.BlockSpec(memory_space=pl.ANY)] * len(mats),
        out_specs=row(D_MODEL),
        out_shape=jax.ShapeDtypeStruct((n, D_MODEL), _F32),
        scratch_shapes=[pltpu.VMEM((tm + 8, CONV_DIM), _F32)] + [pltpu.VMEM(m.shape, _BF) for m in mats]
        + [pltpu.SemaphoreType.DMA((len(mats),))],
        compiler_params=pltpu.CompilerParams(dimension_semantics=("arbitrary",), vmem_limit_bytes=VMEM_LIMIT),
        name="tail",
    )(h, o_att, cb, u, up, p, *consts, *mats)


def _split_heads(x, lane_lo):
    xr = pltpu.roll(x, HEAD_DIM, axis=1)
    zero = jnp.zeros_like(x)
    a = (jnp.where(lane_lo, x, zero), jnp.where(lane_lo, xr, zero))
    b = (jnp.where(lane_lo, zero, xr), jnp.where(lane_lo, zero, x))
    return a, b


def _compress_rows(load, ns, bd_ref, pe_ref):
    acc = [jnp.zeros((2 * ns, LANES), _F32), jnp.zeros((2 * ns, LANES), _F32)]
    for l in range(CMP_BLOCK):
        for plane in range(2):
            x = jnp.concatenate([load(plane, l), load(plane, CMP_BLOCK + l)], axis=0)
            x = x + pe_ref[l, plane:plane + 1, :]
            acc[plane] = acc[plane] + _dot(x.astype(_BF), bd_ref[l, plane])
    return acc


def _compress_prompt_body(slab_ref, bd_ref, pe_ref, kg_ref, g128_ref, kA_ref, kB_ref, vA_ref, vB_ref):
    ns = slab_ref.shape[1] // (2 * SEL_BLOCK)
    k, v = _compress_rows(lambda plane, t0: slab_ref[0, pl.ds(2 * t0 + plane, ns, stride=2 * SEL_BLOCK), :],
                          ns, bd_ref, pe_ref)
    k = _group_rms(k, g128_ref[...], kg_ref[...])
    lane_lo = lax.broadcasted_iota(jnp.int32, k.shape, 1) < HEAD_DIM
    ka, kb = _split_heads(k, lane_lo)
    va, vb = _split_heads(v, lane_lo)
    pad = jnp.zeros((LANES - 2 * ns, LANES), _BF)
    for g in range(N_KV):
        for ref, val in ((kA_ref, ka[g]), (kB_ref, kb[g]), (vA_ref, va[g]), (vB_ref, vb[g])):
            ref[0, g, 0:2 * ns, :] = val.astype(_BF)
            if 2 * ns < LANES:
                ref[0, g, 2 * ns:, :] = pad


def _compress_prompt(slab, w):
    b, t, _ = slab.shape
    assert t % QBLK == 0 and t // CMP_BLOCK <= LANES
    slab = slab.reshape(b, 2 * t, LANES)
    full = lambda a: pl.BlockSpec(a.shape, lambda i: (0,) * a.ndim)
    consts = (w['bd'], w['pe_t'], w['kc_gain'], w['g128'])
    out = jax.ShapeDtypeStruct((b, N_KV, LANES, LANES), _BF)
    return pl.pallas_call(
        _compress_prompt_body,
        grid=(b,),
        in_specs=[pl.BlockSpec((1, 2 * t, LANES), lambda i: (i, 0, 0))] + [full(c) for c in consts],
        out_specs=[pl.BlockSpec((1, N_KV, LANES, LANES), lambda i: (i, 0, 0, 0))] * 4,
        out_shape=[out] * 4,
        compiler_params=pltpu.CompilerParams(dimension_semantics=("arbitrary",), vmem_limit_bytes=VMEM_LIMIT),
        name="compress_prompt",
    )(slab, *consts)


BUILD_ROWS = 256
QSTEP = 2


def _top_extra(impb, cand, lane_f, n_extra):
    v = jnp.where(cand, impb, -1.0)
    picked = jnp.zeros(impb.shape, _F32)
    for _ in range(n_extra):
        mx = jnp.max(v, axis=-1, keepdims=True)
        first = jnp.min(jnp.where(v == mx, lane_f, 1e9), axis=-1, keepdims=True)
        hit = lane_f == first
        picked = jnp.where(hit, 1.0, picked)
        v = jnp.where(hit, -1.0, v)
    return picked


def _attn_prompt_body(q_ref, ks_ref, kw_ref, gate_ref, kcA_ref, kcB_ref, vcA_ref, vcB_ref, cbias_ref, tiles_ref,
                      eg_ref, o_ref, ksA, ksB, vsA, vsB, kwA, kwB, vwA, vwB, s_scr, m_scr, acc_scr):
    step = pl.program_id(1)
    t_len = ks_ref.shape[1]
    n_sel = t_len // SEL_BLOCK

    @pl.when(step == 0)
    def _build():
        def chunk(c, carry):
            r0 = pl.multiple_of(c * BUILD_ROWS, BUILD_ROWS)
            rows = pl.ds(r0, BUILD_ROWS)
            lane = lax.broadcasted_iota(jnp.int32, (BUILD_ROWS, LANES), 1)
            blk = (r0 + lax.broadcasted_iota(jnp.int32, (BUILD_ROWS, LANES), 0)) // SEL_BLOCK
            lane_lo = lane < HEAD_DIM
            oh_hi = jnp.where(lane == blk + HEAD_DIM, 1.0, 0.0)
            oh_lo = jnp.where(lane == blk, 1.0, 0.0)
            one_hi = jnp.where(lane == HEAD_DIM, 1.0, 0.0)
            one_lo = jnp.where(lane == 0, 1.0, 0.0)
            for src, k_a, k_b, v_a, v_b, onehot in ((ks_ref, ksA, ksB, vsA, vsB, True),
                                                    (kw_ref, kwA, kwB, vwA, vwB, False)):
                ka, kb = _split_heads(src[0, rows, 0:128], lane_lo)
                va, vb = _split_heads(src[0, rows, 128:256], lane_lo)
                for g in range(N_KV):
                    if onehot:
                        k_a[g, rows, :] = jnp.where(lane_lo, ka[g], oh_hi).astype(_BF)
                        k_b[g, rows, :] = jnp.where(lane_lo, oh_lo, kb[g]).astype(_BF)
                    else:
                        k_a[g, rows, :] = ka[g].astype(_BF)
                        k_b[g, rows, :] = kb[g].astype(_BF)
                    v_a[g, rows, :] = jnp.where(lane_lo, va[g], one_hi).astype(_BF)
                    v_b[g, rows, :] = jnp.where(lane_lo, one_lo, vb[g]).astype(_BF)
            return carry
        lax.fori_loop(0, t_len // BUILD_ROWS, chunk, 0)

    lane2 = lax.broadcasted_iota(jnp.int32, (2 * QBLK, LANES), 1)
    lo2 = lane2 < HEAD_DIM
    lane1 = lax.broadcasted_iota(jnp.int32, (QBLK, LANES), 1)
    lane1_f = lane1.astype(_F32)
    row1 = lax.broadcasted_iota(jnp.int32, (QBLK, LANES), 0)
    halves = range(QSTEP)
    qbs = [step * QSTEP + h for h in halves]
    qb_last = qbs[-1]
    rows_of = [slice(h * QBLK, (h + 1) * QBLK) for h in halves]
    curs = [(qb * QBLK + row1) // SEL_BLOCK for qb in qbs]

    gates = gate_ref[0]
    g_hi = gates.astype(_BF)
    g_lo = (gates - g_hi.astype(_F32)).astype(_BF)
    gexp = _dot(g_hi, eg_ref[...]) + _dot(g_lo, eg_ref[...])

    def normalize(acc, x):
        l = jnp.sum(jnp.where(lane2 == (HEAD_DIM if x == 0 else 0), acc, 0.0), axis=-1, keepdims=True)
        keep = lo2 if x == 0 else jnp.logical_not(lo2)
        return jnp.where(keep, acc / l, 0.0)

    chains = [(h, g, x) for h in halves for g in range(N_KV) for x in range(2)]
    o_cmp, q_plain, q_pair, importance = [], [], [], []
    for h in halves:
        for g in range(N_KV):
            qs = jnp.concatenate([q_ref[0, rows_of[h], (2 * g) * LANES:(2 * g + 1) * LANES],
                                  q_ref[0, rows_of[h], (2 * g + 1) * LANES:(2 * g + 2) * LANES]], axis=0)
            zero = jnp.zeros_like(qs)
            q_a = jnp.where(lo2, qs, zero)
            q_b = jnp.where(lo2, zero, qs)
            qs2 = jnp.concatenate([q_ref[0, rows_of[h], ATT_WIDTH + (2 * g) * LANES:ATT_WIDTH + (2 * g + 1) * LANES],
                                   q_ref[0, rows_of[h], ATT_WIDTH + (2 * g + 1) * LANES:ATT_WIDTH + (2 * g + 2) * LANES]],
                                  axis=0)
            q_pair.append(qs2)
            q_plain += [jnp.where(lo2, qs2, zero), jnp.where(lo2, zero, qs2)]

            def cmp_probs(qx, k_ref, x):
                s = _dot_nt(qx, k_ref[0, g]) + cbias_ref[h, g, x]
                m = jnp.max(s, axis=-1, keepdims=True)
                e = jnp.where(s > 0.5 * NEG, jnp.exp(s - m), 0.0)
                l = jnp.sum(e, axis=-1, keepdims=True)
                return e / jnp.where(l > 0.0, l, 1.0)
            p_a = cmp_probs(q_a, kcA_ref, 0)
            p_b = cmp_probs(q_b, kcB_ref, 1)
            o_cmp.append(_dot(p_a.astype(_BF), vcA_ref[0, g]) + _dot(p_b.astype(_BF), vcB_ref[0, g]))
            imp = p_a[0:QBLK] + p_a[QBLK:] + p_b[0:QBLK] + p_b[QBLK:]
            importance.append(imp + pltpu.roll(imp, LANES - n_sel, axis=1))

    two = lambda a: jnp.concatenate([a, a], axis=0)
    cands = jnp.concatenate([two((lane1 >= 1) & (lane1 <= curs[h] - 2)) for h in halves], axis=0)
    picked = lax.cond(qb_last * QBLK >= TOP_N * SEL_BLOCK,
                      lambda: _top_extra(jnp.concatenate(importance, axis=0), cands,
                                         jnp.concatenate([lane1_f] * (2 * QSTEP), axis=0), TOP_N - 3),
                      lambda: jnp.zeros((2 * QSTEP * QBLK, LANES), _F32))
    q_aug = []
    for h in halves:
        cur = curs[h]
        forced = (lane1 == 0) | (lane1 == cur) | (lane1 == cur - 1)
        few = cur <= TOP_N - 1
        for g in range(N_KV):
            i = h * N_KV + g
            chosen = forced | (few & (lane1 <= cur)) | ((picked[i * QBLK:(i + 1) * QBLK] > 0.5) & jnp.logical_not(few))
            sb_lo = jnp.where(chosen | (lane1 >= n_sel), 0.0, NEG)
            sb_hi = pltpu.roll(sb_lo, HEAD_DIM, axis=1)
            q_aug += [jnp.where(lo2, q_pair[i], two(sb_hi).astype(_BF)),
                      jnp.where(lo2, two(sb_lo).astype(_BF), q_pair[i])]

    k_sel, v_sel, k_win, v_win = (ksA, ksB), (vsA, vsB), (kwA, kwB), (vwA, vwB)
    n_win = WINDOW // QBLK + 1

    o_win = []
    for ch, (h, g, x) in enumerate(chains):
        w_first = jnp.maximum(qbs[h] - (n_win - 1), 0)
        w_rows = pl.ds(pl.multiple_of(w_first * QBLK, QBLK), n_win * QBLK)
        s = _dot_nt(q_plain[ch], k_win[x][g, w_rows, :])
        parts = []
        for i in range(n_win):
            idx = qbs[h] - w_first - i
            parts.append(s[:, i * QBLK:(i + 1) * QBLK] + tiles_ref[jnp.where(idx < 0, n_win, idx), g, x])
        m = parts[0]
        for part in parts[1:]:
            m = jnp.maximum(m, part)
        m = jnp.max(m, axis=-1, keepdims=True)
        p = jnp.concatenate([jnp.exp2(part - m).astype(_BF) for part in parts], axis=1)
        o_win.append(normalize(_dot(p, v_win[x][g, w_rows, :]), x))

    n_quart = 4
    qw = t_len // n_quart
    per_q = qw // QBLK
    m_scr[...] = jnp.full(m_scr.shape, NEG, _F32)
    for qi in range(n_quart):
        @pl.when(qb_last * QBLK >= qi * qw)
        def _scores(qi=qi):
            for ch, (h, g, x) in enumerate(chains):
                s = _dot_nt(q_aug[ch], k_sel[x][g, qi * qw:(qi + 1) * qw, :])
                m = m_scr[ch]
                for ci in range(per_q):
                    c = qi * per_q + ci
                    idx = qbs[h] - c
                    sb = s[:, ci * QBLK:(ci + 1) * QBLK] + tiles_ref[jnp.where(idx < 0, n_win, jnp.minimum(idx, 2)), g, x]
                    s_scr[ch, :, c * QBLK:(c + 1) * QBLK] = sb
                    m = jnp.maximum(m, sb)
                m_scr[ch] = m
    m_sel = [jnp.max(m_scr[ch], axis=-1, keepdims=True) for ch in range(len(chains))]

    acc_scr[...] = jnp.zeros(acc_scr.shape, _F32)
    for qi in range(n_quart):
        @pl.when(qb_last * QBLK >= qi * qw)
        def _weighted(qi=qi):
            for ch, (h, g, x) in enumerate(chains):
                p = jnp.exp2(s_scr[ch, :, qi * qw:(qi + 1) * qw] - m_sel[ch]).astype(_BF)
                acc_scr[ch] += _dot(p, v_sel[x][g, qi * qw:(qi + 1) * qw, :])

    for h in halves:
        for g in range(N_KV):
            i = h * N_KV + g
            o_c = o_cmp[i]
            o_s = normalize(acc_scr[2 * i], 0) + normalize(acc_scr[2 * i + 1], 1)
            o_w = o_win[2 * i] + o_win[2 * i + 1]
            for pr in range(2):
                rows = slice(pr * QBLK, (pr + 1) * QBLK)
                col = (2 * g + pr) * LANES
                gx = lambda br: gexp[rows_of[h], br * ATT_WIDTH + col:br * ATT_WIDTH + col + LANES]
                o_ref[0, rows_of[h], col:col + LANES] = gx(0) * o_c[rows] + gx(1) * o_s[rows] + gx(2) * o_w[rows]


def _attn_prompt(q, sel, win, gates, kc, cbias, w):
    b, t, _ = sel.shape
    nq = t // QBLK
    assert t // SEL_BLOCK <= CMP_BLOCK and t % (4 * QBLK) == 0 and t >= WINDOW + QBLK and nq % QSTEP == 0
    kcA, kcB, vcA, vcB = kc
    full = lambda a: pl.BlockSpec(a.shape, lambda i, j: (0,) * a.ndim)
    slab = pl.BlockSpec((1, t, 256), lambda i, j: (i, 0, 0))
    kcs = pl.BlockSpec((1, N_KV, LANES, LANES), lambda i, j: (i, 0, 0, 0))
    n_chain = QSTEP * 2 * N_KV
    q_rows = QSTEP * QBLK
    scratch = [pltpu.VMEM((N_KV, t, LANES), _BF)] * 8 + [
        pltpu.VMEM((n_chain, 2 * QBLK, t), _F32),
        pltpu.VMEM((n_chain, 2 * QBLK, LANES), _F32), pltpu.VMEM((n_chain, 2 * QBLK, LANES), _F32)]
    return pl.pallas_call(
        _attn_prompt_body,
        grid=(b, nq // QSTEP),
        in_specs=[pl.BlockSpec((1, q_rows, 2 * ATT_WIDTH), lambda i, j: (i, j, 0)), slab, slab,
                  pl.BlockSpec((1, q_rows, LANES), lambda i, j: (i, j, 0)), kcs, kcs, kcs, kcs,
                  pl.BlockSpec((QSTEP, N_KV, 2, 2 * QBLK, LANES), lambda i, j: (j, 0, 0, 0, 0)),
                  full(w['tiles']), full(w['egate'])],
        out_specs=pl.BlockSpec((1, q_rows, ATT_WIDTH), lambda i, j: (i, j, 0)),
        out_shape=jax.ShapeDtypeStruct((b, t, ATT_WIDTH), _F32),
        scratch_shapes=scratch,
        compiler_params=pltpu.CompilerParams(dimension_semantics=("arbitrary", "arbitrary"),
                                             vmem_limit_bytes=VMEM_LIMIT),
        name="attn_prompt",
    )(q, sel, win, gates, kcA, kcB, vcA, vcB, cbias, w['tiles'], w['egate'])


N_PICK = TOP_N - 3
N_SLOT = TOP_N - 1
PAGE_ROWS = 2 * KV_WIDTH


PAIR_TOKENS = 2 * PAGE_SIZE


def _sample_cmp_body(pt_ref, q_ref, cache_ref, bd_ref, pet_ref, perm_ref, kg_ref, g128_ref, sb_ref, oc_ref, pick_ref,
                     buf, rows_scr, sem):
    b = pl.program_id(0)
    nb = pl.num_programs(0)
    n_pairs = pt_ref.shape[1] // 2
    nc = 8 * n_pairs
    slot = b % 2

    def page_copy(bb, pair, half, sl):
        return pltpu.make_async_copy(cache_ref.at[pt_ref[bb, 2 * pair + half]],
                                     buf.at[sl, pair, :, half * PAGE_SIZE:(half + 1) * PAGE_SIZE], sem.at[sl])

    def for_pages(fn):
        def step(pair, c):
            fn(pair, 0)
            fn(pair, 1)
            return c
        lax.fori_loop(0, n_pairs, step, 0)

    @pl.when(b == 0)
    def _prime():
        for_pages(lambda pair, half: page_copy(0, pair, half, 0).start())

    @pl.when(b + 1 < nb)
    def _prefetch():
        for_pages(lambda pair, half: page_copy(b + 1, pair, half, 1 - slot).start())

    for_pages(lambda pair, half: page_copy(b, pair, half, slot).wait())

    def regroup(pair, c):
        x = (buf[slot, pair] + pet_ref[...]).astype(_BF)
        rows_scr[pair] = _dot_nt(perm_ref[...], x)
        return c
    lax.fori_loop(0, n_pairs, regroup, 0, unroll=4)

    acc = jnp.zeros((nc, 2 * KV_WIDTH), _F32)
    for l in range(CMP_BLOCK):
        x = rows_scr[:, 8 * l:8 * (l + 1), :].reshape(nc, 2 * KV_WIDTH)
        acc = acc + _dot(x.astype(_BF), bd_ref[l])
    k = _group_rms(acc[:, 0:KV_WIDTH], g128_ref[...], kg_ref[...]).astype(_BF)
    v = acc[:, KV_WIDTH:].astype(_BF)
    s = _dot_nt(q_ref[0], k) + sb_ref[...]
    m = jnp.max(s, axis=-1, keepdims=True)
    e = jnp.exp(s - m)
    p = e / jnp.sum(e, axis=-1, keepdims=True)
    oc_ref[0] = _dot(p.astype(_BF), v)
    imp8 = p + pltpu.roll(p, nc - 1, axis=1)
    rows = [imp8[h:h + 1] for h in range(N_HEADS)]
    imp_g = [rows[g * GROUP] + rows[g * GROUP + 1] + rows[g * GROUP + 2] + rows[g * GROUP + 3] for g in range(N_KV)]
    imp = jnp.concatenate(imp_g + [jnp.full((8 - N_KV, nc), -1.0, _F32)], axis=0)
    lane = lax.broadcasted_iota(jnp.int32, (8, nc), 1)
    lane_f = lane.astype(_F32)
    v_c = jnp.where((lane % 2 == 0) & (lane >= 2) & (lane <= nc - 4), imp, -1.0)
    picks = jnp.zeros((8, nc), _F32)
    for i in range(N_PICK):
        mx = jnp.max(v_c, axis=-1, keepdims=True)
        first = jnp.min(jnp.where(v_c == mx, lane_f, 1e9), axis=-1, keepdims=True)
        picks = jnp.where(lane == i, first * 0.5, picks)
        v_c = jnp.where(lane_f == first, -1.0, v_c)
    pick_ref[0] = picks.astype(jnp.int32)


def _sample_cmp(page_table, q8, cache, sbias, w):
    db, n_pages = page_table.shape
    nc = 4 * n_pages
    assert n_pages % 2 == 0 and nc // 2 - 2 >= N_PICK
    full = lambda a: pl.BlockSpec(a.shape, lambda i, pt: (0,) * a.ndim)
    consts = (w['bd_kv'], w['pe_tok'], w['perm'], w['kc_gain'], w['g128'], sbias)
    grid_spec = pltpu.PrefetchScalarGridSpec(
        num_scalar_prefetch=1,
        grid=(db,),
        in_specs=[pl.BlockSpec((1, 8, LANES), lambda i, pt: (i, 0, 0)), pl.BlockSpec(memory_space=pl.ANY)]
        + [full(c) for c in consts],
        out_specs=[pl.BlockSpec((1, 8, LANES), lambda i, pt: (i, 0, 0)),
                   pl.BlockSpec((1, 8, nc), lambda i, pt: (i, 0, 0))],
        scratch_shapes=[pltpu.VMEM((2, n_pages // 2, 2 * KV_WIDTH, PAIR_TOKENS), _F32),
                        pltpu.VMEM((n_pages // 2, PAIR_TOKENS, 2 * KV_WIDTH), _F32),
                        pltpu.SemaphoreType.DMA((2,))],
    )
    return pl.pallas_call(
        _sample_cmp_body,
        grid_spec=grid_spec,
        out_shape=[jax.ShapeDtypeStruct((db, 8, LANES), _F32), jax.ShapeDtypeStruct((db, 8, nc), jnp.int32)],
        compiler_params=pltpu.CompilerParams(dimension_semantics=("arbitrary",), vmem_limit_bytes=VMEM_LIMIT),
        name="sample_cmp",
    )(page_table, q8, cache, *consts)


def _sample_attn_body(pt_ref, pick_ref, q_ref, cache_ref, win_ref, nsel_ref, nwin_ref, ncol_ref, gate_ref, oc_ref,
                      bsel_ref, b0_ref, bwin_ref, o_ref, wout_ref, kb, sem):
    b = pl.program_id(0)
    nb = pl.num_programs(0)
    nj = 2 * pt_ref.shape[1]
    wlen = win_ref.shape[2]
    slot = b % 2

    def block_of(bb, g, s):
        if s == 0:
            return 0
        if s == N_SLOT - 1:
            return nj - 1
        return pick_ref[bb, g * N_PICK + (s - 1)]

    def tile_copy(bb, g, s, plane, sl):
        page = pt_ref[bb, block_of(bb, g, s) // 2]
        return pltpu.make_async_copy(cache_ref.at[page, plane * N_KV + g],
                                     kb.at[sl, g, plane, :, pl.ds(s * PAGE_SIZE, PAGE_SIZE)], sem.at[sl])

    def for_tiles(fn):
        for g in range(N_KV):
            for s in range(N_SLOT):
                for plane in range(2):
                    fn(g, s, plane)

    @pl.when(b == 0)
    def _prime():
        for_tiles(lambda g, s, plane: tile_copy(0, g, s, plane, 0).start())

    @pl.when(b + 1 < nb)
    def _prefetch():
        for_tiles(lambda g, s, plane: tile_copy(b + 1, g, s, plane, 1 - slot).start())

    q8 = q_ref[0]
    q8f = q8.astype(_F32)
    head_g = lax.broadcasted_iota(jnp.int32, (8, HEAD_DIM), 0) // GROUP
    own_half = lambda x: jnp.where(head_g == 0, x[:, 0:HEAD_DIM], x[:, HEAD_DIM:])
    b0 = b0_ref[:, 0:1]

    x = win_ref[0]
    lane_w = lax.broadcasted_iota(jnp.int32, x.shape, 1)
    wout_ref[0] = jnp.where(lane_w == wlen - 1, ncol_ref[0], pltpu.roll(x, wlen - 1, axis=1))
    sw = _dot(q8, x[0:KV_WIDTH, :].astype(_BF)) + bwin_ref[...]
    sw_new = jnp.sum(q8f * nwin_ref[0][:, 0:KV_WIDTH], axis=-1, keepdims=True) + b0
    mw = jnp.maximum(jnp.max(sw, axis=-1, keepdims=True), sw_new)
    ew = jnp.exp(sw - mw)
    ew_new = jnp.exp(sw_new - mw)
    o_w = _dot_nt(ew.astype(_BF), x[KV_WIDTH:, :].astype(_BF)) + ew_new * nwin_ref[0][:, KV_WIDTH:]
    o_w = own_half(o_w / (jnp.sum(ew, axis=-1, keepdims=True) + ew_new))

    for_tiles(lambda g, s, plane: tile_copy(b, g, s, plane, slot).wait())

    ss_new = jnp.sum(q8f * nsel_ref[0][:, 0:KV_WIDTH], axis=-1, keepdims=True) + b0
    lane_half = lax.broadcasted_iota(jnp.int32, (8, PAGE_SIZE), 1) // SEL_BLOCK
    o_sel = []
    for g in range(N_KV):
        pieces = []
        for s in range(N_SLOT):
            j = jnp.full((8, PAGE_SIZE), block_of(b, g, s), jnp.int32)
            if s == 0:
                tab = bsel_ref[0]
            elif s == N_SLOT - 1:
                tab = bsel_ref[2]
            else:
                tab = jnp.where(j == nj - 2, bsel_ref[1], bsel_ref[0])
            pieces.append(jnp.where(lane_half == j % 2, tab, NEG))
        ss = _dot(q8[:, g * HEAD_DIM:(g + 1) * HEAD_DIM], kb[slot, g, 0].astype(_BF)) + jnp.concatenate(pieces, axis=1)
        ms = jnp.maximum(jnp.max(ss, axis=-1, keepdims=True), ss_new)
        es = jnp.exp(ss - ms)
        es_new = jnp.exp(ss_new - ms)
        v_new = nsel_ref[0][:, KV_WIDTH + g * HEAD_DIM:KV_WIDTH + (g + 1) * HEAD_DIM]
        o_g = _dot_nt(es.astype(_BF), kb[slot, g, 1].astype(_BF)) + es_new * v_new
        o_sel.append(o_g / (jnp.sum(es, axis=-1, keepdims=True) + es_new))
    o_s = jnp.where(head_g == 0, o_sel[0], o_sel[1])
    gates = gate_ref[0]
    o_ref[0] = gates[:, 0:1] * own_half(oc_ref[0]) + gates[:, 1:2] * o_s + gates[:, 2:3] * o_w


def _sample_attn(page_table, picks, q8, cache, win, new_sel, new_win, gates8, o_c, tabs):
    db, n_pages = page_table.shape
    wlen = win.shape[2]
    full = lambda a: pl.BlockSpec(a.shape, lambda i, pt, pk: (0,) * a.ndim)
    per_b = lambda r, c: pl.BlockSpec((1, r, c), lambda i, pt, pk: (i, 0, 0))
    bsel, b0, bwin = tabs
    grid_spec = pltpu.PrefetchScalarGridSpec(
        num_scalar_prefetch=2,
        grid=(db,),
        in_specs=[per_b(8, LANES), pl.BlockSpec(memory_space=pl.ANY), per_b(2 * KV_WIDTH, wlen), per_b(1, 256),
                  per_b(1, 256), per_b(2 * KV_WIDTH, 1), per_b(8, LANES), per_b(8, LANES),
                  full(bsel), full(b0), full(bwin)],
        out_specs=[per_b(8, HEAD_DIM), per_b(2 * KV_WIDTH, wlen)],
        scratch_shapes=[pltpu.VMEM((2, N_KV, 2, HEAD_DIM, N_SLOT * PAGE_SIZE), _F32), pltpu.SemaphoreType.DMA((2,))],
    )
    return pl.pallas_call(
        _sample_attn_body,
        grid_spec=grid_spec,
        out_shape=[jax.ShapeDtypeStruct((db, 8, HEAD_DIM), _F32), jax.ShapeDtypeStruct((db, 2 * KV_WIDTH, wlen), _F32)],
        compiler_params=pltpu.CompilerParams(dimension_semantics=("arbitrary",), vmem_limit_bytes=VMEM_LIMIT),
        name="sample_attn",
    )(page_table, picks, q8, cache, win, new_sel, new_win, new_win[:, 0, :, None], gates8, o_c, bsel, b0, bwin)


def _rel_bucket(dist):
    n = np.maximum(np.asarray(dist), 0)
    max_exact = RP_BUCKETS // 2
    nf = np.maximum(n, 1).astype(np.float32)
    large = max_exact + (np.log(nf / np.float32(max_exact)) / np.float32(math.log(RP_MAX_DIST / max_exact))
                         * np.float32(RP_BUCKETS - max_exact)).astype(np.int32)
    large = np.minimum(large, RP_BUCKETS - 1)
    return np.where(n < max_exact, n, large)


def _bias_of(rel_bias, dist):
    onehot = _rel_bucket(dist)[..., None, None] == np.arange(RP_BUCKETS)[:, None]
    return jnp.sum(jnp.where(onehot, rel_bias, 0.0), axis=-2)


def _head_major(x, lead):
    n = x.shape[-2]
    x = x.reshape(lead + (QBLK, n, N_KV, 2, 2))
    nl = len(lead)
    x = jnp.transpose(x, tuple(range(nl)) + (nl + 2, nl + 4, nl + 3, nl, nl + 1))
    return x.reshape(lead + (N_KV, 2, 2 * QBLK, n))


def _prompt_tables(rel_bias, t_len):
    nq = t_len // QBLK
    ns = t_len // SEL_BLOCK
    ti = np.arange(QBLK)[:, None]
    ki = np.arange(QBLK)[None, :]
    n_idx = WINDOW // QBLK + 1
    pad = QBLK - 1
    bvec = _bias_of(rel_bias, np.arange(-pad, t_len + pad))
    wins = jnp.stack([bvec[i * QBLK:i * QBLK + 2 * QBLK - 1][::-1] for i in range(n_idx)])
    skew = jnp.tile(wins, (1, QBLK + 1, 1))[:, :QBLK * 2 * QBLK].reshape(n_idx, QBLK, 2 * QBLK, N_HEADS)
    tiles = skew[:, ::-1, :QBLK]
    dist = np.stack([i * QBLK + ti - ki for i in range(n_idx)])
    valid = dist >= 0
    valid[n_idx - 1] &= dist[n_idx - 1] < WINDOW
    tiles = jnp.where(valid[..., None], tiles, NEG)
    tiles = jnp.concatenate([tiles, jnp.full_like(tiles[:1], NEG)])
    tiles = _head_major(tiles * LOG2E, (n_idx + 1,))
    shift = CMP_BLOCK * ns * 2
    bcmp = _bias_of(rel_bias, np.arange(-shift, t_len))
    cols = [bcmp[shift - (CMP_BLOCK * blk + CMP_BLOCK - 1):][:t_len]
            for blk in list(range(0, 2 * ns, 2)) + list(range(1, 2 * ns, 2))]
    cb = jnp.stack(cols, axis=1)
    cb = jnp.pad(cb, ((0, 0), (0, LANES - 2 * ns), (0, 0)))
    lane = np.arange(LANES)
    blk = np.where(lane < ns, 2 * lane, 2 * (lane - ns) + 1)
    dist_c = np.arange(t_len)[:, None] - (CMP_BLOCK * blk + CMP_BLOCK - 1)[None, :]
    valid_c = (dist_c >= 0) & (lane < 2 * ns)[None, :]
    cb = jnp.where(valid_c[..., None], cb, NEG)
    cb = _head_major(cb.reshape(nq, QBLK, LANES, N_HEADS), (nq,))
    return tiles, cb


def _sample_tables(rel_bias, past_len, wlen):
    blk = np.arange(past_len // CMP_BLOCK)
    sbias = _bias_of(rel_bias, past_len - (CMP_BLOCK * blk + CMP_BLOCK - 1)).T
    pos = np.arange(PAGE_SIZE) % SEL_BLOCK
    bsel = jnp.stack([_bias_of(rel_bias, np.full(PAGE_SIZE, past_len)).T,
                      _bias_of(rel_bias, 2 * SEL_BLOCK - pos).T,
                      _bias_of(rel_bias, SEL_BLOCK - pos).T])
    b0 = _bias_of(rel_bias, np.zeros((LANES,), np.int32)).T
    tok = np.arange(wlen)
    bwin = jnp.where((tok >= 1)[None, :], _bias_of(rel_bias, wlen - tok).T, NEG)
    return sbias, (bsel, b0, bwin)


def _prep(norm_mix, w_in, q_norm, k_norm, cmp_pe, w_cmp, conv_w, out_norm, w_out, norm_mlp, w_up, w_down,
          norm_ple, w_ple_gate, w_ple_proj):
    w_in = w_in[0]
    o_kv = ATT_WIDTH
    o_g = o_kv + 6 * KV_WIDTH
    o_c = o_g + 3 * N_HEADS
    bd = jnp.einsum('gh,plde->lpgdhe', jnp.eye(N_KV, dtype=_F32), w_cmp[0]).reshape(CMP_BLOCK, 2, LANES, LANES)
    pe_t = jnp.transpose(jnp.tile(cmp_pe[0], (1, 1, N_KV)), (1, 0, 2))
    egate = np.zeros((LANES, 3 * ATT_WIDTH), np.float32)
    for br in range(3):
        for h in range(N_HEADS):
            egate[br * N_HEADS + h, br * ATT_WIDTH + h * HEAD_DIM:br * ATT_WIDTH + (h + 1) * HEAD_DIM] = 1.0
    ones = np.ones((HEAD_DIM, HEAD_DIM), np.float32)
    perm = np.zeros((PAIR_TOKENS, PAIR_TOKENS), np.float32)
    tok = np.arange(PAIR_TOKENS)
    perm[(tok % CMP_BLOCK) * (PAIR_TOKENS // CMP_BLOCK) + tok // CMP_BLOCK, tok] = 1.0
    pe_tok = jnp.tile(jnp.transpose(cmp_pe[0], (0, 2, 1)), (1, N_KV, PAIR_TOKENS // CMP_BLOCK))
    pe_tok = pe_tok.reshape(2 * KV_WIDTH, PAIR_TOKENS)
    zero = jnp.zeros_like(bd[:, 0])
    bd_kv = jnp.concatenate([jnp.concatenate([bd[:, 0], zero], axis=2),
                             jnp.concatenate([zero, bd[:, 1]], axis=2)], axis=1)
    return {
        'perm': jnp.asarray(perm, _BF), 'pe_tok': pe_tok, 'bd_kv': bd_kv.astype(_BF),
        'norm_mix': norm_mix[0][None], 'wq': w_in[:, :o_kv].astype(_BF), 'wkv': w_in[:, o_kv:o_g].astype(_BF),
        'wg': jnp.pad(w_in[:, o_g:o_c], ((0, 0), (0, LANES - 3 * N_HEADS))).astype(_BF),
        'wc': w_in[:, o_c:].astype(_BF),
        'q_gain': jnp.tile(q_norm[0], N_HEADS)[None],
        'k_gain': jnp.stack([jnp.tile(k_norm[0, 1], N_KV), jnp.tile(k_norm[0, 2], N_KV)]),
        'kc_gain': jnp.tile(k_norm[0, 0], N_KV)[None],
        'g512': jnp.asarray(np.kron(np.eye(N_HEADS, dtype=np.float32), ones), _BF),
        'g128': jnp.asarray(np.kron(np.eye(N_KV, dtype=np.float32), ones), _BF),
        'bd': bd.astype(_BF), 'pe_t': pe_t,
        'egate': jnp.asarray(egate, _BF),
        'conv_w': conv_w[0], 'out_norm': out_norm[0][None], 'w_out': w_out[0].astype(_BF),
        'norm_mlp': norm_mlp[0][None], 'w_up': w_up[0].astype(_BF), 'w_down': w_down[0].astype(_BF),
        'norm_ple': norm_ple[0][None], 'w_ple_gate': w_ple_gate[0].astype(_BF),
        'w_ple_proj': w_ple_proj[0].astype(_BF),
    }


def kernel(x_prompt, x_sample, p_prompt, p_sample, cache_cmp_kv, cache_sel_kv, state_win_kv, state_conv, page_table, rel_bias, norm_mix, w_in, q_norm, k_norm, cmp_pe, w_cmp, conv_w, out_norm, w_out, norm_mlp, w_up, w_down, norm_ple, w_ple_gate, w_ple_proj):
    bp, tp, _ = x_prompt.shape
    db, ts, _ = x_sample.shape
    assert norm_mix.shape[0] == 1 and ts == 1 and tp >= WINDOW
    n_pages = page_table.shape[1]
    past_len = n_pages * PAGE_SIZE
    wlen = state_win_kv.shape[2]
    assert wlen == WINDOW and past_len >= 4 * SEL_BLOCK
    w = _prep(norm_mix, w_in, q_norm, k_norm, cmp_pe, w_cmp, conv_w, out_norm, w_out, norm_mlp, w_up, w_down,
              norm_ple, w_ple_gate, w_ple_proj)
    w['tiles'], cbias = _prompt_tables(rel_bias, tp)
    sbias, stabs = _sample_tables(rel_bias, past_len, wlen)
    kv6 = lambda a, b, t: a.reshape(1, b, t, 2, N_KV, HEAD_DIM)

    xp = x_prompt.reshape(bp * tp, D_MODEL)
    q, cmp_p, sel_p, win_p, gates, cb, u, cmp_t, sel_t, win_t = _inproj(xp, w, tp)
    seq = lambda a: a.reshape(bp, tp, a.shape[-1])
    kv6_t = lambda a: jnp.transpose(a.reshape(bp, 2, N_KV, HEAD_DIM, a.shape[-1]), (0, 4, 1, 2, 3))[None]
    kc = _compress_prompt(seq(cmp_p), w)
    o_att = _attn_prompt(seq(q), seq(sel_p), seq(win_p), seq(gates), kc, cbias, w)
    y_p = _tail(xp, o_att.reshape(bp * tp, ATT_WIDTH), cb, u, None, p_prompt[0].reshape(bp * tp, PLE_DIM), w, tp)

    xs = x_sample.reshape(db, D_MODEL)
    q_s, cmp_s, sel_s, win_s, gates_s, cb_s, u_s = _inproj(xs, w)
    qh = q_s[:, :ATT_WIDTH].reshape(db, N_KV, GROUP, HEAD_DIM)
    zq = jnp.zeros_like(qh[:, 0])
    q8 = jnp.concatenate([jnp.concatenate([qh[:, 0], zq], axis=-1), jnp.concatenate([zq, qh[:, 1]], axis=-1)], axis=1)
    n_phys = cache_cmp_kv.shape[1]
    token_minor = lambda a: jnp.transpose(a[0], (0, 2, 3, 4, 1))
    o_c, picks = _sample_cmp(page_table, q8, token_minor(cache_cmp_kv).reshape(n_phys, 2 * KV_WIDTH, PAGE_SIZE), sbias, w)
    picks = picks[:, :N_KV, :N_PICK].reshape(db, N_KV * N_PICK)
    gates8 = jnp.pad(jnp.transpose(gates_s[:, :3 * N_HEADS].reshape(db, 3, N_HEADS), (0, 2, 1)),
                     ((0, 0), (0, 0), (0, LANES - 3)))
    o8, win_new = _sample_attn(page_table, picks, q8,
                               token_minor(cache_sel_kv).reshape(n_phys, 2 * N_KV, HEAD_DIM, PAGE_SIZE),
                               token_minor(state_win_kv).reshape(db, 2 * KV_WIDTH, wlen),
                               sel_s[:, None, :], win_s[:, None, :], gates8, o_c, stabs)
    o_att_s = o8.reshape(db, ATT_WIDTH)
    win_new = jnp.transpose(win_new.reshape(db, 2, N_KV, HEAD_DIM, wlen), (0, 4, 1, 2, 3))
    u_prev = jnp.transpose(state_conv[0], (1, 0, 2))
    y_s = _tail(xs, o_att_s, cb_s, u_s, u_prev, p_sample[0].reshape(db, PLE_DIM), w, None)

    return (y_p.reshape(bp, tp, D_MODEL), y_s.reshape(db, 1, D_MODEL),
            kv6_t(cmp_t), kv6_t(sel_t), kv6_t(win_t),
            seq(u)[:, tp - 2:][None],
            kv6(cmp_s, db, 1), kv6(sel_s, db, 1), win_new[None],
            jnp.concatenate([state_conv[0][:, 1:], u_s[:, None, :]], axis=1)[None])
```

```python
import functools
import math

import jax
import jax.numpy as jnp
import numpy as np
from jax import lax
from jax.experimental import pallas as pl
from jax.experimental.pallas import tpu as pltpu

D_MODEL = 1024
HEAD_DIM = 64
N_HEADS = 8
N_KV = 2
GROUP = N_HEADS // N_KV
ATT_WIDTH = N_HEADS * HEAD_DIM
KV_WIDTH = N_KV * HEAD_DIM
CONV_DIM = D_MODEL - ATT_WIDTH
PAGE_SIZE = 128
CMP_BLOCK = 32
SEL_BLOCK = 64
TOP_N = 8
WINDOW = 512
RP_BUCKETS = 32
RP_MAX_DIST = 128
D_FF = 4 * D_MODEL
PLE_DIM = 256
SCALE = HEAD_DIM ** -0.5
NEG = -1e30
EPS = 1e-6
LOG2E = 1.4426950408889634

LANES = 128
QBLK = 128
ROW_TILE = 512
VMEM_LIMIT = 56 * 1024 * 1024

_BF = jnp.bfloat16
_F32 = jnp.float32


def _dot(a, b):
    return jnp.dot(a, b, preferred_element_type=_F32)


def _dot_nt(a, b):
    return lax.dot_general(a, b, (((1,), (1,)), ((), ())), preferred_element_type=_F32)


def _rms_rows(x, gain):
    return x * lax.rsqrt(jnp.mean(x * x, axis=-1, keepdims=True) + EPS) * gain


def _group_rms(z, gmat, gain):
    ssq = _dot((z * z).astype(_BF), gmat) * (1.0 / HEAD_DIM)
    return z * lax.rsqrt(ssq + EPS) * gain


def _inproj_body(x_ref, nm_ref, wq_ref, wkv_ref, wg_ref, wc_ref, qg_ref, kg_ref, g512_ref, g128_ref,
                 q_ref, cmp_ref, sel_ref, win_ref, gate_ref, cb_ref, u_ref, *kv_t_refs):
    a = _rms_rows(x_ref[...], nm_ref[...]).astype(_BF)
    zq = _dot(a, wq_ref[...])
    qn = _group_rms(zq, g512_ref[...], qg_ref[...]) * SCALE
    q_ref[:, 0:ATT_WIDTH] = qn.astype(_BF)
    q_ref[:, ATT_WIDTH:] = (qn * LOG2E).astype(_BF)
    zkv = _dot(a, wkv_ref[...])
    g128 = g128_ref[...]
    halves = ((zkv[:, 0:128], zkv[:, 128:256]),
              (_group_rms(zkv[:, 256:384], g128, kg_ref[0:1, :]), zkv[:, 384:512]),
              (_group_rms(zkv[:, 512:640], g128, kg_ref[1:2, :]), zkv[:, 640:768]))
    for i, (ref, (k, v)) in enumerate(zip((cmp_ref, sel_ref, win_ref), halves)):
        if i == 0:
            ref[0] = k
            ref[1] = v
        else:
            ref[:, 0:128] = k
            ref[:, 128:256] = v
        if kv_t_refs:
            kv_t_refs[i][0, 0:128, :] = k.T
            kv_t_refs[i][0, 128:256, :] = v.T
    gate_ref[...] = jax.nn.sigmoid(_dot(a, wg_ref[...]))
    zc = _dot(a, wc_ref[...])
    cb_ref[...] = zc[:, 0:512]
    u_ref[...] = zc[:, 512:1024] * zc[:, 1024:1536]


def _inproj(x, w, seq_len=None):
    n = x.shape[0]
    tm = min(ROW_TILE, n)
    row = lambda c: pl.BlockSpec((tm, c), lambda i: (i, 0))
    full = lambda a: pl.BlockSpec(a.shape, lambda i: (0,) * a.ndim)
    consts = (w['norm_mix'], w['wq'], w['wkv'], w['wg'], w['wc'], w['q_gain'], w['k_gain'], w['g512'], w['g128'])
    out_specs = [row(2 * ATT_WIDTH), pl.BlockSpec((2, tm, KV_WIDTH), lambda i: (0, i, 0)), row(256), row(256),
                 row(128), row(512), row(512)]
    out_shape = ([jax.ShapeDtypeStruct((n, 2 * ATT_WIDTH), _BF), jax.ShapeDtypeStruct((2, n, KV_WIDTH), _F32)]
                 + [jax.ShapeDtypeStruct((n, 256), _F32)] * 2
                 + [jax.ShapeDtypeStruct((n, 128), _F32)] + [jax.ShapeDtypeStruct((n, 512), _F32)] * 2)
    if seq_len is not None:
        nt = seq_len // tm
        lead = nt - WINDOW // tm
        kv_t = pl.BlockSpec((1, 256, tm), lambda i: (i // nt, 0, i % nt))
        win_t = pl.BlockSpec((1, 256, tm), lambda i: (i // nt, 0, jnp.maximum(i % nt - lead, 0)))
        out_specs += [kv_t, kv_t, win_t]
        out_shape += [jax.ShapeDtypeStruct((n // seq_len, 256, seq_len), _F32)] * 2 + [
            jax.ShapeDtypeStruct((n // seq_len, 256, WINDOW), _F32)]
    return pl.pallas_call(
        _inproj_body,
        grid=(n // tm,),
        in_specs=[row(D_MODEL)] + [full(c) for c in consts],
        out_specs=out_specs,
        out_shape=out_shape,
        compiler_params=pltpu.CompilerParams(dimension_semantics=("arbitrary",), vmem_limit_bytes=VMEM_LIMIT),
        name="inproj",
    )(x, *consts)


FF_CHUNK = 1024


def _tail_body(halo, h_ref, o_ref, cb_ref, u_ref, up_ref, p_ref, cw_ref, on_ref, nmlp_ref, nple_ref,
               wout_hbm, wup_hbm, wdn_hbm, wgate_hbm, wproj_hbm, y_ref,
               uext_ref, wout_ref, wup_ref, wdn_ref, wgate_ref, wproj_ref, wsem):
    @pl.when(pl.program_id(0) == 0)
    def _load_weights():
        copies = [pltpu.make_async_copy(src, dst, wsem.at[i]) for i, (src, dst) in enumerate(
            ((wout_hbm, wout_ref), (wup_hbm, wup_ref), (wdn_hbm, wdn_ref), (wgate_hbm, wgate_ref),
             (wproj_hbm, wproj_ref)))]
        for c in copies:
            c.start()
        for c in copies:
            c.wait()

    tm = h_ref.shape[0]
    u = u_ref[...]
    if halo:
        first = (pl.program_id(0) % halo) == 0
        prev = jnp.where(first, 0.0, up_ref[...])
        uext_ref[0:8, :] = prev
        uext_ref[8:tm + 8, :] = u
        u2 = uext_ref[6:tm + 6, :]
        u1 = uext_ref[7:tm + 7, :]
    else:
        u2 = up_ref[0]
        u1 = up_ref[1]
    yc = cw_ref[0:1, :] * u2 + cw_ref[1:2, :] * u1 + cw_ref[2:3, :] * u
    mix_a = _rms_rows(o_ref[...], on_ref[:, 0:ATT_WIDTH]).astype(_BF)
    mix_c = _rms_rows(cb_ref[...] * yc, on_ref[:, ATT_WIDTH:]).astype(_BF)
    h = h_ref[...] + _dot(mix_a, wout_ref[0:ATT_WIDTH, :]) + _dot(mix_c, wout_ref[ATT_WIDTH:, :])
    a = _rms_rows(h, nmlp_ref[...]).astype(_BF)
    y_ref[...] = h
    for c in range(D_FF // FF_CHUNK):
        t = jnp.maximum(_dot(a, wup_ref[:, c * FF_CHUNK:(c + 1) * FF_CHUNK]), 0.0)
        y_ref[...] += _dot((t * t).astype(_BF), wdn_ref[c * FF_CHUNK:(c + 1) * FF_CHUNK, :])
    h = y_ref[...]
    a = _rms_rows(h, nple_ref[...]).astype(_BF)
    gate = jax.nn.sigmoid(_dot(a, wgate_ref[...]))
    y_ref[...] = h + gate * _dot(p_ref[...].astype(_BF), wproj_ref[...])


def _tail(h, o_att, cb, u, u_prev, p, w, seq_len):
    n = h.shape[0]
    tm = min(ROW_TILE, n)
    row = lambda c: pl.BlockSpec((tm, c), lambda i: (i, 0))
    const = lambda a: pl.BlockSpec(a.shape, lambda i: (0,) * a.ndim)
    if seq_len is not None:
        halo = seq_len // tm
        up_spec = pl.BlockSpec((8, CONV_DIM), lambda i: (jnp.maximum(i * (tm // 8) - 1, 0), 0))
        up = u
    else:
        halo = 0
        up_spec = pl.BlockSpec((2, tm, CONV_DIM), lambda i: (0, i, 0))
        up = u_prev
    consts = (w['conv_w'], w['out_norm'], w['norm_mlp'], w['norm_ple'])
    mats = (w['w_out'], w['w_up'], w['w_down'], w['w_ple_gate'], w['w_ple_proj'])
    return pl.pallas_call(
        functools.partial(_tail_body, halo),
        grid=(n // tm,),
        in_specs=[row(D_MODEL), row(ATT_WIDTH), row(CONV_DIM), row(CONV_DIM), up_spec, row(PLE_DIM)]
        + [const(c) for c in consts] + [pl.BlockSpec(memory_space=pl.ANY)] * len(mats),
        out_specs=row(D_MODEL),
        out_shape=jax.ShapeDtypeStruct((n, D_MODEL), _F32),
        scratch_shapes=[pltpu.VMEM((tm + 8, CONV_DIM), _F32)] + [pltpu.VMEM(m.shape, _BF) for m in mats]
        + [pltpu.SemaphoreType.DMA((len(mats),))],
        compiler_params=pltpu.CompilerParams(dimension_semantics=("arbitrary",), vmem_limit_bytes=VMEM_LIMIT),
        name="tail",
    )(h, o_att, cb, u, up, p, *consts, *mats)


def _split_heads(x, lane_lo):
    xr = pltpu.roll(x, HEAD_DIM, axis=1)
    zero = jnp.zeros_like(x)
    a = (jnp.where(lane_lo, x, zero), jnp.where(lane_lo, xr, zero))
    b = (jnp.where(lane_lo, zero, xr), jnp.where(lane_lo, zero, x))
    return a, b


def _compress_rows(load, ns, bd_ref, pe_ref):
    acc = [jnp.zeros((2 * ns, LANES), _F32), jnp.zeros((2 * ns, LANES), _F32)]
    for l in range(CMP_BLOCK):
        for plane in range(2):
            x = jnp.concatenate([load(plane, l), load(plane, CMP_BLOCK + l)], axis=0)
            x = x + pe_ref[l, plane:plane + 1, :]
            acc[plane] = acc[plane] + _dot(x.astype(_BF), bd_ref[l, plane])
    return acc


def _compress_prompt_body(slab_ref, bd_ref, pe_ref, kg_ref, g128_ref, kA_ref, kB_ref, vA_ref, vB_ref):
    ns = slab_ref.shape[2] // SEL_BLOCK
    k, v = _compress_rows(lambda plane, t0: slab_ref[plane, 0, pl.ds(t0, ns, stride=SEL_BLOCK), :],
                          ns, bd_ref, pe_ref)
    k = _group_rms(k, g128_ref[...], kg_ref[...])
    lane_lo = lax.broadcasted_iota(jnp.int32, k.shape, 1) < HEAD_DIM
    ka, kb = _split_heads(k, lane_lo)
    va, vb = _split_heads(v, lane_lo)
    pad = jnp.zeros((LANES - 2 * ns, LANES), _BF)
    for g in range(N_KV):
        for ref, val in ((kA_ref, ka[g]), (kB_ref, kb[g]), (vA_ref, va[g]), (vB_ref, vb[g])):
            ref[0, g, 0:2 * ns, :] = val.astype(_BF)
            if 2 * ns < LANES:
                ref[0, g, 2 * ns:, :] = pad


def _compress_prompt(slab, w):
    _, b, t, _ = slab.shape
    assert t % QBLK == 0 and t // CMP_BLOCK <= LANES
    full = lambda a: pl.BlockSpec(a.shape, lambda i: (0,) * a.ndim)
    consts = (w['bd'], w['pe_t'], w['kc_gain'], w['g128'])
    out = jax.ShapeDtypeStruct((b, N_KV, LANES, LANES), _BF)
    return pl.pallas_call(
        _compress_prompt_body,
        grid=(b,),
        in_specs=[pl.BlockSpec((2, 1, t, LANES), lambda i: (0, i, 0, 0))] + [full(c) for c in consts],
        out_specs=[pl.BlockSpec((1, N_KV, LANES, LANES), lambda i: (i, 0, 0, 0))] * 4,
        out_shape=[out] * 4,
        compiler_params=pltpu.CompilerParams(dimension_semantics=("arbitrary",), vmem_limit_bytes=VMEM_LIMIT),
        name="compress_prompt",
    )(slab, *consts)


BUILD_ROWS = 256
QSTEP = 2


def _top_extra(impb, cand, lane_f, n_extra):
    v = jnp.where(cand, impb, -1.0)
    picked = jnp.zeros(impb.shape, _F32)
    for _ in range(n_extra):
        mx = jnp.max(v, axis=-1, keepdims=True)
        first = jnp.min(jnp.where(v == mx, lane_f, 1e9), axis=-1, keepdims=True)
        hit = lane_f == first
        picked = jnp.where(hit, 1.0, picked)
        v = jnp.where(hit, -1.0, v)
    return picked


def _attn_prompt_body(q_ref, ks_ref, kw_ref, gate_ref, kcA_ref, kcB_ref, vcA_ref, vcB_ref, cbias_ref, tiles_ref,
                      eg_ref, o_ref, ksA, ksB, vsA, vsB, kwA, kwB, vwA, vwB, s_scr, m_scr, acc_scr):
    step = pl.program_id(1)
    t_len = ks_ref.shape[1]
    n_sel = t_len // SEL_BLOCK

    @pl.when(step == 0)
    def _build():
        def chunk(c, carry):
            r0 = pl.multiple_of(c * BUILD_ROWS, BUILD_ROWS)
            rows = pl.ds(r0, BUILD_ROWS)
            lane = lax.broadcasted_iota(jnp.int32, (BUILD_ROWS, LANES), 1)
            blk = (r0 + lax.broadcasted_iota(jnp.int32, (BUILD_ROWS, LANES), 0)) // SEL_BLOCK
            lane_lo = lane < HEAD_DIM
            oh_hi = jnp.where(lane == blk + HEAD_DIM, 1.0, 0.0)
            oh_lo = jnp.where(lane == blk, 1.0, 0.0)
            one_hi = jnp.where(lane == HEAD_DIM, 1.0, 0.0)
            one_lo = jnp.where(lane == 0, 1.0, 0.0)
            for src, k_a, k_b, v_a, v_b, onehot in ((ks_ref, ksA, ksB, vsA, vsB, True),
                                                    (kw_ref, kwA, kwB, vwA, vwB, False)):
                ka, kb = _split_heads(src[0, rows, 0:128], lane_lo)
                va, vb = _split_heads(src[0, rows, 128:256], lane_lo)
                for g in range(N_KV):
                    if onehot:
                        k_a[g, rows, :] = jnp.where(lane_lo, ka[g], oh_hi).astype(_BF)
                        k_b[g, rows, :] = jnp.where(lane_lo, oh_lo, kb[g]).astype(_BF)
                    else:
                        k_a[g, rows, :] = ka[g].astype(_BF)
                        k_b[g, rows, :] = kb[g].astype(_BF)
                    v_a[g, rows, :] = jnp.where(lane_lo, va[g], one_hi).astype(_BF)
                    v_b[g, rows, :] = jnp.where(lane_lo, one_lo, vb[g]).astype(_BF)
            return carry
        lax.fori_loop(0, t_len // BUILD_ROWS, chunk, 0)

    lane2 = lax.broadcasted_iota(jnp.int32, (2 * QBLK, LANES), 1)
    lo2 = lane2 < HEAD_DIM
    lane1 = lax.broadcasted_iota(jnp.int32, (QBLK, LANES), 1)
    lane1_f = lane1.astype(_F32)
    row1 = lax.broadcasted_iota(jnp.int32, (QBLK, LANES), 0)
    halves = range(QSTEP)
    qbs = [step * QSTEP + h for h in halves]
    qb_last = qbs[-1]
    rows_of = [slice(h * QBLK, (h + 1) * QBLK) for h in halves]
    curs = [(qb * QBLK + row1) // SEL_BLOCK for qb in qbs]

    gates = gate_ref[0]
    g_hi = gates.astype(_BF)
    g_lo = (gates - g_hi.astype(_F32)).astype(_BF)
    gexp = _dot(g_hi, eg_ref[...]) + _dot(g_lo, eg_ref[...])

    def normalize(acc, x):
        l = jnp.sum(jnp.where(lane2 == (HEAD_DIM if x == 0 else 0), acc, 0.0), axis=-1, keepdims=True)
        keep = lo2 if x == 0 else jnp.logical_not(lo2)
        return jnp.where(keep, acc / l, 0.0)

    chains = [(h, g, x) for h in halves for g in range(N_KV) for x in range(2)]
    o_cmp, q_plain, q_pair, importance = [], [], [], []
    for h in halves:
        for g in range(N_KV):
            qs = jnp.concatenate([q_ref[0, rows_of[h], (2 * g) * LANES:(2 * g + 1) * LANES],
                                  q_ref[0, rows_of[h], (2 * g + 1) * LANES:(2 * g + 2) * LANES]], axis=0)
            zero = jnp.zeros_like(qs)
            q_a = jnp.where(lo2, qs, zero)
            q_b = jnp.where(lo2, zero, qs)
            qs2 = jnp.concatenate([q_ref[0, rows_of[h], ATT_WIDTH + (2 * g) * LANES:ATT_WIDTH + (2 * g + 1) * LANES],
                                   q_ref[0, rows_of[h], ATT_WIDTH + (2 * g + 1) * LANES:ATT_WIDTH + (2 * g + 2) * LANES]],
                                  axis=0)
            q_pair.append(qs2)
            q_plain += [jnp.where(lo2, qs2, zero), jnp.where(lo2, zero, qs2)]

            def cmp_probs(qx, k_ref, x):
                s = _dot_nt(qx, k_ref[0, g]) + cbias_ref[h, g, x]
                m = jnp.max(s, axis=-1, keepdims=True)
                e = jnp.where(s > 0.5 * NEG, jnp.exp(s - m), 0.0)
                l = jnp.sum(e, axis=-1, keepdims=True)
                return e / jnp.where(l > 0.0, l, 1.0)
            p_a = cmp_probs(q_a, kcA_ref, 0)
            p_b = cmp_probs(q_b, kcB_ref, 1)
            o_cmp.append(_dot(p_a.astype(_BF), vcA_ref[0, g]) + _dot(p_b.astype(_BF), vcB_ref[0, g]))
            imp = p_a[0:QBLK] + p_a[QBLK:] + p_b[0:QBLK] + p_b[QBLK:]
            importance.append(imp + pltpu.roll(imp, LANES - n_sel, axis=1))

    two = lambda a: jnp.concatenate([a, a], axis=0)
    cands = jnp.concatenate([two((lane1 >= 1) & (lane1 <= curs[h] - 2)) for h in halves], axis=0)
    picked = _top_extra(jnp.concatenate(importance, axis=0), cands,
                        jnp.concatenate([lane1_f] * (2 * QSTEP), axis=0), TOP_N - 3)
    q_aug = []
    for h in halves:
        cur = curs[h]
        forced = (lane1 == 0) | (lane1 == cur) | (lane1 == cur - 1)
        few = cur <= TOP_N - 1
        for g in range(N_KV):
            i = h * N_KV + g
            chosen = forced | (few & (lane1 <= cur)) | ((picked[i * QBLK:(i + 1) * QBLK] > 0.5) & jnp.logical_not(few))
            sb_lo = jnp.where(chosen | (lane1 >= n_sel), 0.0, NEG)
            sb_hi = pltpu.roll(sb_lo, HEAD_DIM, axis=1)
            q_aug += [jnp.where(lo2, q_pair[i], two(sb_hi).astype(_BF)),
                      jnp.where(lo2, two(sb_lo).astype(_BF), q_pair[i])]

    k_sel, v_sel, k_win, v_win = (ksA, ksB), (vsA, vsB), (kwA, kwB), (vwA, vwB)
    n_win = WINDOW // QBLK + 1

    o_win = []
    for ch, (h, g, x) in enumerate(chains):
        w_first = jnp.maximum(qbs[h] - (n_win - 1), 0)
        w_rows = pl.ds(pl.multiple_of(w_first * QBLK, QBLK), n_win * QBLK)
        s = _dot_nt(q_plain[ch], k_win[x][g, w_rows, :])
        parts = []
        for i in range(n_win):
            idx = qbs[h] - w_first - i
            parts.append(s[:, i * QBLK:(i + 1) * QBLK] + tiles_ref[jnp.where(idx < 0, n_win, idx), g, x])
        m = parts[0]
        for part in parts[1:]:
            m = jnp.maximum(m, part)
        m = jnp.max(m, axis=-1, keepdims=True)
        p = jnp.concatenate([jnp.exp2(part - m).astype(_BF) for part in parts], axis=1)
        o_win.append(normalize(_dot(p, v_win[x][g, w_rows, :]), x))

    n_quart = 4
    qw = t_len // n_quart
    per_q = qw // QBLK
    m_scr[...] = jnp.full(m_scr.shape, NEG, _F32)
    for qi in range(n_quart):
        @pl.when(qb_last * QBLK >= qi * qw)
        def _scores(qi=qi):
            for ch, (h, g, x) in enumerate(chains):
                s = _dot_nt(q_aug[ch], k_sel[x][g, qi * qw:(qi + 1) * qw, :])
                m = m_scr[ch]
                for ci in range(per_q):
                    c = qi * per_q + ci
                    idx = qbs[h] - c
                    sb = s[:, ci * QBLK:(ci + 1) * QBLK] + tiles_ref[jnp.where(idx < 0, n_win, jnp.minimum(idx, 2)), g, x]
                    s_scr[ch, :, c * QBLK:(c + 1) * QBLK] = sb
                    m = jnp.maximum(m, sb)
                m_scr[ch] = m
    m_sel = [jnp.max(m_scr[ch], axis=-1, keepdims=True) for ch in range(len(chains))]

    acc_scr[...] = jnp.zeros(acc_scr.shape, _F32)
    for qi in range(n_quart):
        @pl.when(qb_last * QBLK >= qi * qw)
        def _weighted(qi=qi):
            for ch, (h, g, x) in enumerate(chains):
                p = jnp.exp2(s_scr[ch, :, qi * qw:(qi + 1) * qw] - m_sel[ch]).astype(_BF)
                acc_scr[ch] += _dot(p, v_sel[x][g, qi * qw:(qi + 1) * qw, :])

    for h in halves:
        for g in range(N_KV):
            i = h * N_KV + g
            o_c = o_cmp[i]
            o_s = normalize(acc_scr[2 * i], 0) + normalize(acc_scr[2 * i + 1], 1)
            o_w = o_win[2 * i] + o_win[2 * i + 1]
            for pr in range(2):
                rows = slice(pr * QBLK, (pr + 1) * QBLK)
                col = (2 * g + pr) * LANES
                gx = lambda br: gexp[rows_of[h], br * ATT_WIDTH + col:br * ATT_WIDTH + col + LANES]
                o_ref[0, rows_of[h], col:col + LANES] = gx(0) * o_c[rows] + gx(1) * o_s[rows] + gx(2) * o_w[rows]


def _attn_prompt(q, sel, win, gates, kc, cbias, w):
    b, t, _ = sel.shape
    nq = t // QBLK
    assert t // SEL_BLOCK <= CMP_BLOCK and t % (4 * QBLK) == 0 and t >= WINDOW + QBLK and nq % QSTEP == 0
    kcA, kcB, vcA, vcB = kc
    full = lambda a: pl.BlockSpec(a.shape, lambda i, j: (0,) * a.ndim)
    slab = pl.BlockSpec((1, t, 256), lambda i, j: (i, 0, 0))
    kcs = pl.BlockSpec((1, N_KV, LANES, LANES), lambda i, j: (i, 0, 0, 0))
    n_chain = QSTEP * 2 * N_KV
    q_rows = QSTEP * QBLK
    scratch = [pltpu.VMEM((N_KV, t, LANES), _BF)] * 8 + [
        pltpu.VMEM((n_chain, 2 * QBLK, t), _F32),
        pltpu.VMEM((n_chain, 2 * QBLK, LANES), _F32), pltpu.VMEM((n_chain, 2 * QBLK, LANES), _F32)]
    return pl.pallas_call(
        _attn_prompt_body,
        grid=(b, nq // QSTEP),
        in_specs=[pl.BlockSpec((1, q_rows, 2 * ATT_WIDTH), lambda i, j: (i, j, 0)), slab, slab,
                  pl.BlockSpec((1, q_rows, LANES), lambda i, j: (i, j, 0)), kcs, kcs, kcs, kcs,
                  pl.BlockSpec((QSTEP, N_KV, 2, 2 * QBLK, LANES), lambda i, j: (j, 0, 0, 0, 0)),
                  full(w['tiles']), full(w['egate'])],
        out_specs=pl.BlockSpec((1, q_rows, ATT_WIDTH), lambda i, j: (i, j, 0)),
        out_shape=jax.ShapeDtypeStruct((b, t, ATT_WIDTH), _F32),
        scratch_shapes=scratch,
        compiler_params=pltpu.CompilerParams(dimension_semantics=("arbitrary", "arbitrary"),
                                             vmem_limit_bytes=VMEM_LIMIT),
        name="attn_prompt",
    )(q, sel, win, gates, kcA, kcB, vcA, vcB, cbias, w['tiles'], w['egate'])


N_PICK = TOP_N - 3
N_SLOT = TOP_N - 1
PAGE_ROWS = 2 * KV_WIDTH


PAIR_TOKENS = 2 * PAGE_SIZE


def _sample_cmp_body(pt_ref, q_ref, cache_ref, bd_ref, pet_ref, perm_ref, kg_ref, g128_ref, sb_ref, oc_ref, pick_ref,
                     buf, rows_scr, sem):
    b = pl.program_id(0)
    nb = pl.num_programs(0)
    n_pairs = pt_ref.shape[1] // 2
    nc = 8 * n_pairs
    slot = b % 2

    def page_copy(bb, pair, half, sl):
        return pltpu.make_async_copy(cache_ref.at[pt_ref[bb, 2 * pair + half]],
                                     buf.at[sl, pair, :, half * PAGE_SIZE:(half + 1) * PAGE_SIZE], sem.at[sl])

    def for_pages(fn):
        def step(pair, c):
            fn(pair, 0)
            fn(pair, 1)
            return c
        lax.fori_loop(0, n_pairs, step, 0)

    @pl.when(b == 0)
    def _prime():
        for_pages(lambda pair, half: page_copy(0, pair, half, 0).start())

    @pl.when(b + 1 < nb)
    def _prefetch():
        for_pages(lambda pair, half: page_copy(b + 1, pair, half, 1 - slot).start())

    for_pages(lambda pair, half: page_copy(b, pair, half, slot).wait())

    def regroup(pair, c):
        x = (buf[slot, pair] + pet_ref[...]).astype(_BF)
        rows_scr[pair] = _dot_nt(perm_ref[...], x)
        return c
    lax.fori_loop(0, n_pairs, regroup, 0, unroll=4)

    acc = jnp.zeros((nc, 2 * KV_WIDTH), _F32)
    for l in range(CMP_BLOCK):
        x = rows_scr[:, 8 * l:8 * (l + 1), :].reshape(nc, 2 * KV_WIDTH)
        acc = acc + _dot(x.astype(_BF), bd_ref[l])
    k = _group_rms(acc[:, 0:KV_WIDTH], g128_ref[...], kg_ref[...]).astype(_BF)
    v = acc[:, KV_WIDTH:].astype(_BF)
    s = _dot_nt(q_ref[0], k) + sb_ref[...]
    m = jnp.max(s, axis=-1, keepdims=True)
    e = jnp.exp(s - m)
    p = e / jnp.sum(e, axis=-1, keepdims=True)
    oc_ref[0] = _dot(p.astype(_BF), v)
    imp8 = p + pltpu.roll(p, nc - 1, axis=1)
    rows = [imp8[h:h + 1] for h in range(N_HEADS)]
    imp_g = [rows[g * GROUP] + rows[g * GROUP + 1] + rows[g * GROUP + 2] + rows[g * GROUP + 3] for g in range(N_KV)]
    imp = jnp.concatenate(imp_g + [jnp.full((8 - N_KV, nc), -1.0, _F32)], axis=0)
    lane = lax.broadcasted_iota(jnp.int32, (8, nc), 1)
    lane_f = lane.astype(_F32)
    v_c = jnp.where((lane % 2 == 0) & (lane >= 2) & (lane <= nc - 4), imp, -1.0)
    picks = jnp.zeros((8, nc), _F32)
    for i in range(N_PICK):
        mx = jnp.max(v_c, axis=-1, keepdims=True)
        first = jnp.min(jnp.where(v_c == mx, lane_f, 1e9), axis=-1, keepdims=True)
        picks = jnp.where(lane == i, first * 0.5, picks)
        v_c = jnp.where(lane_f == first, -1.0, v_c)
    pick_ref[0] = picks.astype(jnp.int32)


def _sample_cmp(page_table, q8, cache, sbias, w):
    db, n_pages = page_table.shape
    nc = 4 * n_pages
    assert n_pages % 2 == 0 and nc // 2 - 2 >= N_PICK
    full = lambda a: pl.BlockSpec(a.shape, lambda i, pt: (0,) * a.ndim)
    consts = (w['bd_kv'], w['pe_tok'], w['perm'], w['kc_gain'], w['g128'], sbias)
    grid_spec = pltpu.PrefetchScalarGridSpec(
        num_scalar_prefetch=1,
        grid=(db,),
        in_specs=[pl.BlockSpec((1, 8, LANES), lambda i, pt: (i, 0, 0)), pl.BlockSpec(memory_space=pl.ANY)]
        + [full(c) for c in consts],
        out_specs=[pl.BlockSpec((1, 8, LANES), lambda i, pt: (i, 0, 0)),
                   pl.BlockSpec((1, 8, nc), lambda i, pt: (i, 0, 0))],
        scratch_shapes=[pltpu.VMEM((2, n_pages // 2, 2 * KV_WIDTH, PAIR_TOKENS), _F32),
                        pltpu.VMEM((n_pages // 2, PAIR_TOKENS, 2 * KV_WIDTH), _F32),
                        pltpu.SemaphoreType.DMA((2,))],
    )
    return pl.pallas_call(
        _sample_cmp_body,
        grid_spec=grid_spec,
        out_shape=[jax.ShapeDtypeStruct((db, 8, LANES), _F32), jax.ShapeDtypeStruct((db, 8, nc), jnp.int32)],
        compiler_params=pltpu.CompilerParams(dimension_semantics=("arbitrary",), vmem_limit_bytes=VMEM_LIMIT),
        name="sample_cmp",
    )(page_table, q8, cache, *consts)


def _sample_attn_body(pt_ref, pick_ref, q_ref, cache_ref, win_ref, nsel_ref, nwin_ref, ncol_ref, gate_ref, oc_ref,
                      bsel_ref, b0_ref, bwin_ref, o_ref, wout_ref, kb, sem):
    b = pl.program_id(0)
    nb = pl.num_programs(0)
    nj = 2 * pt_ref.shape[1]
    wlen = win_ref.shape[2]
    slot = b % 2

    def block_of(bb, g, s):
        if s == 0:
            return 0
        if s == N_SLOT - 1:
            return nj - 1
        return pick_ref[bb, g * N_PICK + (s - 1)]

    def tile_copy(bb, g, s, plane, sl):
        page = pt_ref[bb, block_of(bb, g, s) // 2]
        return pltpu.make_async_copy(cache_ref.at[page, plane * N_KV + g],
                                     kb.at[sl, g, plane, :, pl.ds(s * PAGE_SIZE, PAGE_SIZE)], sem.at[sl])

    def for_tiles(fn):
        for g in range(N_KV):
            for s in range(N_SLOT):
                for plane in range(2):
                    fn(g, s, plane)

    @pl.when(b == 0)
    def _prime():
        for_tiles(lambda g, s, plane: tile_copy(0, g, s, plane, 0).start())

    @pl.when(b + 1 < nb)
    def _prefetch():
        for_tiles(lambda g, s, plane: tile_copy(b + 1, g, s, plane, 1 - slot).start())

    q8 = q_ref[0]
    q8f = q8.astype(_F32)
    head_g = lax.broadcasted_iota(jnp.int32, (8, HEAD_DIM), 0) // GROUP
    own_half = lambda x: jnp.where(head_g == 0, x[:, 0:HEAD_DIM], x[:, HEAD_DIM:])
    b0 = b0_ref[:, 0:1]

    x = win_ref[0]
    lane_w = lax.broadcasted_iota(jnp.int32, x.shape, 1)
    wout_ref[0] = jnp.where(lane_w == wlen - 1, ncol_ref[0], pltpu.roll(x, wlen - 1, axis=1))
    sw = _dot(q8, x[0:KV_WIDTH, :].astype(_BF)) + bwin_ref[...]
    sw_new = jnp.sum(q8f * nwin_ref[0][:, 0:KV_WIDTH], axis=-1, keepdims=True) + b0
    mw = jnp.maximum(jnp.max(sw, axis=-1, keepdims=True), sw_new)
    ew = jnp.exp(sw - mw)
    ew_new = jnp.exp(sw_new - mw)
    o_w = _dot_nt(ew.astype(_BF), x[KV_WIDTH:, :].astype(_BF)) + ew_new * nwin_ref[0][:, KV_WIDTH:]
    o_w = own_half(o_w / (jnp.sum(ew, axis=-1, keepdims=True) + ew_new))

    for_tiles(lambda g, s, plane: tile_copy(b, g, s, plane, slot).wait())

    ss_new = jnp.sum(q8f * nsel_ref[0][:, 0:KV_WIDTH], axis=-1, keepdims=True) + b0
    lane_half = lax.broadcasted_iota(jnp.int32, (8, PAGE_SIZE), 1) // SEL_BLOCK
    o_sel = []
    for g in range(N_KV):
        pieces = []
        for s in range(N_SLOT):
            j = jnp.full((8, PAGE_SIZE), block_of(b, g, s), jnp.int32)
            if s == 0:
                tab = bsel_ref[0]
            elif s == N_SLOT - 1:
                tab = bsel_ref[2]
            else:
                tab = jnp.where(j == nj - 2, bsel_ref[1], bsel_ref[0])
            pieces.append(jnp.where(lane_half == j % 2, tab, NEG))
        ss = _dot(q8[:, g * HEAD_DIM:(g + 1) * HEAD_DIM], kb[slot, g, 0].astype(_BF)) + jnp.concatenate(pieces, axis=1)
        ms = jnp.maximum(jnp.max(ss, axis=-1, keepdims=True), ss_new)
        es = jnp.exp(ss - ms)
        es_new = jnp.exp(ss_new - ms)
        v_new = nsel_ref[0][:, KV_WIDTH + g * HEAD_DIM:KV_WIDTH + (g + 1) * HEAD_DIM]
        o_g = _dot_nt(es.astype(_BF), kb[slot, g, 1].astype(_BF)) + es_new * v_new
        o_sel.append(o_g / (jnp.sum(es, axis=-1, keepdims=True) + es_new))
    o_s = jnp.where(head_g == 0, o_sel[0], o_sel[1])
    gates = gate_ref[0]
    o_ref[0] = gates[:, 0:1] * own_half(oc_ref[0]) + gates[:, 1:2] * o_s + gates[:, 2:3] * o_w


def _sample_attn(page_table, picks, q8, cache, win, new_sel, new_win, gates8, o_c, tabs):
    db, n_pages = page_table.shape
    wlen = win.shape[2]
    full = lambda a: pl.BlockSpec(a.shape, lambda i, pt, pk: (0,) * a.ndim)
    per_b = lambda r, c: pl.BlockSpec((1, r, c), lambda i, pt, pk: (i, 0, 0))
    bsel, b0, bwin = tabs
    grid_spec = pltpu.PrefetchScalarGridSpec(
        num_scalar_prefetch=2,
        grid=(db,),
        in_specs=[per_b(8, LANES), pl.BlockSpec(memory_space=pl.ANY), per_b(2 * KV_WIDTH, wlen), per_b(1, 256),
                  per_b(1, 256), per_b(2 * KV_WIDTH, 1), per_b(8, LANES), per_b(8, LANES),
                  full(bsel), full(b0), full(bwin)],
        out_specs=[per_b(8, HEAD_DIM), per_b(2 * KV_WIDTH, wlen)],
        scratch_shapes=[pltpu.VMEM((2, N_KV, 2, HEAD_DIM, N_SLOT * PAGE_SIZE), _F32), pltpu.SemaphoreType.DMA((2,))],
    )
    return pl.pallas_call(
        _sample_attn_body,
        grid_spec=grid_spec,
        out_shape=[jax.ShapeDtypeStruct((db, 8, HEAD_DIM), _F32), jax.ShapeDtypeStruct((db, 2 * KV_WIDTH, wlen), _F32)],
        compiler_params=pltpu.CompilerParams(dimension_semantics=("arbitrary",), vmem_limit_bytes=VMEM_LIMIT),
        name="sample_attn",
    )(page_table, picks, q8, cache, win, new_sel, new_win, new_win[:, 0, :, None], gates8, o_c, bsel, b0, bwin)


def _rel_bucket(dist):
    n = np.maximum(np.asarray(dist), 0)
    max_exact = RP_BUCKETS // 2
    nf = np.maximum(n, 1).astype(np.float32)
    large = max_exact + (np.log(nf / np.float32(max_exact)) / np.float32(math.log(RP_MAX_DIST / max_exact))
                         * np.float32(RP_BUCKETS - max_exact)).astype(np.int32)
    large = np.minimum(large, RP_BUCKETS - 1)
    return np.where(n < max_exact, n, large)


def _bias_of(rel_bias, dist):
    onehot = _rel_bucket(dist)[..., None, None] == np.arange(RP_BUCKETS)[:, None]
    return jnp.sum(jnp.where(onehot, rel_bias, 0.0), axis=-2)


def _head_major(x, lead):
    n = x.shape[-2]
    x = x.reshape(lead + (QBLK, n, N_KV, 2, 2))
    nl = len(lead)
    x = jnp.transpose(x, tuple(range(nl)) + (nl + 2, nl + 4, nl + 3, nl, nl + 1))
    return x.reshape(lead + (N_KV, 2, 2 * QBLK, n))


def _prompt_tables(rel_bias, t_len):
    nq = t_len // QBLK
    ns = t_len // SEL_BLOCK
    ti = np.arange(QBLK)[:, None]
    ki = np.arange(QBLK)[None, :]
    n_idx = WINDOW // QBLK + 1
    pad = QBLK - 1
    bvec = _bias_of(rel_bias, np.arange(-pad, t_len + pad))
    wins = jnp.stack([bvec[i * QBLK:i * QBLK + 2 * QBLK - 1][::-1] for i in range(n_idx)])
    skew = jnp.tile(wins, (1, QBLK + 1, 1))[:, :QBLK * 2 * QBLK].reshape(n_idx, QBLK, 2 * QBLK, N_HEADS)
    tiles = skew[:, ::-1, :QBLK]
    dist = np.stack([i * QBLK + ti - ki for i in range(n_idx)])
    valid = dist >= 0
    valid[n_idx - 1] &= dist[n_idx - 1] < WINDOW
    tiles = jnp.where(valid[..., None], tiles, NEG)
    tiles = jnp.concatenate([tiles, jnp.full_like(tiles[:1], NEG)])
    tiles = _head_major(tiles * LOG2E, (n_idx + 1,))
    shift = CMP_BLOCK * ns * 2
    bcmp = _bias_of(rel_bias, np.arange(-shift, t_len))
    cols = [bcmp[shift - (CMP_BLOCK * blk + CMP_BLOCK - 1):][:t_len]
            for blk in list(range(0, 2 * ns, 2)) + list(range(1, 2 * ns, 2))]
    cb = jnp.stack(cols, axis=1)
    cb = jnp.pad(cb, ((0, 0), (0, LANES - 2 * ns), (0, 0)))
    lane = np.arange(LANES)
    blk = np.where(lane < ns, 2 * lane, 2 * (lane - ns) + 1)
    dist_c = np.arange(t_len)[:, None] - (CMP_BLOCK * blk + CMP_BLOCK - 1)[None, :]
    valid_c = (dist_c >= 0) & (lane < 2 * ns)[None, :]
    cb = jnp.where(valid_c[..., None], cb, NEG)
    cb = _head_major(cb.reshape(nq, QBLK, LANES, N_HEADS), (nq,))
    return tiles, cb


def _sample_tables(rel_bias, past_len, wlen):
    blk = np.arange(past_len // CMP_BLOCK)
    sbias = _bias_of(rel_bias, past_len - (CMP_BLOCK * blk + CMP_BLOCK - 1)).T
    pos = np.arange(PAGE_SIZE) % SEL_BLOCK
    bsel = jnp.stack([_bias_of(rel_bias, np.full(PAGE_SIZE, past_len)).T,
                      _bias_of(rel_bias, 2 * SEL_BLOCK - pos).T,
                      _bias_of(rel_bias, SEL_BLOCK - pos).T])
    b0 = _bias_of(rel_bias, np.zeros((LANES,), np.int32)).T
    tok = np.arange(wlen)
    bwin = jnp.where((tok >= 1)[None, :], _bias_of(rel_bias, wlen - tok).T, NEG)
    return sbias, (bsel, b0, bwin)


def _prep(norm_mix, w_in, q_norm, k_norm, cmp_pe, w_cmp, conv_w, out_norm, w_out, norm_mlp, w_up, w_down,
          norm_ple, w_ple_gate, w_ple_proj):
    w_in = w_in[0]
    o_kv = ATT_WIDTH
    o_g = o_kv + 6 * KV_WIDTH
    o_c = o_g + 3 * N_HEADS
    bd = jnp.einsum('gh,plde->lpgdhe', jnp.eye(N_KV, dtype=_F32), w_cmp[0]).reshape(CMP_BLOCK, 2, LANES, LANES)
    pe_t = jnp.transpose(jnp.tile(cmp_pe[0], (1, 1, N_KV)), (1, 0, 2))
    egate = np.zeros((LANES, 3 * ATT_WIDTH), np.float32)
    for br in range(3):
        for h in range(N_HEADS):
            egate[br * N_HEADS + h, br * ATT_WIDTH + h * HEAD_DIM:br * ATT_WIDTH + (h + 1) * HEAD_DIM] = 1.0
    ones = np.ones((HEAD_DIM, HEAD_DIM), np.float32)
    perm = np.zeros((PAIR_TOKENS, PAIR_TOKENS), np.float32)
    tok = np.arange(PAIR_TOKENS)
    perm[(tok % CMP_BLOCK) * (PAIR_TOKENS // CMP_BLOCK) + tok // CMP_BLOCK, tok] = 1.0
    pe_tok = jnp.tile(jnp.transpose(cmp_pe[0], (0, 2, 1)), (1, N_KV, PAIR_TOKENS // CMP_BLOCK))
    pe_tok = pe_tok.reshape(2 * KV_WIDTH, PAIR_TOKENS)
    zero = jnp.zeros_like(bd[:, 0])
    bd_kv = jnp.concatenate([jnp.concatenate([bd[:, 0], zero], axis=2),
                             jnp.concatenate([zero, bd[:, 1]], axis=2)], axis=1)
    return {
        'perm': jnp.asarray(perm, _BF), 'pe_tok': pe_tok, 'bd_kv': bd_kv.astype(_BF),
        'norm_mix': norm_mix[0][None], 'wq': w_in[:, :o_kv].astype(_BF), 'wkv': w_in[:, o_kv:o_g].astype(_BF),
        'wg': jnp.pad(w_in[:, o_g:o_c], ((0, 0), (0, LANES - 3 * N_HEADS))).astype(_BF),
        'wc': w_in[:, o_c:].astype(_BF),
        'q_gain': jnp.tile(q_norm[0], N_HEADS)[None],
        'k_gain': jnp.stack([jnp.tile(k_norm[0, 1], N_KV), jnp.tile(k_norm[0, 2], N_KV)]),
        'kc_gain': jnp.tile(k_norm[0, 0], N_KV)[None],
        'g512': jnp.asarray(np.kron(np.eye(N_HEADS, dtype=np.float32), ones), _BF),
        'g128': jnp.asarray(np.kron(np.eye(N_KV, dtype=np.float32), ones), _BF),
        'bd': bd.astype(_BF), 'pe_t': pe_t,
        'egate': jnp.asarray(egate, _BF),
        'conv_w': conv_w[0], 'out_norm': out_norm[0][None], 'w_out': w_out[0].astype(_BF),
        'norm_mlp': norm_mlp[0][None], 'w_up': w_up[0].astype(_BF), 'w_down': w_down[0].astype(_BF),
        'norm_ple': norm_ple[0][None], 'w_ple_gate': w_ple_gate[0].astype(_BF),
        'w_ple_proj': w_ple_proj[0].astype(_BF),
    }


def kernel(x_prompt, x_sample, p_prompt, p_sample, cache_cmp_kv, cache_sel_kv, state_win_kv, state_conv, page_table, rel_bias, norm_mix, w_in, q_norm, k_norm, cmp_pe, w_cmp, conv_w, out_norm, w_out, norm_mlp, w_up, w_down, norm_ple, w_ple_gate, w_ple_proj):
    bp, tp, _ = x_prompt.shape
    db, ts, _ = x_sample.shape
    assert norm_mix.shape[0] == 1 and ts == 1 and tp >= WINDOW
    n_pages = page_table.shape[1]
    past_len = n_pages * PAGE_SIZE
    wlen = state_win_kv.shape[2]
    assert wlen == WINDOW and past_len >= 4 * SEL_BLOCK
    w = _prep(norm_mix, w_in, q_norm, k_norm, cmp_pe, w_cmp, conv_w, out_norm, w_out, norm_mlp, w_up, w_down,
              norm_ple, w_ple_gate, w_ple_proj)
    w['tiles'], cbias = _prompt_tables(rel_bias, tp)
    sbias, stabs = _sample_tables(rel_bias, past_len, wlen)
    kv6 = lambda a, b, t: a.reshape(1, b, t, 2, N_KV, HEAD_DIM)

    xp = x_prompt.reshape(bp * tp, D_MODEL)
    q, cmp_p, sel_p, win_p, gates, cb, u, cmp_t, sel_t, win_t = _inproj(xp, w, tp)
    seq = lambda a: a.reshape(bp, tp, a.shape[-1])
    kv6_t = lambda a: jnp.transpose(a.reshape(bp, 2, N_KV, HEAD_DIM, a.shape[-1]), (0, 4, 1, 2, 3))[None]
    kc = _compress_prompt(cmp_p.reshape(2, bp, tp, KV_WIDTH), w)
    o_att = _attn_prompt(seq(q), seq(sel_p), seq(win_p), seq(gates), kc, cbias, w)
    y_p = _tail(xp, o_att.reshape(bp * tp, ATT_WIDTH), cb, u, None, p_prompt[0].reshape(bp * tp, PLE_DIM), w, tp)

    xs = x_sample.reshape(db, D_MODEL)
    q_s, cmp_s, sel_s, win_s, gates_s, cb_s, u_s = _inproj(xs, w)
    qh = q_s[:, :ATT_WIDTH].reshape(db, N_KV, GROUP, HEAD_DIM)
    zq = jnp.zeros_like(qh[:, 0])
    q8 = jnp.concatenate([jnp.concatenate([qh[:, 0], zq], axis=-1), jnp.concatenate([zq, qh[:, 1]], axis=-1)], axis=1)
    n_phys = cache_cmp_kv.shape[1]
    token_minor = lambda a: jnp.transpose(a[0], (0, 2, 3, 4, 1))
    o_c, picks = _sample_cmp(page_table, q8, token_minor(cache_cmp_kv).reshape(n_phys, 2 * KV_WIDTH, PAGE_SIZE), sbias, w)
    picks = picks[:, :N_KV, :N_PICK].reshape(db, N_KV * N_PICK)
    gates8 = jnp.pad(jnp.transpose(gates_s[:, :3 * N_HEADS].reshape(db, 3, N_HEADS), (0, 2, 1)),
                     ((0, 0), (0, 0), (0, LANES - 3)))
    o8, win_new = _sample_attn(page_table, picks, q8,
                               token_minor(cache_sel_kv).reshape(n_phys, 2 * N_KV, HEAD_DIM, PAGE_SIZE),
                               token_minor(state_win_kv).reshape(db, 2 * KV_WIDTH, wlen),
                               sel_s[:, None, :], win_s[:, None, :], gates8, o_c, stabs)
    o_att_s = o8.reshape(db, ATT_WIDTH)
    win_new = jnp.transpose(win_new.reshape(db, 2, N_KV, HEAD_DIM, wlen), (0, 4, 1, 2, 3))
    u_prev = jnp.transpose(state_conv[0], (1, 0, 2))
    y_s = _tail(xs, o_att_s, cb_s, u_s, u_prev, p_sample[0].reshape(db, PLE_DIM), w, None)

    return (y_p.reshape(bp, tp, D_MODEL), y_s.reshape(db, 1, D_MODEL),
            kv6_t(cmp_t), kv6_t(sel_t), kv6_t(win_t),
            seq(u)[:, tp - 2:][None],
            kv6(jnp.concatenate([cmp_s[0], cmp_s[1]], axis=-1), db, 1), kv6(sel_s, db, 1), win_new[None],
            jnp.concatenate([state_conv[0][:, 1:], u_s[:, None, :]], axis=1)[None])
```

```python
import functools
import math

import jax
import jax.numpy as jnp
import numpy as np
from jax import lax
from jax.experimental import pallas as pl
from jax.experimental.pallas import tpu as pltpu

D_MODEL = 1024
HEAD_DIM = 64
N_HEADS = 8
N_KV = 2
GROUP = N_HEADS // N_KV
ATT_WIDTH = N_HEADS * HEAD_DIM
KV_WIDTH = N_KV * HEAD_DIM
CONV_DIM = D_MODEL - ATT_WIDTH
PAGE_SIZE = 128
CMP_BLOCK = 32
SEL_BLOCK = 64
TOP_N = 8
WINDOW = 512
RP_BUCKETS = 32
RP_MAX_DIST = 128
D_FF = 4 * D_MODEL
PLE_DIM = 256
SCALE = HEAD_DIM ** -0.5
NEG = -1e30
EPS = 1e-6
LOG2E = 1.4426950408889634

LANES = 128
QBLK = 128
ROW_TILE = 512
VMEM_LIMIT = 56 * 1024 * 1024

_BF = jnp.bfloat16
_F32 = jnp.float32


def _dot(a, b):
    return jnp.dot(a, b, preferred_element_type=_F32)


def _dot_nt(a, b):
    return lax.dot_general(a, b, (((1,), (1,)), ((), ())), preferred_element_type=_F32)


def _rms_rows(x, gain):
    return x * lax.rsqrt(jnp.mean(x * x, axis=-1, keepdims=True) + EPS) * gain


def _group_rms(z, gmat, gain):
    ssq = _dot((z * z).astype(_BF), gmat) * (1.0 / HEAD_DIM)
    return z * lax.rsqrt(ssq + EPS) * gain


def _inproj_body(x_ref, nm_ref, wq_ref, wkv_ref, wg_ref, wc_ref, qg_ref, kg_ref, g512_ref, g128_ref,
                 q_ref, cmp_ref, sel_ref, win_ref, gate_ref, cb_ref, u_ref, *kv_t_refs):
    a = _rms_rows(x_ref[...], nm_ref[...]).astype(_BF)
    zq = _dot(a, wq_ref[...])
    qn = _group_rms(zq, g512_ref[...], qg_ref[...]) * SCALE
    q_ref[:, 0:ATT_WIDTH] = qn.astype(_BF)
    q_ref[:, ATT_WIDTH:] = (qn * LOG2E).astype(_BF)
    zkv = _dot(a, wkv_ref[...])
    g128 = g128_ref[...]
    halves = ((zkv[:, 0:128], zkv[:, 128:256]),
              (_group_rms(zkv[:, 256:384], g128, kg_ref[0:1, :]), zkv[:, 384:512]),
              (_group_rms(zkv[:, 512:640], g128, kg_ref[1:2, :]), zkv[:, 640:768]))
    for i, (ref, (k, v)) in enumerate(zip((cmp_ref, sel_ref, win_ref), halves)):
        if i == 0:
            ref[0] = k
            ref[1] = v
        else:
            ref[:, 0:128] = k
            ref[:, 128:256] = v
        if kv_t_refs:
            kv_t_refs[i][0, 0:128, :] = k.T
            kv_t_refs[i][0, 128:256, :] = v.T
    gate_ref[...] = jax.nn.sigmoid(_dot(a, wg_ref[...]))
    zc = _dot(a, wc_ref[...])
    cb_ref[...] = zc[:, 0:512]
    u_ref[...] = zc[:, 512:1024] * zc[:, 1024:1536]


def _inproj(x, w, seq_len=None):
    n = x.shape[0]
    tm = min(ROW_TILE, n)
    row = lambda c: pl.BlockSpec((tm, c), lambda i: (i, 0))
    full = lambda a: pl.BlockSpec(a.shape, lambda i: (0,) * a.ndim)
    consts = (w['norm_mix'], w['wq'], w['wkv'], w['wg'], w['wc'], w['q_gain'], w['k_gain'], w['g512'], w['g128'])
    out_specs = [row(2 * ATT_WIDTH), pl.BlockSpec((2, tm, KV_WIDTH), lambda i: (0, i, 0)), row(256), row(256),
                 row(128), row(512), row(512)]
    out_shape = ([jax.ShapeDtypeStruct((n, 2 * ATT_WIDTH), _BF), jax.ShapeDtypeStruct((2, n, KV_WIDTH), _F32)]
                 + [jax.ShapeDtypeStruct((n, 256), _F32)] * 2
                 + [jax.ShapeDtypeStruct((n, 128), _F32)] + [jax.ShapeDtypeStruct((n, 512), _F32)] * 2)
    if seq_len is not None:
        nt = seq_len // tm
        lead = nt - WINDOW // tm
        kv_t = pl.BlockSpec((1, 256, tm), lambda i: (i // nt, 0, i % nt))
        win_t = pl.BlockSpec((1, 256, tm), lambda i: (i // nt, 0, jnp.maximum(i % nt - lead, 0)))
        out_specs += [kv_t, kv_t, win_t]
        out_shape += [jax.ShapeDtypeStruct((n // seq_len, 256, seq_len), _F32)] * 2 + [
            jax.ShapeDtypeStruct((n // seq_len, 256, WINDOW), _F32)]
    return pl.pallas_call(
        _inproj_body,
        grid=(n // tm,),
        in_specs=[row(D_MODEL)] + [full(c) for c in consts],
        out_specs=out_specs,
        out_shape=out_shape,
        compiler_params=pltpu.CompilerParams(dimension_semantics=("arbitrary",), vmem_limit_bytes=VMEM_LIMIT),
        name="inproj",
    )(x, *consts)


FF_CHUNK = 1024


def _tail_body(halo, h_ref, o_ref, cb_ref, u_ref, up_ref, p_ref, cw_ref, on_ref, nmlp_ref, nple_ref,
               wout_hbm, wup_hbm, wdn_hbm, wgate_hbm, wproj_hbm, y_ref,
               uext_ref, wout_ref, wup_ref, wdn_ref, wgate_ref, wproj_ref, wsem):
    @pl.when(pl.program_id(0) == 0)
    def _load_weights():
        copies = [pltpu.make_async_copy(src, dst, wsem.at[i]) for i, (src, dst) in enumerate(
            ((wout_hbm, wout_ref), (wup_hbm, wup_ref), (wdn_hbm, wdn_ref), (wgate_hbm, wgate_ref),
             (wproj_hbm, wproj_ref)))]
        for c in copies:
            c.start()
        for c in copies:
            c.wait()

    tm = h_ref.shape[0]
    u = u_ref[...]
    if halo:
        first = (pl.program_id(0) % halo) == 0
        prev = jnp.where(first, 0.0, up_ref[...])
        uext_ref[0:8, :] = prev
        uext_ref[8:tm + 8, :] = u
        u2 = uext_ref[6:tm + 6, :]
        u1 = uext_ref[7:tm + 7, :]
    else:
        u2 = up_ref[0]
        u1 = up_ref[1]
    yc = cw_ref[0:1, :] * u2 + cw_ref[1:2, :] * u1 + cw_ref[2:3, :] * u
    mix_a = _rms_rows(o_ref[...], on_ref[:, 0:ATT_WIDTH]).astype(_BF)
    mix_c = _rms_rows(cb_ref[...] * yc, on_ref[:, ATT_WIDTH:]).astype(_BF)
    h = h_ref[...] + _dot(mix_a, wout_ref[0:ATT_WIDTH, :]) + _dot(mix_c, wout_ref[ATT_WIDTH:, :])
    a = _rms_rows(h, nmlp_ref[...]).astype(_BF)
    y_ref[...] = h
    for c in range(D_FF // FF_CHUNK):
        t = jnp.maximum(_dot(a, wup_ref[:, c * FF_CHUNK:(c + 1) * FF_CHUNK]), 0.0)
        y_ref[...] += _dot((t * t).astype(_BF), wdn_ref[c * FF_CHUNK:(c + 1) * FF_CHUNK, :])
    h = y_ref[...]
    a = _rms_rows(h, nple_ref[...]).astype(_BF)
    gate = jax.nn.sigmoid(_dot(a, wgate_ref[...]))
    y_ref[...] = h + gate * _dot(p_ref[...].astype(_BF), wproj_ref[...])


def _tail(h, o_att, cb, u, u_prev, p, w, seq_len):
    n = h.shape[0]
    tm = min(ROW_TILE, n)
    row = lambda c: pl.BlockSpec((tm, c), lambda i: (i, 0))
    const = lambda a: pl.BlockSpec(a.shape, lambda i: (0,) * a.ndim)
    if seq_len is not None:
        halo = seq_len // tm
        up_spec = pl.BlockSpec((8, CONV_DIM), lambda i: (jnp.maximum(i * (tm // 8) - 1, 0), 0))
        up = u
    else:
        halo = 0
        up_spec = pl.BlockSpec((2, tm, CONV_DIM), lambda i: (0, i, 0))
        up = u_prev
    consts = (w['conv_w'], w['out_norm'], w['norm_mlp'], w['norm_ple'])
    mats = (w['w_out'], w['w_up'], w['w_down'], w['w_ple_gate'], w['w_ple_proj'])
    return pl.pallas_call(
        functools.partial(_tail_body, halo),
        grid=(n // tm,),
        in_specs=[row(D_MODEL), row(ATT_WIDTH), row(CONV_DIM), row(CONV_DIM), up_spec, row(PLE_DIM)]
        + [const(c) for c in consts] + [pl.BlockSpec(memory_space=pl.ANY)] * len(mats),
        out_specs=row(D_MODEL),
        out_shape=jax.ShapeDtypeStruct((n, D_MODEL), _F32),
        scratch_shapes=[pltpu.VMEM((tm + 8, CONV_DIM), _F32)] + [pltpu.VMEM(m.shape, _BF) for m in mats]
        + [pltpu.SemaphoreType.DMA((len(mats),))],
        compiler_params=pltpu.CompilerParams(dimension_semantics=("arbitrary",), vmem_limit_bytes=VMEM_LIMIT),
        name="tail",
    )(h, o_att, cb, u, up, p, *consts, *mats)


def _split_heads(x, lane_lo):
    xr = pltpu.roll(x, HEAD_DIM, axis=1)
    zero = jnp.zeros_like(x)
    a = (jnp.where(lane_lo, x, zero), jnp.where(lane_lo, xr, zero))
    b = (jnp.where(lane_lo, zero, xr), jnp.where(lane_lo, zero, x))
    return a, b


def _compress_rows(load, ns, bd_ref, pe_ref):
    acc = [jnp.zeros((2 * ns, LANES), _F32), jnp.zeros((2 * ns, LANES), _F32)]
    for l in range(CMP_BLOCK):
        for plane in range(2):
            x = jnp.concatenate([load(plane, l), load(plane, CMP_BLOCK + l)], axis=0)
            x = x + pe_ref[l, plane:plane + 1, :]
            acc[plane] = acc[plane] + _dot(x.astype(_BF), bd_ref[l, plane])
    return acc


def _compress_prompt_body(slab_ref, bd_ref, pe_ref, kg_ref, g128_ref, kA_ref, kB_ref, vA_ref, vB_ref):
    ns = slab_ref.shape[2] // SEL_BLOCK
    k, v = _compress_rows(lambda plane, t0: slab_ref[plane, 0, pl.ds(t0, ns, stride=SEL_BLOCK), :],
                          ns, bd_ref, pe_ref)
    k = _group_rms(k, g128_ref[...], kg_ref[...])
    lane_lo = lax.broadcasted_iota(jnp.int32, k.shape, 1) < HEAD_DIM
    ka, kb = _split_heads(k, lane_lo)
    va, vb = _split_heads(v, lane_lo)
    pad = jnp.zeros((LANES - 2 * ns, LANES), _BF)
    for g in range(N_KV):
        for ref, val in ((kA_ref, ka[g]), (kB_ref, kb[g]), (vA_ref, va[g]), (vB_ref, vb[g])):
            ref[0, g, 0:2 * ns, :] = val.astype(_BF)
            if 2 * ns < LANES:
                ref[0, g, 2 * ns:, :] = pad


def _compress_prompt(slab, w):
    _, b, t, _ = slab.shape
    assert t % QBLK == 0 and t // CMP_BLOCK <= LANES
    full = lambda a: pl.BlockSpec(a.shape, lambda i: (0,) * a.ndim)
    consts = (w['bd'], w['pe_t'], w['kc_gain'], w['g128'])
    out = jax.ShapeDtypeStruct((b, N_KV, LANES, LANES), _BF)
    return pl.pallas_call(
        _compress_prompt_body,
        grid=(b,),
        in_specs=[pl.BlockSpec((2, 1, t, LANES), lambda i: (0, i, 0, 0))] + [full(c) for c in consts],
        out_specs=[pl.BlockSpec((1, N_KV, LANES, LANES), lambda i: (i, 0, 0, 0))] * 4,
        out_shape=[out] * 4,
        compiler_params=pltpu.CompilerParams(dimension_semantics=("arbitrary",), vmem_limit_bytes=VMEM_LIMIT),
        name="compress_prompt",
    )(slab, *consts)


BUILD_ROWS = 256
QSTEP = 2


def _top_extra(impb, cand, lane_f, n_extra):
    v = jnp.where(cand, impb, -1.0)
    picked = jnp.zeros(impb.shape, _F32)
    for _ in range(n_extra):
        mx = jnp.max(v, axis=-1, keepdims=True)
        first = jnp.min(jnp.where(v == mx, lane_f, 1e9), axis=-1, keepdims=True)
        hit = lane_f == first
        picked = jnp.where(hit, 1.0, picked)
        v = jnp.where(hit, -1.0, v)
    return picked


def _attn_prompt_body(q_ref, ks_ref, kw_ref, gate_ref, kcA_ref, kcB_ref, vcA_ref, vcB_ref, cbias_ref, tiles_ref,
                      eg_ref, o_ref, ksA, ksB, vsA, vsB, kwA, kwB, vwA, vwB, s_scr, w_scr, m_scr, acc_scr):
    step = pl.program_id(1)
    t_len = ks_ref.shape[1]
    n_sel = t_len // SEL_BLOCK

    @pl.when(step == 0)
    def _build():
        def chunk(c, carry):
            r0 = pl.multiple_of(c * BUILD_ROWS, BUILD_ROWS)
            rows = pl.ds(r0, BUILD_ROWS)
            lane = lax.broadcasted_iota(jnp.int32, (BUILD_ROWS, LANES), 1)
            blk = (r0 + lax.broadcasted_iota(jnp.int32, (BUILD_ROWS, LANES), 0)) // SEL_BLOCK
            lane_lo = lane < HEAD_DIM
            oh_hi = jnp.where(lane == blk + HEAD_DIM, 1.0, 0.0)
            oh_lo = jnp.where(lane == blk, 1.0, 0.0)
            one_hi = jnp.where(lane == HEAD_DIM, 1.0, 0.0)
            one_lo = jnp.where(lane == 0, 1.0, 0.0)
            for src, k_a, k_b, v_a, v_b, onehot in ((ks_ref, ksA, ksB, vsA, vsB, True),
                                                    (kw_ref, kwA, kwB, vwA, vwB, False)):
                ka, kb = _split_heads(src[0, rows, 0:128], lane_lo)
                va, vb = _split_heads(src[0, rows, 128:256], lane_lo)
                for g in range(N_KV):
                    if onehot:
                        k_a[g, rows, :] = jnp.where(lane_lo, ka[g], oh_hi).astype(_BF)
                        k_b[g, rows, :] = jnp.where(lane_lo, oh_lo, kb[g]).astype(_BF)
                    else:
                        k_a[g, rows, :] = ka[g].astype(_BF)
                        k_b[g, rows, :] = kb[g].astype(_BF)
                    v_a[g, rows, :] = jnp.where(lane_lo, va[g], one_hi).astype(_BF)
                    v_b[g, rows, :] = jnp.where(lane_lo, one_lo, vb[g]).astype(_BF)
            return carry
        lax.fori_loop(0, t_len // BUILD_ROWS, chunk, 0)

    lane2 = lax.broadcasted_iota(jnp.int32, (2 * QBLK, LANES), 1)
    lo2 = lane2 < HEAD_DIM
    lane1 = lax.broadcasted_iota(jnp.int32, (QBLK, LANES), 1)
    lane1_f = lane1.astype(_F32)
    row1 = lax.broadcasted_iota(jnp.int32, (QBLK, LANES), 0)
    halves = range(QSTEP)
    qbs = [step * QSTEP + h for h in halves]
    qb_last = qbs[-1]
    rows_of = [slice(h * QBLK, (h + 1) * QBLK) for h in halves]
    curs = [(qb * QBLK + row1) // SEL_BLOCK for qb in qbs]

    def normalize(acc, x):
        l = jnp.sum(jnp.where(lane2 == (HEAD_DIM if x == 0 else 0), acc, 0.0), axis=-1, keepdims=True)
        keep = lo2 if x == 0 else jnp.logical_not(lo2)
        return jnp.where(keep, acc / l, 0.0)

    chains = [(h, g, x) for h in halves for g in range(N_KV) for x in range(2)]
    kv_chains = [(g, x) for g in range(N_KV) for x in range(2)]
    o_cmp, q_plain, q_pair, importance = [], [], [], []
    for h in halves:
        for g in range(N_KV):
            qs = jnp.concatenate([q_ref[0, rows_of[h], (2 * g) * LANES:(2 * g + 1) * LANES],
                                  q_ref[0, rows_of[h], (2 * g + 1) * LANES:(2 * g + 2) * LANES]], axis=0)
            zero = jnp.zeros_like(qs)
            q_a = jnp.where(lo2, qs, zero)
            q_b = jnp.where(lo2, zero, qs)
            qs2 = jnp.concatenate([q_ref[0, rows_of[h], ATT_WIDTH + (2 * g) * LANES:ATT_WIDTH + (2 * g + 1) * LANES],
                                   q_ref[0, rows_of[h], ATT_WIDTH + (2 * g + 1) * LANES:ATT_WIDTH + (2 * g + 2) * LANES]],
                                  axis=0)
            q_pair.append(qs2)
            q_plain += [jnp.where(lo2, qs2, zero), jnp.where(lo2, zero, qs2)]

            def cmp_probs(qx, k_ref, x):
                s = _dot_nt(qx, k_ref[0, g]) + cbias_ref[h, g, x]
                m = jnp.max(s, axis=-1, keepdims=True)
                e = jnp.where(s > 0.5 * NEG, jnp.exp(s - m), 0.0)
                l = jnp.sum(e, axis=-1, keepdims=True)
                return e / jnp.where(l > 0.0, l, 1.0)
            p_a = cmp_probs(q_a, kcA_ref, 0)
            p_b = cmp_probs(q_b, kcB_ref, 1)
            o_cmp.append(_dot(p_a.astype(_BF), vcA_ref[0, g]) + _dot(p_b.astype(_BF), vcB_ref[0, g]))
            imp = p_a[0:QBLK] + p_a[QBLK:] + p_b[0:QBLK] + p_b[QBLK:]
            importance.append(imp + pltpu.roll(imp, LANES - n_sel, axis=1))

    two = lambda a: jnp.concatenate([a, a], axis=0)
    cands = jnp.concatenate([two((lane1 >= 1) & (lane1 <= curs[h] - 2)) for h in halves], axis=0)
    picked = _top_extra(jnp.concatenate(importance, axis=0), cands,
                        jnp.concatenate([lane1_f] * (2 * QSTEP), axis=0), TOP_N - 3)
    q_aug = []
    for h in halves:
        cur = curs[h]
        forced = (lane1 == 0) | (lane1 == cur) | (lane1 == cur - 1)
        few = cur <= TOP_N - 1
        for g in range(N_KV):
            i = h * N_KV + g
            chosen = forced | (few & (lane1 <= cur)) | ((picked[i * QBLK:(i + 1) * QBLK] > 0.5) & jnp.logical_not(few))
            sb_lo = jnp.where(chosen | (lane1 >= n_sel), 0.0, NEG)
            sb_hi = pltpu.roll(sb_lo, HEAD_DIM, axis=1)
            q_aug += [jnp.where(lo2, q_pair[i], two(sb_hi).astype(_BF)),
                      jnp.where(lo2, two(sb_lo).astype(_BF), q_pair[i])]

    k_sel, v_sel, k_win, v_win = (ksA, ksB), (vsA, vsB), (kwA, kwB), (vwA, vwB)
    n_win = WINDOW // QBLK + 1

    w_firsts = [jnp.maximum(qb - (n_win - 1), 0) for qb in qbs]
    w_rows = [pl.ds(pl.multiple_of(wf * QBLK, QBLK), n_win * QBLK) for wf in w_firsts]
    m_win = []
    for ch, (h, g, x) in enumerate(chains):
        s = _dot_nt(q_plain[ch], k_win[x][g, w_rows[h], :])
        m = None
        for i in range(n_win):
            idx = qbs[h] - w_firsts[h] - i
            sb = s[:, i * QBLK:(i + 1) * QBLK] + tiles_ref[jnp.where(idx < 0, n_win, idx), g, x]
            w_scr[ch, :, i * QBLK:(i + 1) * QBLK] = sb
            m = sb if m is None else jnp.maximum(m, sb)
        m_win.append(jnp.max(m, axis=-1, keepdims=True))
    o_win = []
    for ch, (h, g, x) in enumerate(chains):
        p = jnp.exp2(w_scr[ch] - m_win[ch]).astype(_BF)
        o_win.append(normalize(_dot(p, v_win[x][g, w_rows[h], :]), x))

    n_quart = 4
    qw = t_len // n_quart
    per_q = qw // QBLK
    m_scr[...] = jnp.full(m_scr.shape, NEG, _F32)
    for qi in range(n_quart):
        @pl.when(qb_last * QBLK >= qi * qw)
        def _scores(qi=qi):
            for g, x in kv_chains:
                chs = [chains.index((h, g, x)) for h in halves]
                s_all = _dot_nt(jnp.concatenate([q_aug[ch] for ch in chs], axis=0),
                                k_sel[x][g, qi * qw:(qi + 1) * qw, :])
                for h, ch in zip(halves, chs):
                    s = s_all[h * 2 * QBLK:(h + 1) * 2 * QBLK]
                    m = m_scr[ch]
                    for ci in range(per_q):
                        c = qi * per_q + ci
                        idx = qbs[h] - c
                        sb = s[:, ci * QBLK:(ci + 1) * QBLK] + tiles_ref[jnp.where(idx < 0, n_win, jnp.minimum(idx, 2)), g, x]
                        s_scr[ch, :, c * QBLK:(c + 1) * QBLK] = sb
                        m = jnp.maximum(m, sb)
                    m_scr[ch] = m
    m_sel = [jnp.max(m_scr[ch], axis=-1, keepdims=True) for ch in range(len(chains))]

    acc_scr[...] = jnp.zeros(acc_scr.shape, _F32)
    for qi in range(n_quart):
        @pl.when(qb_last * QBLK >= qi * qw)
        def _weighted(qi=qi):
            for g, x in kv_chains:
                chs = [chains.index((h, g, x)) for h in halves]
                p = jnp.concatenate([jnp.exp2(s_scr[ch, :, qi * qw:(qi + 1) * qw] - m_sel[ch]).astype(_BF)
                                     for ch in chs], axis=0)
                pv = _dot(p, v_sel[x][g, qi * qw:(qi + 1) * qw, :])
                for h, ch in zip(halves, chs):
                    acc_scr[ch] += pv[h * 2 * QBLK:(h + 1) * 2 * QBLK]

    gates = gate_ref[0]
    g_hi = gates.astype(_BF)
    g_lo = (gates - g_hi.astype(_F32)).astype(_BF)
    gexp = _dot(g_hi, eg_ref[...]) + _dot(g_lo, eg_ref[...])
    for h in halves:
        for g in range(N_KV):
            i = h * N_KV + g
            o_c = o_cmp[i]
            o_s = normalize(acc_scr[2 * i], 0) + normalize(acc_scr[2 * i + 1], 1)
            o_w = o_win[2 * i] + o_win[2 * i + 1]
            for pr in range(2):
                rows = slice(pr * QBLK, (pr + 1) * QBLK)
                col = (2 * g + pr) * LANES
                gx = lambda br: gexp[rows_of[h], br * ATT_WIDTH + col:br * ATT_WIDTH + col + LANES]
                o_ref[0, rows_of[h], col:col + LANES] = gx(0) * o_c[rows] + gx(1) * o_s[rows] + gx(2) * o_w[rows]


def _attn_prompt(q, sel, win, gates, kc, cbias, w):
    b, t, _ = sel.shape
    nq = t // QBLK
    assert t // SEL_BLOCK <= CMP_BLOCK and t % (4 * QBLK) == 0 and t >= WINDOW + QBLK and nq % QSTEP == 0
    kcA, kcB, vcA, vcB = kc
    full = lambda a: pl.BlockSpec(a.shape, lambda i, j: (0,) * a.ndim)
    slab = pl.BlockSpec((1, t, 256), lambda i, j: (i, 0, 0))
    kcs = pl.BlockSpec((1, N_KV, LANES, LANES), lambda i, j: (i, 0, 0, 0))
    n_chain = QSTEP * 2 * N_KV
    q_rows = QSTEP * QBLK
    scratch = [pltpu.VMEM((N_KV, t, LANES), _BF)] * 8 + [
        pltpu.VMEM((n_chain, 2 * QBLK, t), _F32), pltpu.VMEM((n_chain, 2 * QBLK, WINDOW + QBLK), _F32),
        pltpu.VMEM((n_chain, 2 * QBLK, LANES), _F32), pltpu.VMEM((n_chain, 2 * QBLK, LANES), _F32)]
    return pl.pallas_call(
        _attn_prompt_body,
        grid=(b, nq // QSTEP),
        in_specs=[pl.BlockSpec((1, q_rows, 2 * ATT_WIDTH), lambda i, j: (i, j, 0)), slab, slab,
                  pl.BlockSpec((1, q_rows, LANES), lambda i, j: (i, j, 0)), kcs, kcs, kcs, kcs,
                  pl.BlockSpec((QSTEP, N_KV, 2, 2 * QBLK, LANES), lambda i, j: (j, 0, 0, 0, 0)),
                  full(w['tiles']), full(w['egate'])],
        out_specs=pl.BlockSpec((1, q_rows, ATT_WIDTH), lambda i, j: (i, j, 0)),
        out_shape=jax.ShapeDtypeStruct((b, t, ATT_WIDTH), _F32),
        scratch_shapes=scratch,
        compiler_params=pltpu.CompilerParams(dimension_semantics=("arbitrary", "arbitrary"),
                                             vmem_limit_bytes=VMEM_LIMIT),
        name="attn_prompt",
    )(q, sel, win, gates, kcA, kcB, vcA, vcB, cbias, w['tiles'], w['egate'])


N_PICK = TOP_N - 3
N_SLOT = TOP_N - 1
PAGE_ROWS = 2 * KV_WIDTH


PAIR_TOKENS = 2 * PAGE_SIZE


def _sample_cmp_body(pt_ref, q_ref, cache_ref, bd_ref, pet_ref, perm_ref, kg_ref, g128_ref, sb_ref, oc_ref, pick_ref,
                     buf, rows_scr, sem):
    b = pl.program_id(0)
    nb = pl.num_programs(0)
    n_pairs = pt_ref.shape[1] // 2
    nc = 8 * n_pairs
    slot = b % 2

    def page_copy(bb, pair, half, sl):
        return pltpu.make_async_copy(cache_ref.at[pt_ref[bb, 2 * pair + half]],
                                     buf.at[sl, pair, :, half * PAGE_SIZE:(half + 1) * PAGE_SIZE], sem.at[sl])

    def for_pages(fn):
        def step(pair, c):
            fn(pair, 0)
            fn(pair, 1)
            return c
        lax.fori_loop(0, n_pairs, step, 0)

    @pl.when(b == 0)
    def _prime():
        for_pages(lambda pair, half: page_copy(0, pair, half, 0).start())

    @pl.when(b + 1 < nb)
    def _prefetch():
        for_pages(lambda pair, half: page_copy(b + 1, pair, half, 1 - slot).start())

    for_pages(lambda pair, half: page_copy(b, pair, half, slot).wait())

    def regroup(pair, c):
        x = (buf[slot, pair] + pet_ref[...]).astype(_BF)
        rows_scr[pair] = _dot_nt(perm_ref[...], x)
        return c
    lax.fori_loop(0, n_pairs, regroup, 0, unroll=4)

    acc = jnp.zeros((nc, 2 * KV_WIDTH), _F32)
    for l in range(CMP_BLOCK):
        x = rows_scr[:, 8 * l:8 * (l + 1), :].reshape(nc, 2 * KV_WIDTH)
        acc = acc + _dot(x.astype(_BF), bd_ref[l])
    k = _group_rms(acc[:, 0:KV_WIDTH], g128_ref[...], kg_ref[...]).astype(_BF)
    v = acc[:, KV_WIDTH:].astype(_BF)
    s = _dot_nt(q_ref[0], k) + sb_ref[...]
    m = jnp.max(s, axis=-1, keepdims=True)
    e = jnp.exp(s - m)
    p = e / jnp.sum(e, axis=-1, keepdims=True)
    oc_ref[0] = _dot(p.astype(_BF), v)
    imp8 = p + pltpu.roll(p, nc - 1, axis=1)
    rows = [imp8[h:h + 1] for h in range(N_HEADS)]
    imp_g = [rows[g * GROUP] + rows[g * GROUP + 1] + rows[g * GROUP + 2] + rows[g * GROUP + 3] for g in range(N_KV)]
    imp = jnp.concatenate(imp_g + [jnp.full((8 - N_KV, nc), -1.0, _F32)], axis=0)
    lane = lax.broadcasted_iota(jnp.int32, (8, nc), 1)
    lane_f = lane.astype(_F32)
    v_c = jnp.where((lane % 2 == 0) & (lane >= 2) & (lane <= nc - 4), imp, -1.0)
    picks = jnp.zeros((8, nc), _F32)
    for i in range(N_PICK):
        mx = jnp.max(v_c, axis=-1, keepdims=True)
        first = jnp.min(jnp.where(v_c == mx, lane_f, 1e9), axis=-1, keepdims=True)
        picks = jnp.where(lane == i, first * 0.5, picks)
        v_c = jnp.where(lane_f == first, -1.0, v_c)
    pick_ref[0] = picks.astype(jnp.int32)


def _sample_cmp(page_table, q8, cache, sbias, w):
    db, n_pages = page_table.shape
    nc = 4 * n_pages
    assert n_pages % 2 == 0 and nc // 2 - 2 >= N_PICK
    full = lambda a: pl.BlockSpec(a.shape, lambda i, pt: (0,) * a.ndim)
    consts = (w['bd_kv'], w['pe_tok'], w['perm'], w['kc_gain'], w['g128'], sbias)
    grid_spec = pltpu.PrefetchScalarGridSpec(
        num_scalar_prefetch=1,
        grid=(db,),
        in_specs=[pl.BlockSpec((1, 8, LANES), lambda i, pt: (i, 0, 0)), pl.BlockSpec(memory_space=pl.ANY)]
        + [full(c) for c in consts],
        out_specs=[pl.BlockSpec((1, 8, LANES), lambda i, pt: (i, 0, 0)),
                   pl.BlockSpec((1, 8, nc), lambda i, pt: (i, 0, 0))],
        scratch_shapes=[pltpu.VMEM((2, n_pages // 2, 2 * KV_WIDTH, PAIR_TOKENS), _F32),
                        pltpu.VMEM((n_pages // 2, PAIR_TOKENS, 2 * KV_WIDTH), _F32),
                        pltpu.SemaphoreType.DMA((2,))],
    )
    return pl.pallas_call(
        _sample_cmp_body,
        grid_spec=grid_spec,
        out_shape=[jax.ShapeDtypeStruct((db, 8, LANES), _F32), jax.ShapeDtypeStruct((db, 8, nc), jnp.int32)],
        compiler_params=pltpu.CompilerParams(dimension_semantics=("arbitrary",), vmem_limit_bytes=VMEM_LIMIT),
        name="sample_cmp",
    )(page_table, q8, cache, *consts)


def _sample_attn_body(pt_ref, pick_ref, q_ref, cache_ref, win_ref, nsel_ref, nwin_ref, ncol_ref, gate_ref, oc_ref,
                      bsel_ref, b0_ref, bwin_ref, o_ref, wout_ref, kb, sem):
    b = pl.program_id(0)
    nb = pl.num_programs(0)
    nj = 2 * pt_ref.shape[1]
    wlen = win_ref.shape[2]
    slot = b % 2

    def block_of(bb, g, s):
        if s == 0:
            return 0
        if s == N_SLOT - 1:
            return nj - 1
        return pick_ref[bb, g * N_PICK + (s - 1)]

    def tile_copy(bb, g, s, plane, sl):
        page = pt_ref[bb, block_of(bb, g, s) // 2]
        return pltpu.make_async_copy(cache_ref.at[page, plane * N_KV + g],
                                     kb.at[sl, g, plane, :, pl.ds(s * PAGE_SIZE, PAGE_SIZE)], sem.at[sl])

    def for_tiles(fn):
        for g in range(N_KV):
            for s in range(N_SLOT):
                for plane in range(2):
                    fn(g, s, plane)

    @pl.when(b == 0)
    def _prime():
        for_tiles(lambda g, s, plane: tile_copy(0, g, s, plane, 0).start())

    @pl.when(b + 1 < nb)
    def _prefetch():
        for_tiles(lambda g, s, plane: tile_copy(b + 1, g, s, plane, 1 - slot).start())

    q8 = q_ref[0]
    q8f = q8.astype(_F32)
    head_g = lax.broadcasted_iota(jnp.int32, (8, HEAD_DIM), 0) // GROUP
    own_half = lambda x: jnp.where(head_g == 0, x[:, 0:HEAD_DIM], x[:, HEAD_DIM:])
    b0 = b0_ref[:, 0:1]

    x = win_ref[0]
    lane_w = lax.broadcasted_iota(jnp.int32, x.shape, 1)
    wout_ref[0] = jnp.where(lane_w == wlen - 1, ncol_ref[0], pltpu.roll(x, wlen - 1, axis=1))
    sw = _dot(q8, x[0:KV_WIDTH, :].astype(_BF)) + bwin_ref[...]
    sw_new = jnp.sum(q8f * nwin_ref[0][:, 0:KV_WIDTH], axis=-1, keepdims=True) + b0
    mw = jnp.maximum(jnp.max(sw, axis=-1, keepdims=True), sw_new)
    ew = jnp.exp(sw - mw)
    ew_new = jnp.exp(sw_new - mw)
    o_w = _dot_nt(ew.astype(_BF), x[KV_WIDTH:, :].astype(_BF)) + ew_new * nwin_ref[0][:, KV_WIDTH:]
    o_w = own_half(o_w / (jnp.sum(ew, axis=-1, keepdims=True) + ew_new))

    for_tiles(lambda g, s, plane: tile_copy(b, g, s, plane, slot).wait())

    ss_new = jnp.sum(q8f * nsel_ref[0][:, 0:KV_WIDTH], axis=-1, keepdims=True) + b0
    lane_half = lax.broadcasted_iota(jnp.int32, (8, PAGE_SIZE), 1) // SEL_BLOCK
    o_sel = []
    for g in range(N_KV):
        pieces = []
        for s in range(N_SLOT):
            j = jnp.full((8, PAGE_SIZE), block_of(b, g, s), jnp.int32)
            if s == 0:
                tab = bsel_ref[0]
            elif s == N_SLOT - 1:
                tab = bsel_ref[2]
            else:
                tab = jnp.where(j == nj - 2, bsel_ref[1], bsel_ref[0])
            pieces.append(jnp.where(lane_half == j % 2, tab, NEG))
        ss = _dot(q8[:, g * HEAD_DIM:(g + 1) * HEAD_DIM], kb[slot, g, 0].astype(_BF)) + jnp.concatenate(pieces, axis=1)
        ms = jnp.maximum(jnp.max(ss, axis=-1, keepdims=True), ss_new)
        es = jnp.exp(ss - ms)
        es_new = jnp.exp(ss_new - ms)
        v_new = nsel_ref[0][:, KV_WIDTH + g * HEAD_DIM:KV_WIDTH + (g + 1) * HEAD_DIM]
        o_g = _dot_nt(es.astype(_BF), kb[slot, g, 1].astype(_BF)) + es_new * v_new
        o_sel.append(o_g / (jnp.sum(es, axis=-1, keepdims=True) + es_new))
    o_s = jnp.where(head_g == 0, o_sel[0], o_sel[1])
    gates = gate_ref[0]
    o_ref[0] = gates[:, 0:1] * own_half(oc_ref[0]) + gates[:, 1:2] * o_s + gates[:, 2:3] * o_w


def _sample_attn(page_table, picks, q8, cache, win, new_sel, new_win, gates8, o_c, tabs):
    db, n_pages = page_table.shape
    wlen = win.shape[2]
    full = lambda a: pl.BlockSpec(a.shape, lambda i, pt, pk: (0,) * a.ndim)
    per_b = lambda r, c: pl.BlockSpec((1, r, c), lambda i, pt, pk: (i, 0, 0))
    bsel, b0, bwin = tabs
    grid_spec = pltpu.PrefetchScalarGridSpec(
        num_scalar_prefetch=2,
        grid=(db,),
        in_specs=[per_b(8, LANES), pl.BlockSpec(memory_space=pl.ANY), per_b(2 * KV_WIDTH, wlen), per_b(1, 256),
                  per_b(1, 256), per_b(2 * KV_WIDTH, 1), per_b(8, LANES), per_b(8, LANES),
                  full(bsel), full(b0), full(bwin)],
        out_specs=[per_b(8, HEAD_DIM), per_b(2 * KV_WIDTH, wlen)],
        scratch_shapes=[pltpu.VMEM((2, N_KV, 2, HEAD_DIM, N_SLOT * PAGE_SIZE), _F32), pltpu.SemaphoreType.DMA((2,))],
    )
    return pl.pallas_call(
        _sample_attn_body,
        grid_spec=grid_spec,
        out_shape=[jax.ShapeDtypeStruct((db, 8, HEAD_DIM), _F32), jax.ShapeDtypeStruct((db, 2 * KV_WIDTH, wlen), _F32)],
        compiler_params=pltpu.CompilerParams(dimension_semantics=("arbitrary",), vmem_limit_bytes=VMEM_LIMIT),
        name="sample_attn",
    )(page_table, picks, q8, cache, win, new_sel, new_win, new_win[:, 0, :, None], gates8, o_c, bsel, b0, bwin)


def _rel_bucket(dist):
    n = np.maximum(np.asarray(dist), 0)
    max_exact = RP_BUCKETS // 2
    nf = np.maximum(n, 1).astype(np.float32)
    large = max_exact + (np.log(nf / np.float32(max_exact)) / np.float32(math.log(RP_MAX_DIST / max_exact))
                         * np.float32(RP_BUCKETS - max_exact)).astype(np.int32)
    large = np.minimum(large, RP_BUCKETS - 1)
    return np.where(n < max_exact, n, large)


def _bias_of(rel_bias, dist):
    onehot = _rel_bucket(dist)[..., None, None] == np.arange(RP_BUCKETS)[:, None]
    return jnp.sum(jnp.where(onehot, rel_bias, 0.0), axis=-2)


def _head_major(x, lead):
    n = x.shape[-2]
    x = x.reshape(lead + (QBLK, n, N_KV, 2, 2))
    nl = len(lead)
    x = jnp.transpose(x, tuple(range(nl)) + (nl + 2, nl + 4, nl + 3, nl, nl + 1))
    return x.reshape(lead + (N_KV, 2, 2 * QBLK, n))


def _prompt_tables(rel_bias, t_len):
    nq = t_len // QBLK
    ns = t_len // SEL_BLOCK
    ti = np.arange(QBLK)[:, None]
    ki = np.arange(QBLK)[None, :]
    n_idx = WINDOW // QBLK + 1
    pad = QBLK - 1
    bvec = _bias_of(rel_bias, np.arange(-pad, t_len + pad))
    wins = jnp.stack([bvec[i * QBLK:i * QBLK + 2 * QBLK - 1][::-1] for i in range(n_idx)])
    skew = jnp.tile(wins, (1, QBLK + 1, 1))[:, :QBLK * 2 * QBLK].reshape(n_idx, QBLK, 2 * QBLK, N_HEADS)
    tiles = skew[:, ::-1, :QBLK]
    dist = np.stack([i * QBLK + ti - ki for i in range(n_idx)])
    valid = dist >= 0
    valid[n_idx - 1] &= dist[n_idx - 1] < WINDOW
    tiles = jnp.where(valid[..., None], tiles, NEG)
    tiles = jnp.concatenate([tiles, jnp.full_like(tiles[:1], NEG)])
    tiles = _head_major(tiles * LOG2E, (n_idx + 1,))
    shift = CMP_BLOCK * ns * 2
    bcmp = _bias_of(rel_bias, np.arange(-shift, t_len))
    cols = [bcmp[shift - (CMP_BLOCK * blk + CMP_BLOCK - 1):][:t_len]
            for blk in list(range(0, 2 * ns, 2)) + list(range(1, 2 * ns, 2))]
    cb = jnp.stack(cols, axis=1)
    cb = jnp.pad(cb, ((0, 0), (0, LANES - 2 * ns), (0, 0)))
    lane = np.arange(LANES)
    blk = np.where(lane < ns, 2 * lane, 2 * (lane - ns) + 1)
    dist_c = np.arange(t_len)[:, None] - (CMP_BLOCK * blk + CMP_BLOCK - 1)[None, :]
    valid_c = (dist_c >= 0) & (lane < 2 * ns)[None, :]
    cb = jnp.where(valid_c[..., None], cb, NEG)
    cb = _head_major(cb.reshape(nq, QBLK, LANES, N_HEADS), (nq,))
    return tiles, cb


def _sample_tables(rel_bias, past_len, wlen):
    blk = np.arange(past_len // CMP_BLOCK)
    sbias = _bias_of(rel_bias, past_len - (CMP_BLOCK * blk + CMP_BLOCK - 1)).T
    pos = np.arange(PAGE_SIZE) % SEL_BLOCK
    bsel = jnp.stack([_bias_of(rel_bias, np.full(PAGE_SIZE, past_len)).T,
                      _bias_of(rel_bias, 2 * SEL_BLOCK - pos).T,
                      _bias_of(rel_bias, SEL_BLOCK - pos).T])
    b0 = _bias_of(rel_bias, np.zeros((LANES,), np.int32)).T
    tok = np.arange(wlen)
    bwin = jnp.where((tok >= 1)[None, :], _bias_of(rel_bias, wlen - tok).T, NEG)
    return sbias, (bsel, b0, bwin)


def _prep(norm_mix, w_in, q_norm, k_norm, cmp_pe, w_cmp, conv_w, out_norm, w_out, norm_mlp, w_up, w_down,
          norm_ple, w_ple_gate, w_ple_proj):
    w_in = w_in[0]
    o_kv = ATT_WIDTH
    o_g = o_kv + 6 * KV_WIDTH
    o_c = o_g + 3 * N_HEADS
    bd = jnp.einsum('gh,plde->lpgdhe', jnp.eye(N_KV, dtype=_F32), w_cmp[0]).reshape(CMP_BLOCK, 2, LANES, LANES)
    pe_t = jnp.transpose(jnp.tile(cmp_pe[0], (1, 1, N_KV)), (1, 0, 2))
    egate = np.zeros((LANES, 3 * ATT_WIDTH), np.float32)
    for br in range(3):
        for h in range(N_HEADS):
            egate[br * N_HEADS + h, br * ATT_WIDTH + h * HEAD_DIM:br * ATT_WIDTH + (h + 1) * HEAD_DIM] = 1.0
    ones = np.ones((HEAD_DIM, HEAD_DIM), np.float32)
    perm = np.zeros((PAIR_TOKENS, PAIR_TOKENS), np.float32)
    tok = np.arange(PAIR_TOKENS)
    perm[(tok % CMP_BLOCK) * (PAIR_TOKENS // CMP_BLOCK) + tok // CMP_BLOCK, tok] = 1.0
    pe_tok = jnp.tile(jnp.transpose(cmp_pe[0], (0, 2, 1)), (1, N_KV, PAIR_TOKENS // CMP_BLOCK))
    pe_tok = pe_tok.reshape(2 * KV_WIDTH, PAIR_TOKENS)
    zero = jnp.zeros_like(bd[:, 0])
    bd_kv = jnp.concatenate([jnp.concatenate([bd[:, 0], zero], axis=2),
                             jnp.concatenate([zero, bd[:, 1]], axis=2)], axis=1)
    return {
        'perm': jnp.asarray(perm, _BF), 'pe_tok': pe_tok, 'bd_kv': bd_kv.astype(_BF),
        'norm_mix': norm_mix[0][None], 'wq': w_in[:, :o_kv].astype(_BF), 'wkv': w_in[:, o_kv:o_g].astype(_BF),
        'wg': jnp.pad(w_in[:, o_g:o_c], ((0, 0), (0, LANES - 3 * N_HEADS))).astype(_BF),
        'wc': w_in[:, o_c:].astype(_BF),
        'q_gain': jnp.tile(q_norm[0], N_HEADS)[None],
        'k_gain': jnp.stack([jnp.tile(k_norm[0, 1], N_KV), jnp.tile(k_norm[0, 2], N_KV)]),
        'kc_gain': jnp.tile(k_norm[0, 0], N_KV)[None],
        'g512': jnp.asarray(np.kron(np.eye(N_HEADS, dtype=np.float32), ones), _BF),
        'g128': jnp.asarray(np.kron(np.eye(N_KV, dtype=np.float32), ones), _BF),
        'bd': bd.astype(_BF), 'pe_t': pe_t,
        'egate': jnp.asarray(egate, _BF),
        'conv_w': conv_w[0], 'out_norm': out_norm[0][None], 'w_out': w_out[0].astype(_BF),
        'norm_mlp': norm_mlp[0][None], 'w_up': w_up[0].astype(_BF), 'w_down': w_down[0].astype(_BF),
        'norm_ple': norm_ple[0][None], 'w_ple_gate': w_ple_gate[0].astype(_BF),
        'w_ple_proj': w_ple_proj[0].astype(_BF),
    }


def kernel(x_prompt, x_sample, p_prompt, p_sample, cache_cmp_kv, cache_sel_kv, state_win_kv, state_conv, page_table, rel_bias, norm_mix, w_in, q_norm, k_norm, cmp_pe, w_cmp, conv_w, out_norm, w_out, norm_mlp, w_up, w_down, norm_ple, w_ple_gate, w_ple_proj):
    bp, tp, _ = x_prompt.shape
    db, ts, _ = x_sample.shape
    assert norm_mix.shape[0] == 1 and ts == 1 and tp >= WINDOW
    n_pages = page_table.shape[1]
    past_len = n_pages * PAGE_SIZE
    wlen = state_win_kv.shape[2]
    assert wlen == WINDOW and past_len >= 4 * SEL_BLOCK
    w = _prep(norm_mix, w_in, q_norm, k_norm, cmp_pe, w_cmp, conv_w, out_norm, w_out, norm_mlp, w_up, w_down,
              norm_ple, w_ple_gate, w_ple_proj)
    w['tiles'], cbias = _prompt_tables(rel_bias, tp)
    sbias, stabs = _sample_tables(rel_bias, past_len, wlen)
    kv6 = lambda a, b, t: a.reshape(1, b, t, 2, N_KV, HEAD_DIM)

    xp = x_prompt.reshape(bp * tp, D_MODEL)
    q, cmp_p, sel_p, win_p, gates, cb, u, cmp_t, sel_t, win_t = _inproj(xp, w, tp)
    seq = lambda a: a.reshape(bp, tp, a.shape[-1])
    kv6_t = lambda a: jnp.transpose(a.reshape(bp, 2, N_KV, HEAD_DIM, a.shape[-1]), (0, 4, 1, 2, 3))[None]
    kc = _compress_prompt(cmp_p.reshape(2, bp, tp, KV_WIDTH), w)
    o_att = _attn_prompt(seq(q), seq(sel_p), seq(win_p), seq(gates), kc, cbias, w)
    y_p = _tail(xp, o_att.reshape(bp * tp, ATT_WIDTH), cb, u, None, p_prompt[0].reshape(bp * tp, PLE_DIM), w, tp)

    xs = x_sample.reshape(db, D_MODEL)
    q_s, cmp_s, sel_s, win_s, gates_s, cb_s, u_s = _inproj(xs, w)
    qh = q_s[:, :ATT_WIDTH].reshape(db, N_KV, GROUP, HEAD_DIM)
    zq = jnp.zeros_like(qh[:, 0])
    q8 = jnp.concatenate([jnp.concatenate([qh[:, 0], zq], axis=-1), jnp.concatenate([zq, qh[:, 1]], axis=-1)], axis=1)
    n_phys = cache_cmp_kv.shape[1]
    token_minor = lambda a: jnp.transpose(a[0], (0, 2, 3, 4, 1))
    o_c, picks = _sample_cmp(page_table, q8, token_minor(cache_cmp_kv).reshape(n_phys, 2 * KV_WIDTH, PAGE_SIZE), sbias, w)
    picks = picks[:, :N_KV, :N_PICK].reshape(db, N_KV * N_PICK)
    gates8 = jnp.pad(jnp.transpose(gates_s[:, :3 * N_HEADS].reshape(db, 3, N_HEADS), (0, 2, 1)),
                     ((0, 0), (0, 0), (0, LANES - 3)))
    o8, win_new = _sample_attn(page_table, picks, q8,
                               token_minor(cache_sel_kv).reshape(n_phys, 2 * N_KV, HEAD_DIM, PAGE_SIZE),
                               token_minor(state_win_kv).reshape(db, 2 * KV_WIDTH, wlen),
                               sel_s[:, None, :], win_s[:, None, :], gates8, o_c, stabs)
    o_att_s = o8.reshape(db, ATT_WIDTH)
    win_new = jnp.transpose(win_new.reshape(db, 2, N_KV, HEAD_DIM, wlen), (0, 4, 1, 2, 3))
    u_prev = jnp.transpose(state_conv[0], (1, 0, 2))
    y_s = _tail(xs, o_att_s, cb_s, u_s, u_prev, p_sample[0].reshape(db, PLE_DIM), w, None)

    return (y_p.reshape(bp, tp, D_MODEL), y_s.reshape(db, 1, D_MODEL),
            kv6_t(cmp_t), kv6_t(sel_t), kv6_t(win_t),
            seq(u)[:, tp - 2:][None],
            kv6(jnp.concatenate([cmp_s[0], cmp_s[1]], axis=-1), db, 1), kv6(sel_s, db, 1), win_new[None],
            jnp.concatenate([state_conv[0][:, 1:], u_s[:, None, :]], axis=1)[None])
```

```python
import functools
import math

import jax
import jax.numpy as jnp
import numpy as np
from jax import lax
from jax.experimental import pallas as pl
from jax.experimental.pallas import tpu as pltpu

D_MODEL = 1024
HEAD_DIM = 64
N_HEADS = 8
N_KV = 2
GROUP = N_HEADS // N_KV
ATT_WIDTH = N_HEADS * HEAD_DIM
KV_WIDTH = N_KV * HEAD_DIM
CONV_DIM = D_MODEL - ATT_WIDTH
PAGE_SIZE = 128
CMP_BLOCK = 32
SEL_BLOCK = 64
TOP_N = 8
WINDOW = 512
RP_BUCKETS = 32
RP_MAX_DIST = 128
D_FF = 4 * D_MODEL
PLE_DIM = 256
SCALE = HEAD_DIM ** -0.5
NEG = -1e30
EPS = 1e-6
LOG2E = 1.4426950408889634

LANES = 128
QBLK = 128
ROW_TILE = 512
VMEM_LIMIT = 56 * 1024 * 1024

_BF = jnp.bfloat16
_F32 = jnp.float32


def _dot(a, b):
    return jnp.dot(a, b, preferred_element_type=_F32)


def _dot_nt(a, b):
    return lax.dot_general(a, b, (((1,), (1,)), ((), ())), preferred_element_type=_F32)


def _rms_rows(x, gain):
    return x * lax.rsqrt(jnp.mean(x * x, axis=-1, keepdims=True) + EPS) * gain


def _group_rms(z, gmat, gain):
    ssq = _dot((z * z).astype(_BF), gmat) * (1.0 / HEAD_DIM)
    return z * lax.rsqrt(ssq + EPS) * gain


def _inproj_body(x_ref, nm_ref, wq_ref, wkv_ref, wg_ref, wc_ref, qg_ref, kg_ref, g512_ref, g128_ref,
                 q_ref, cmp_ref, sel_ref, win_ref, gate_ref, cb_ref, u_ref, *kv_t_refs):
    a = _rms_rows(x_ref[...], nm_ref[...]).astype(_BF)
    zq = _dot(a, wq_ref[...])
    qn = _group_rms(zq, g512_ref[...], qg_ref[...]) * SCALE
    q_ref[:, 0:ATT_WIDTH] = qn.astype(_BF)
    q_ref[:, ATT_WIDTH:] = (qn * LOG2E).astype(_BF)
    zkv = _dot(a, wkv_ref[...])
    g128 = g128_ref[...]
    halves = ((zkv[:, 0:128], zkv[:, 128:256]),
              (_group_rms(zkv[:, 256:384], g128, kg_ref[0:1, :]), zkv[:, 384:512]),
              (_group_rms(zkv[:, 512:640], g128, kg_ref[1:2, :]), zkv[:, 640:768]))
    for i, (ref, (k, v)) in enumerate(zip((cmp_ref, sel_ref, win_ref), halves)):
        if i == 0:
            ref[0] = k
            ref[1] = v
        else:
            ref[:, 0:128] = k
            ref[:, 128:256] = v
        if kv_t_refs:
            kv_t_refs[i][0, 0:128, :] = k.T
            kv_t_refs[i][0, 128:256, :] = v.T
    gate_ref[...] = jax.nn.sigmoid(_dot(a, wg_ref[...]))
    zc = _dot(a, wc_ref[...])
    cb_ref[...] = zc[:, 0:512]
    u_ref[...] = zc[:, 512:1024] * zc[:, 1024:1536]


def _inproj(x, w, seq_len=None):
    n = x.shape[0]
    tm = min(ROW_TILE, n)
    row = lambda c: pl.BlockSpec((tm, c), lambda i: (i, 0))
    full = lambda a: pl.BlockSpec(a.shape, lambda i: (0,) * a.ndim)
    consts = (w['norm_mix'], w['wq'], w['wkv'], w['wg'], w['wc'], w['q_gain'], w['k_gain'], w['g512'], w['g128'])
    out_specs = [row(2 * ATT_WIDTH), pl.BlockSpec((2, tm, KV_WIDTH), lambda i: (0, i, 0)), row(256), row(256),
                 row(128), row(512), row(512)]
    out_shape = ([jax.ShapeDtypeStruct((n, 2 * ATT_WIDTH), _BF), jax.ShapeDtypeStruct((2, n, KV_WIDTH), _F32)]
                 + [jax.ShapeDtypeStruct((n, 256), _F32)] * 2
                 + [jax.ShapeDtypeStruct((n, 128), _F32)] + [jax.ShapeDtypeStruct((n, 512), _F32)] * 2)
    if seq_len is not None:
        nt = seq_len // tm
        lead = nt - WINDOW // tm
        kv_t = pl.BlockSpec((1, 256, tm), lambda i: (i // nt, 0, i % nt))
        win_t = pl.BlockSpec((1, 256, tm), lambda i: (i // nt, 0, jnp.maximum(i % nt - lead, 0)))
        out_specs += [kv_t, kv_t, win_t]
        out_shape += [jax.ShapeDtypeStruct((n // seq_len, 256, seq_len), _F32)] * 2 + [
            jax.ShapeDtypeStruct((n // seq_len, 256, WINDOW), _F32)]
    return pl.pallas_call(
        _inproj_body,
        grid=(n // tm,),
        in_specs=[row(D_MODEL)] + [full(c) for c in consts],
        out_specs=out_specs,
        out_shape=out_shape,
        compiler_params=pltpu.CompilerParams(dimension_semantics=("arbitrary",), vmem_limit_bytes=VMEM_LIMIT),
        name="inproj",
    )(x, *consts)


FF_CHUNK = 1024


def _tail_body(halo, h_ref, o_ref, cb_ref, u_ref, up_ref, p_ref, cw_ref, on_ref, nmlp_ref, nple_ref,
               wout_hbm, wup_hbm, wdn_hbm, wgate_hbm, wproj_hbm, y_ref,
               uext_ref, wout_ref, wup_ref, wdn_ref, wgate_ref, wproj_ref, wsem):
    @pl.when(pl.program_id(0) == 0)
    def _load_weights():
        copies = [pltpu.make_async_copy(src, dst, wsem.at[i]) for i, (src, dst) in enumerate(
            ((wout_hbm, wout_ref), (wup_hbm, wup_ref), (wdn_hbm, wdn_ref), (wgate_hbm, wgate_ref),
             (wproj_hbm, wproj_ref)))]
        for c in copies:
            c.start()
        for c in copies:
            c.wait()

    tm = h_ref.shape[0]
    u = u_ref[...]
    if halo:
        first = (pl.program_id(0) % halo) == 0
        prev = jnp.where(first, 0.0, up_ref[...])
        uext_ref[0:8, :] = prev
        uext_ref[8:tm + 8, :] = u
        u2 = uext_ref[6:tm + 6, :]
        u1 = uext_ref[7:tm + 7, :]
    else:
        u2 = up_ref[0]
        u1 = up_ref[1]
    yc = cw_ref[0:1, :] * u2 + cw_ref[1:2, :] * u1 + cw_ref[2:3, :] * u
    mix_a = _rms_rows(o_ref[...], on_ref[:, 0:ATT_WIDTH]).astype(_BF)
    mix_c = _rms_rows(cb_ref[...] * yc, on_ref[:, ATT_WIDTH:]).astype(_BF)
    h = h_ref[...] + _dot(mix_a, wout_ref[0:ATT_WIDTH, :]) + _dot(mix_c, wout_ref[ATT_WIDTH:, :])
    a = _rms_rows(h, nmlp_ref[...]).astype(_BF)
    y_ref[...] = h
    for c in range(D_FF // FF_CHUNK):
        t = jnp.maximum(_dot(a, wup_ref[:, c * FF_CHUNK:(c + 1) * FF_CHUNK]), 0.0)
        y_ref[...] += _dot((t * t).astype(_BF), wdn_ref[c * FF_CHUNK:(c + 1) * FF_CHUNK, :])
    h = y_ref[...]
    a = _rms_rows(h, nple_ref[...]).astype(_BF)
    gate = jax.nn.sigmoid(_dot(a, wgate_ref[...]))
    y_ref[...] = h + gate * _dot(p_ref[...].astype(_BF), wproj_ref[...])


def _tail(h, o_att, cb, u, u_prev, p, w, seq_len):
    n = h.shape[0]
    tm = min(ROW_TILE, n)
    row = lambda c: pl.BlockSpec((tm, c), lambda i: (i, 0))
    const = lambda a: pl.BlockSpec(a.shape, lambda i: (0,) * a.ndim)
    if seq_len is not None:
        halo = seq_len // tm
        up_spec = pl.BlockSpec((8, CONV_DIM), lambda i: (jnp.maximum(i * (tm // 8) - 1, 0), 0))
        up = u
    else:
        halo = 0
        up_spec = pl.BlockSpec((2, tm, CONV_DIM), lambda i: (0, i, 0))
        up = u_prev
    consts = (w['conv_w'], w['out_norm'], w['norm_mlp'], w['norm_ple'])
    mats = (w['w_out'], w['w_up'], w['w_down'], w['w_ple_gate'], w['w_ple_proj'])
    return pl.pallas_call(
        functools.partial(_tail_body, halo),
        grid=(n // tm,),
        in_specs=[row(D_MODEL), row(ATT_WIDTH), row(CONV_DIM), row(CONV_DIM), up_spec, row(PLE_DIM)]
        + [const(c) for c in consts] + [pl.BlockSpec(memory_space=pl.ANY)] * len(mats),
        out_specs=row(D_MODEL),
        out_shape=jax.ShapeDtypeStruct((n, D_MODEL), _F32),
        scratch_shapes=[pltpu.VMEM((tm + 8, CONV_DIM), _F32)] + [pltpu.VMEM(m.shape, _BF) for m in mats]
        + [pltpu.SemaphoreType.DMA((len(mats),))],
        compiler_params=pltpu.CompilerParams(dimension_semantics=("arbitrary",), vmem_limit_bytes=VMEM_LIMIT),
        name="tail",
    )(h, o_att, cb, u, up, p, *consts, *mats)


def _split_heads(x, lane_lo):
    xr = pltpu.roll(x, HEAD_DIM, axis=1)
    zero = jnp.zeros_like(x)
    a = (jnp.where(lane_lo, x, zero), jnp.where(lane_lo, xr, zero))
    b = (jnp.where(lane_lo, zero, xr), jnp.where(lane_lo, zero, x))
    return a, b


def _compress_rows(load, ns, bd_ref, pe_ref):
    acc = [jnp.zeros((2 * ns, LANES), _F32), jnp.zeros((2 * ns, LANES), _F32)]
    for l in range(CMP_BLOCK):
        for plane in range(2):
            x = jnp.concatenate([load(plane, l), load(plane, CMP_BLOCK + l)], axis=0)
            x = x + pe_ref[l, plane:plane + 1, :]
            acc[plane] = acc[plane] + _dot(x.astype(_BF), bd_ref[l, plane])
    return acc


def _compress_prompt_body(slab_ref, bd_ref, pe_ref, kg_ref, g128_ref, kA_ref, kB_ref, vA_ref, vB_ref):
    ns = slab_ref.shape[2] // SEL_BLOCK
    k, v = _compress_rows(lambda plane, t0: slab_ref[plane, 0, pl.ds(t0, ns, stride=SEL_BLOCK), :],
                          ns, bd_ref, pe_ref)
    k = _group_rms(k, g128_ref[...], kg_ref[...])
    lane_lo = lax.broadcasted_iota(jnp.int32, k.shape, 1) < HEAD_DIM
    ka, kb = _split_heads(k, lane_lo)
    va, vb = _split_heads(v, lane_lo)
    pad = jnp.zeros((LANES - 2 * ns, LANES), _BF)
    for g in range(N_KV):
        for ref, val in ((kA_ref, ka[g]), (kB_ref, kb[g]), (vA_ref, va[g]), (vB_ref, vb[g])):
            ref[0, g, 0:2 * ns, :] = val.astype(_BF)
            if 2 * ns < LANES:
                ref[0, g, 2 * ns:, :] = pad


def _compress_prompt(slab, w):
    _, b, t, _ = slab.shape
    assert t % QBLK == 0 and t // CMP_BLOCK <= LANES
    full = lambda a: pl.BlockSpec(a.shape, lambda i: (0,) * a.ndim)
    consts = (w['bd'], w['pe_t'], w['kc_gain'], w['g128'])
    out = jax.ShapeDtypeStruct((b, N_KV, LANES, LANES), _BF)
    return pl.pallas_call(
        _compress_prompt_body,
        grid=(b,),
        in_specs=[pl.BlockSpec((2, 1, t, LANES), lambda i: (0, i, 0, 0))] + [full(c) for c in consts],
        out_specs=[pl.BlockSpec((1, N_KV, LANES, LANES), lambda i: (i, 0, 0, 0))] * 4,
        out_shape=[out] * 4,
        compiler_params=pltpu.CompilerParams(dimension_semantics=("arbitrary",), vmem_limit_bytes=VMEM_LIMIT),
        name="compress_prompt",
    )(slab, *consts)


BUILD_ROWS = 256
QSTEP = 2


def _top_extra(impb, cand, lane_f, n_extra):
    v = jnp.where(cand, impb, -1.0)
    picked = jnp.zeros(impb.shape, _F32)
    for _ in range(n_extra):
        mx = jnp.max(v, axis=-1, keepdims=True)
        first = jnp.min(jnp.where(v == mx, lane_f, 1e9), axis=-1, keepdims=True)
        hit = lane_f == first
        picked = jnp.where(hit, 1.0, picked)
        v = jnp.where(hit, -1.0, v)
    return picked


def _attn_prompt_body(q_ref, ks_ref, kw_ref, gate_ref, kcA_ref, kcB_ref, vcA_ref, vcB_ref, cbias_ref, tiles_ref,
                      eg_ref, o_ref, ksA, ksB, vsA, vsB, kwA, kwB, vwA, vwB, s_scr, w_scr, m_scr, acc_scr):
    step = pl.program_id(1)
    t_len = ks_ref.shape[1]
    n_sel = t_len // SEL_BLOCK

    @pl.when(step == 0)
    def _build():
        def chunk(c, carry):
            r0 = pl.multiple_of(c * BUILD_ROWS, BUILD_ROWS)
            rows = pl.ds(r0, BUILD_ROWS)
            lane = lax.broadcasted_iota(jnp.int32, (BUILD_ROWS, LANES), 1)
            blk = (r0 + lax.broadcasted_iota(jnp.int32, (BUILD_ROWS, LANES), 0)) // SEL_BLOCK
            lane_lo = lane < HEAD_DIM
            oh_hi = jnp.where(lane == blk + HEAD_DIM, 1.0, 0.0)
            oh_lo = jnp.where(lane == blk, 1.0, 0.0)
            one_hi = jnp.where(lane == HEAD_DIM, 1.0, 0.0)
            one_lo = jnp.where(lane == 0, 1.0, 0.0)
            for src, k_a, k_b, v_a, v_b, onehot in ((ks_ref, ksA, ksB, vsA, vsB, True),
                                                    (kw_ref, kwA, kwB, vwA, vwB, False)):
                ka, kb = _split_heads(src[0, rows, 0:128], lane_lo)
                va, vb = _split_heads(src[0, rows, 128:256], lane_lo)
                for g in range(N_KV):
                    if onehot:
                        k_a[g, rows, :] = jnp.where(lane_lo, ka[g], oh_hi).astype(_BF)
                        k_b[g, rows, :] = jnp.where(lane_lo, oh_lo, kb[g]).astype(_BF)
                    else:
                        k_a[g, rows, :] = ka[g].astype(_BF)
                        k_b[g, rows, :] = kb[g].astype(_BF)
                    v_a[g, rows, :] = jnp.where(lane_lo, va[g], one_hi).astype(_BF)
                    v_b[g, rows, :] = jnp.where(lane_lo, one_lo, vb[g]).astype(_BF)
            return carry
        lax.fori_loop(0, t_len // BUILD_ROWS, chunk, 0)

    lane2 = lax.broadcasted_iota(jnp.int32, (2 * QBLK, LANES), 1)
    lo2 = lane2 < HEAD_DIM
    lane1 = lax.broadcasted_iota(jnp.int32, (QBLK, LANES), 1)
    lane1_f = lane1.astype(_F32)
    row1 = lax.broadcasted_iota(jnp.int32, (QBLK, LANES), 0)
    halves = range(QSTEP)
    qbs = [step * QSTEP + h for h in halves]
    qb_last = qbs[-1]
    rows_of = [slice(h * QBLK, (h + 1) * QBLK) for h in halves]
    curs = [(qb * QBLK + row1) // SEL_BLOCK for qb in qbs]

    def normalize(acc, x):
        l = jnp.sum(jnp.where(lane2 == (HEAD_DIM if x == 0 else 0), acc, 0.0), axis=-1, keepdims=True)
        keep = lo2 if x == 0 else jnp.logical_not(lo2)
        return jnp.where(keep, acc / l, 0.0)

    chains = [(h, g, x) for h in halves for g in range(N_KV) for x in range(2)]
    kv_chains = [(g, x) for g in range(N_KV) for x in range(2)]
    o_cmp, q_plain, q_pair, importance = [], [], [], []
    for h in halves:
        for g in range(N_KV):
            qs = jnp.concatenate([q_ref[0, rows_of[h], (2 * g) * LANES:(2 * g + 1) * LANES],
                                  q_ref[0, rows_of[h], (2 * g + 1) * LANES:(2 * g + 2) * LANES]], axis=0)
            zero = jnp.zeros_like(qs)
            q_a = jnp.where(lo2, qs, zero)
            q_b = jnp.where(lo2, zero, qs)
            qs2 = jnp.concatenate([q_ref[0, rows_of[h], ATT_WIDTH + (2 * g) * LANES:ATT_WIDTH + (2 * g + 1) * LANES],
                                   q_ref[0, rows_of[h], ATT_WIDTH + (2 * g + 1) * LANES:ATT_WIDTH + (2 * g + 2) * LANES]],
                                  axis=0)
            q_pair.append(qs2)
            q_plain += [jnp.where(lo2, qs2, zero), jnp.where(lo2, zero, qs2)]

            def cmp_probs(qx, k_ref, x):
                s = _dot_nt(qx, k_ref[0, g]) + cbias_ref[h, g, x]
                m = jnp.max(s, axis=-1, keepdims=True)
                e = jnp.where(s > 0.5 * NEG, jnp.exp(s - m), 0.0)
                l = jnp.sum(e, axis=-1, keepdims=True)
                return e / jnp.where(l > 0.0, l, 1.0)
            p_a = cmp_probs(q_a, kcA_ref, 0)
            p_b = cmp_probs(q_b, kcB_ref, 1)
            o_cmp.append(_dot(p_a.astype(_BF), vcA_ref[0, g]) + _dot(p_b.astype(_BF), vcB_ref[0, g]))
            imp = p_a[0:QBLK] + p_a[QBLK:] + p_b[0:QBLK] + p_b[QBLK:]
            importance.append(imp + pltpu.roll(imp, LANES - n_sel, axis=1))

    two = lambda a: jnp.concatenate([a, a], axis=0)
    cands = jnp.concatenate([two((lane1 >= 1) & (lane1 <= curs[h] - 2)) for h in halves], axis=0)
    picked = _top_extra(jnp.concatenate(importance, axis=0), cands,
                        jnp.concatenate([lane1_f] * (2 * QSTEP), axis=0), TOP_N - 3)
    q_aug = []
    for h in halves:
        cur = curs[h]
        forced = (lane1 == 0) | (lane1 == cur) | (lane1 == cur - 1)
        few = cur <= TOP_N - 1
        for g in range(N_KV):
            i = h * N_KV + g
            chosen = forced | (few & (lane1 <= cur)) | ((picked[i * QBLK:(i + 1) * QBLK] > 0.5) & jnp.logical_not(few))
            sb_lo = jnp.where(chosen | (lane1 >= n_sel), 0.0, NEG)
            sb_hi = pltpu.roll(sb_lo, HEAD_DIM, axis=1)
            q_aug += [jnp.where(lo2, q_pair[i], two(sb_hi).astype(_BF)),
                      jnp.where(lo2, two(sb_lo).astype(_BF), q_pair[i])]

    k_sel, v_sel, k_win, v_win = (ksA, ksB), (vsA, vsB), (kwA, kwB), (vwA, vwB)
    n_win = WINDOW // QBLK + 1

    w_firsts = [jnp.maximum(qb - (n_win - 1), 0) for qb in qbs]
    w_rows = [pl.ds(pl.multiple_of(wf * QBLK, QBLK), n_win * QBLK) for wf in w_firsts]
    m_win = []
    for ch, (h, g, x) in enumerate(chains):
        s = _dot_nt(q_plain[ch], k_win[x][g, w_rows[h], :])
        m = None
        for i in range(n_win):
            idx = qbs[h] - w_firsts[h] - i
            sb = s[:, i * QBLK:(i + 1) * QBLK] + tiles_ref[jnp.where(idx < 0, n_win, idx), g, x]
            w_scr[ch, :, i * QBLK:(i + 1) * QBLK] = sb
            m = sb if m is None else jnp.maximum(m, sb)
        m_win.append(jnp.max(m, axis=-1, keepdims=True))
    o_win = []
    for ch, (h, g, x) in enumerate(chains):
        p = jnp.exp2(w_scr[ch] - m_win[ch]).astype(_BF)
        o_win.append(normalize(_dot(p, v_win[x][g, w_rows[h], :]), x))

    n_quart = 4
    qw = t_len // n_quart
    per_q = qw // QBLK
    m_scr[...] = jnp.full(m_scr.shape, NEG, _F32)
    for qi in range(n_quart):
        @pl.when(qb_last * QBLK >= qi * qw)
        def _scores(qi=qi):
            for g, x in kv_chains:
                chs = [chains.index((h, g, x)) for h in halves]
                s_all = _dot_nt(jnp.concatenate([q_aug[ch] for ch in chs], axis=0),
                                k_sel[x][g, qi * qw:(qi + 1) * qw, :])
                for h, ch in zip(halves, chs):
                    s = s_all[h * 2 * QBLK:(h + 1) * 2 * QBLK]
                    m = m_scr[ch]
                    for ci in range(per_q):
                        c = qi * per_q + ci
                        idx = qbs[h] - c
                        sb = s[:, ci * QBLK:(ci + 1) * QBLK] + tiles_ref[jnp.where(idx < 0, n_win, jnp.minimum(idx, 2)), g, x]
                        s_scr[ch, :, c * QBLK:(c + 1) * QBLK] = sb
                        m = jnp.maximum(m, sb)
                    m_scr[ch] = m
    m_sel = [jnp.max(m_scr[ch], axis=-1, keepdims=True) for ch in range(len(chains))]

    acc_scr[...] = jnp.zeros(acc_scr.shape, _F32)
    for qi in range(n_quart):
        @pl.when(qb_last * QBLK >= qi * qw)
        def _weighted(qi=qi):
            for g, x in kv_chains:
                chs = [chains.index((h, g, x)) for h in halves]
                p = jnp.concatenate([jnp.exp2(s_scr[ch, :, qi * qw:(qi + 1) * qw] - m_sel[ch]).astype(_BF)
                                     for ch in chs], axis=0)
                pv = _dot(p, v_sel[x][g, qi * qw:(qi + 1) * qw, :])
                for h, ch in zip(halves, chs):
                    acc_scr[ch] += pv[h * 2 * QBLK:(h + 1) * 2 * QBLK]

    gates = gate_ref[0]
    g_hi = gates.astype(_BF)
    g_lo = (gates - g_hi.astype(_F32)).astype(_BF)
    gexp = _dot(g_hi, eg_ref[...]) + _dot(g_lo, eg_ref[...])
    for h in halves:
        for g in range(N_KV):
            i = h * N_KV + g
            o_c = o_cmp[i]
            o_s = normalize(acc_scr[2 * i], 0) + normalize(acc_scr[2 * i + 1], 1)
            o_w = o_win[2 * i] + o_win[2 * i + 1]
            for pr in range(2):
                rows = slice(pr * QBLK, (pr + 1) * QBLK)
                col = (2 * g + pr) * LANES
                gx = lambda br: gexp[rows_of[h], br * ATT_WIDTH + col:br * ATT_WIDTH + col + LANES]
                o_ref[0, rows_of[h], col:col + LANES] = gx(0) * o_c[rows] + gx(1) * o_s[rows] + gx(2) * o_w[rows]


def _attn_prompt(q, sel, win, gates, kc, cbias, w):
    b, t, _ = sel.shape
    nq = t // QBLK
    assert t // SEL_BLOCK <= CMP_BLOCK and t % (4 * QBLK) == 0 and t >= WINDOW + QBLK and nq % QSTEP == 0
    kcA, kcB, vcA, vcB = kc
    full = lambda a: pl.BlockSpec(a.shape, lambda i, j: (0,) * a.ndim)
    slab = pl.BlockSpec((1, t, 256), lambda i, j: (i, 0, 0))
    kcs = pl.BlockSpec((1, N_KV, LANES, LANES), lambda i, j: (i, 0, 0, 0))
    n_chain = QSTEP * 2 * N_KV
    q_rows = QSTEP * QBLK
    scratch = [pltpu.VMEM((N_KV, t, LANES), _BF)] * 8 + [
        pltpu.VMEM((n_chain, 2 * QBLK, t), _F32), pltpu.VMEM((n_chain, 2 * QBLK, WINDOW + QBLK), _F32),
        pltpu.VMEM((n_chain, 2 * QBLK, LANES), _F32), pltpu.VMEM((n_chain, 2 * QBLK, LANES), _F32)]
    return pl.pallas_call(
        _attn_prompt_body,
        grid=(b, nq // QSTEP),
        in_specs=[pl.BlockSpec((1, q_rows, 2 * ATT_WIDTH), lambda i, j: (i, j, 0)), slab, slab,
                  pl.BlockSpec((1, q_rows, LANES), lambda i, j: (i, j, 0)), kcs, kcs, kcs, kcs,
                  pl.BlockSpec((QSTEP, N_KV, 2, 2 * QBLK, LANES), lambda i, j: (j, 0, 0, 0, 0)),
                  full(w['tiles']), full(w['egate'])],
        out_specs=pl.BlockSpec((1, q_rows, ATT_WIDTH), lambda i, j: (i, j, 0)),
        out_shape=jax.ShapeDtypeStruct((b, t, ATT_WIDTH), _F32),
        scratch_shapes=scratch,
        compiler_params=pltpu.CompilerParams(dimension_semantics=("arbitrary", "arbitrary"),
                                             vmem_limit_bytes=VMEM_LIMIT),
        name="attn_prompt",
    )(q, sel, win, gates, kcA, kcB, vcA, vcB, cbias, w['tiles'], w['egate'])


N_PICK = TOP_N - 3
N_SLOT = TOP_N - 1
PAGE_ROWS = 2 * KV_WIDTH


PAIR_TOKENS = 2 * PAGE_SIZE


def _sample_cmp_body(pt_ref, q_ref, cache_ref, bd_ref, pet_ref, perm_ref, kg_ref, g128_ref, sb_ref, oc_ref, pick_ref,
                     buf, rows_scr, sem):
    b = pl.program_id(0)
    nb = pl.num_programs(0)
    n_pairs = pt_ref.shape[1] // 2
    nc = 8 * n_pairs
    slot = b % 2

    def page_copy(bb, pair, half, sl):
        return pltpu.make_async_copy(cache_ref.at[pt_ref[bb, 2 * pair + half]],
                                     buf.at[sl, pair, :, half * PAGE_SIZE:(half + 1) * PAGE_SIZE], sem.at[sl])

    def for_pages(fn):
        def step(pair, c):
            fn(pair, 0)
            fn(pair, 1)
            return c
        lax.fori_loop(0, n_pairs, step, 0)

    @pl.when(b == 0)
    def _prime():
        for_pages(lambda pair, half: page_copy(0, pair, half, 0).start())

    @pl.when(b + 1 < nb)
    def _prefetch():
        for_pages(lambda pair, half: page_copy(b + 1, pair, half, 1 - slot).start())

    for_pages(lambda pair, half: page_copy(b, pair, half, slot).wait())

    def regroup(pair, c):
        x = (buf[slot, pair] + pet_ref[...]).astype(_BF)
        rows_scr[pair] = _dot_nt(perm_ref[...], x)
        return c
    lax.fori_loop(0, n_pairs, regroup, 0, unroll=16)

    x_all = jnp.concatenate([rows_scr[:, 8 * l:8 * (l + 1), :].reshape(nc, 2 * KV_WIDTH).astype(_BF)
                             for l in range(CMP_BLOCK)], axis=1)
    acc = _dot(x_all, bd_ref[...].reshape(CMP_BLOCK * 2 * KV_WIDTH, 2 * KV_WIDTH))
    k = _group_rms(acc[:, 0:KV_WIDTH], g128_ref[...], kg_ref[...]).astype(_BF)
    v = acc[:, KV_WIDTH:].astype(_BF)
    s = _dot_nt(q_ref[0], k) + sb_ref[...]
    m = jnp.max(s, axis=-1, keepdims=True)
    e = jnp.exp(s - m)
    p = e / jnp.sum(e, axis=-1, keepdims=True)
    oc_ref[0] = _dot(p.astype(_BF), v)
    imp8 = p + pltpu.roll(p, nc - 1, axis=1)
    rows = [imp8[h:h + 1] for h in range(N_HEADS)]
    imp_g = [rows[g * GROUP] + rows[g * GROUP + 1] + rows[g * GROUP + 2] + rows[g * GROUP + 3] for g in range(N_KV)]
    imp = jnp.concatenate(imp_g + [jnp.full((8 - N_KV, nc), -1.0, _F32)], axis=0)
    lane = lax.broadcasted_iota(jnp.int32, (8, nc), 1)
    lane_f = lane.astype(_F32)
    v_c = jnp.where((lane % 2 == 0) & (lane >= 2) & (lane <= nc - 4), imp, -1.0)
    picks = jnp.zeros((8, nc), _F32)
    for i in range(N_PICK):
        mx = jnp.max(v_c, axis=-1, keepdims=True)
        first = jnp.min(jnp.where(v_c == mx, lane_f, 1e9), axis=-1, keepdims=True)
        picks = jnp.where(lane == i, first * 0.5, picks)
        v_c = jnp.where(lane_f == first, -1.0, v_c)
    pick_ref[0] = picks.astype(jnp.int32)


def _sample_cmp(page_table, q8, cache, sbias, w):
    db, n_pages = page_table.shape
    nc = 4 * n_pages
    assert n_pages % 2 == 0 and nc // 2 - 2 >= N_PICK
    full = lambda a: pl.BlockSpec(a.shape, lambda i, pt: (0,) * a.ndim)
    consts = (w['bd_kv'], w['pe_tok'], w['perm'], w['kc_gain'], w['g128'], sbias)
    grid_spec = pltpu.PrefetchScalarGridSpec(
        num_scalar_prefetch=1,
        grid=(db,),
        in_specs=[pl.BlockSpec((1, 8, LANES), lambda i, pt: (i, 0, 0)), pl.BlockSpec(memory_space=pl.ANY)]
        + [full(c) for c in consts],
        out_specs=[pl.BlockSpec((1, 8, LANES), lambda i, pt: (i, 0, 0)),
                   pl.BlockSpec((1, 8, nc), lambda i, pt: (i, 0, 0))],
        scratch_shapes=[pltpu.VMEM((2, n_pages // 2, 2 * KV_WIDTH, PAIR_TOKENS), _F32),
                        pltpu.VMEM((n_pages // 2, PAIR_TOKENS, 2 * KV_WIDTH), _F32),
                        pltpu.SemaphoreType.DMA((2,))],
    )
    return pl.pallas_call(
        _sample_cmp_body,
        grid_spec=grid_spec,
        out_shape=[jax.ShapeDtypeStruct((db, 8, LANES), _F32), jax.ShapeDtypeStruct((db, 8, nc), jnp.int32)],
        compiler_params=pltpu.CompilerParams(dimension_semantics=("arbitrary",), vmem_limit_bytes=VMEM_LIMIT),
        name="sample_cmp",
    )(page_table, q8, cache, *consts)


def _sample_attn_body(pt_ref, pick_ref, q_ref, cache_ref, win_ref, nsel_ref, nwin_ref, ncol_ref, gate_ref, oc_ref,
                      bsel_ref, b0_ref, bwin_ref, o_ref, wout_ref, kb, sem):
    b = pl.program_id(0)
    nb = pl.num_programs(0)
    nj = 2 * pt_ref.shape[1]
    wlen = win_ref.shape[2]
    slot = b % 2

    def block_of(bb, g, s):
        if s == 0:
            return 0
        if s == N_SLOT - 1:
            return nj - 1
        return pick_ref[bb, g * N_PICK + (s - 1)]

    def tile_copy(bb, g, s, plane, sl):
        page = pt_ref[bb, block_of(bb, g, s) // 2]
        return pltpu.make_async_copy(cache_ref.at[page, plane * N_KV + g],
                                     kb.at[sl, g, plane, :, pl.ds(s * PAGE_SIZE, PAGE_SIZE)], sem.at[sl])

    def for_tiles(fn):
        for g in range(N_KV):
            for s in range(N_SLOT):
                for plane in range(2):
                    fn(g, s, plane)

    @pl.when(b == 0)
    def _prime():
        for_tiles(lambda g, s, plane: tile_copy(0, g, s, plane, 0).start())

    @pl.when(b + 1 < nb)
    def _prefetch():
        for_tiles(lambda g, s, plane: tile_copy(b + 1, g, s, plane, 1 - slot).start())

    q8 = q_ref[0]
    q8f = q8.astype(_F32)
    head_g = lax.broadcasted_iota(jnp.int32, (8, HEAD_DIM), 0) // GROUP
    own_half = lambda x: jnp.where(head_g == 0, x[:, 0:HEAD_DIM], x[:, HEAD_DIM:])
    b0 = b0_ref[:, 0:1]

    x = win_ref[0]
    lane_w = lax.broadcasted_iota(jnp.int32, x.shape, 1)
    wout_ref[0] = jnp.where(lane_w == wlen - 1, ncol_ref[0], pltpu.roll(x, wlen - 1, axis=1))
    sw = _dot(q8, x[0:KV_WIDTH, :].astype(_BF)) + bwin_ref[...]
    sw_new = jnp.sum(q8f * nwin_ref[0][:, 0:KV_WIDTH], axis=-1, keepdims=True) + b0
    mw = jnp.maximum(jnp.max(sw, axis=-1, keepdims=True), sw_new)
    ew = jnp.exp(sw - mw)
    ew_new = jnp.exp(sw_new - mw)
    o_w = _dot_nt(ew.astype(_BF), x[KV_WIDTH:, :].astype(_BF)) + ew_new * nwin_ref[0][:, KV_WIDTH:]
    o_w = own_half(o_w / (jnp.sum(ew, axis=-1, keepdims=True) + ew_new))

    for_tiles(lambda g, s, plane: tile_copy(b, g, s, plane, slot).wait())

    ss_new = jnp.sum(q8f * nsel_ref[0][:, 0:KV_WIDTH], axis=-1, keepdims=True) + b0
    lane_half = lax.broadcasted_iota(jnp.int32, (8, PAGE_SIZE), 1) // SEL_BLOCK
    o_sel = []
    for g in range(N_KV):
        pieces = []
        for s in range(N_SLOT):
            j = jnp.full((8, PAGE_SIZE), block_of(b, g, s), jnp.int32)
            if s == 0:
                tab = bsel_ref[0]
            elif s == N_SLOT - 1:
                tab = bsel_ref[2]
            else:
                tab = jnp.where(j == nj - 2, bsel_ref[1], bsel_ref[0])
            pieces.append(jnp.where(lane_half == j % 2, tab, NEG))
        ss = _dot(q8[:, g * HEAD_DIM:(g + 1) * HEAD_DIM], kb[slot, g, 0].astype(_BF)) + jnp.concatenate(pieces, axis=1)
        ms = jnp.maximum(jnp.max(ss, axis=-1, keepdims=True), ss_new)
        es = jnp.exp(ss - ms)
        es_new = jnp.exp(ss_new - ms)
        v_new = nsel_ref[0][:, KV_WIDTH + g * HEAD_DIM:KV_WIDTH + (g + 1) * HEAD_DIM]
        o_g = _dot_nt(es.astype(_BF), kb[slot, g, 1].astype(_BF)) + es_new * v_new
        o_sel.append(o_g / (jnp.sum(es, axis=-1, keepdims=True) + es_new))
    o_s = jnp.where(head_g == 0, o_sel[0], o_sel[1])
    gates = gate_ref[0]
    o_ref[0] = gates[:, 0:1] * own_half(oc_ref[0]) + gates[:, 1:2] * o_s + gates[:, 2:3] * o_w


def _sample_attn(page_table, picks, q8, cache, win, new_sel, new_win, gates8, o_c, tabs):
    db, n_pages = page_table.shape
    wlen = win.shape[2]
    full = lambda a: pl.BlockSpec(a.shape, lambda i, pt, pk: (0,) * a.ndim)
    per_b = lambda r, c: pl.BlockSpec((1, r, c), lambda i, pt, pk: (i, 0, 0))
    bsel, b0, bwin = tabs
    grid_spec = pltpu.PrefetchScalarGridSpec(
        num_scalar_prefetch=2,
        grid=(db,),
        in_specs=[per_b(8, LANES), pl.BlockSpec(memory_space=pl.ANY), per_b(2 * KV_WIDTH, wlen), per_b(1, 256),
                  per_b(1, 256), per_b(2 * KV_WIDTH, 1), per_b(8, LANES), per_b(8, LANES),
                  full(bsel), full(b0), full(bwin)],
        out_specs=[per_b(8, HEAD_DIM), per_b(2 * KV_WIDTH, wlen)],
        scratch_shapes=[pltpu.VMEM((2, N_KV, 2, HEAD_DIM, N_SLOT * PAGE_SIZE), _F32), pltpu.SemaphoreType.DMA((2,))],
    )
    return pl.pallas_call(
        _sample_attn_body,
        grid_spec=grid_spec,
        out_shape=[jax.ShapeDtypeStruct((db, 8, HEAD_DIM), _F32), jax.ShapeDtypeStruct((db, 2 * KV_WIDTH, wlen), _F32)],
        compiler_params=pltpu.CompilerParams(dimension_semantics=("arbitrary",), vmem_limit_bytes=VMEM_LIMIT),
        name="sample_attn",
    )(page_table, picks, q8, cache, win, new_sel, new_win, new_win[:, 0, :, None], gates8, o_c, bsel, b0, bwin)


def _rel_bucket(dist):
    n = np.maximum(np.asarray(dist), 0)
    max_exact = RP_BUCKETS // 2
    nf = np.maximum(n, 1).astype(np.float32)
    large = max_exact + (np.log(nf / np.float32(max_exact)) / np.float32(math.log(RP_MAX_DIST / max_exact))
                         * np.float32(RP_BUCKETS - max_exact)).astype(np.int32)
    large = np.minimum(large, RP_BUCKETS - 1)
    return np.where(n < max_exact, n, large)


def _bias_of(rel_bias, dist):
    onehot = _rel_bucket(dist)[..., None, None] == np.arange(RP_BUCKETS)[:, None]
    return jnp.sum(jnp.where(onehot, rel_bias, 0.0), axis=-2)


def _head_major(x, lead):
    n = x.shape[-2]
    x = x.reshape(lead + (QBLK, n, N_KV, 2, 2))
    nl = len(lead)
    x = jnp.transpose(x, tuple(range(nl)) + (nl + 2, nl + 4, nl + 3, nl, nl + 1))
    return x.reshape(lead + (N_KV, 2, 2 * QBLK, n))


def _prompt_tables(rel_bias, t_len):
    nq = t_len // QBLK
    ns = t_len // SEL_BLOCK
    ti = np.arange(QBLK)[:, None]
    ki = np.arange(QBLK)[None, :]
    n_idx = WINDOW // QBLK + 1
    pad = QBLK - 1
    bvec = _bias_of(rel_bias, np.arange(-pad, t_len + pad))
    wins = jnp.stack([bvec[i * QBLK:i * QBLK + 2 * QBLK - 1][::-1] for i in range(n_idx)])
    skew = jnp.tile(wins, (1, QBLK + 1, 1))[:, :QBLK * 2 * QBLK].reshape(n_idx, QBLK, 2 * QBLK, N_HEADS)
    tiles = skew[:, ::-1, :QBLK]
    dist = np.stack([i * QBLK + ti - ki for i in range(n_idx)])
    valid = dist >= 0
    valid[n_idx - 1] &= dist[n_idx - 1] < WINDOW
    tiles = jnp.where(valid[..., None], tiles, NEG)
    tiles = jnp.concatenate([tiles, jnp.full_like(tiles[:1], NEG)])
    tiles = _head_major(tiles * LOG2E, (n_idx + 1,))
    shift = CMP_BLOCK * ns * 2
    bcmp = _bias_of(rel_bias, np.arange(-shift, t_len))
    cols = [bcmp[shift - (CMP_BLOCK * blk + CMP_BLOCK - 1):][:t_len]
            for blk in list(range(0, 2 * ns, 2)) + list(range(1, 2 * ns, 2))]
    cb = jnp.stack(cols, axis=1)
    cb = jnp.pad(cb, ((0, 0), (0, LANES - 2 * ns), (0, 0)))
    lane = np.arange(LANES)
    blk = np.where(lane < ns, 2 * lane, 2 * (lane - ns) + 1)
    dist_c = np.arange(t_len)[:, None] - (CMP_BLOCK * blk + CMP_BLOCK - 1)[None, :]
    valid_c = (dist_c >= 0) & (lane < 2 * ns)[None, :]
    cb = jnp.where(valid_c[..., None], cb, NEG)
    cb = _head_major(cb.reshape(nq, QBLK, LANES, N_HEADS), (nq,))
    return tiles, cb


def _sample_tables(rel_bias, past_len, wlen):
    blk = np.arange(past_len // CMP_BLOCK)
    sbias = _bias_of(rel_bias, past_len - (CMP_BLOCK * blk + CMP_BLOCK - 1)).T
    pos = np.arange(PAGE_SIZE) % SEL_BLOCK
    bsel = jnp.stack([_bias_of(rel_bias, np.full(PAGE_SIZE, past_len)).T,
                      _bias_of(rel_bias, 2 * SEL_BLOCK - pos).T,
                      _bias_of(rel_bias, SEL_BLOCK - pos).T])
    b0 = _bias_of(rel_bias, np.zeros((LANES,), np.int32)).T
    tok = np.arange(wlen)
    bwin = jnp.where((tok >= 1)[None, :], _bias_of(rel_bias, wlen - tok).T, NEG)
    return sbias, (bsel, b0, bwin)


def _prep(norm_mix, w_in, q_norm, k_norm, cmp_pe, w_cmp, conv_w, out_norm, w_out, norm_mlp, w_up, w_down,
          norm_ple, w_ple_gate, w_ple_proj):
    w_in = w_in[0]
    o_kv = ATT_WIDTH
    o_g = o_kv + 6 * KV_WIDTH
    o_c = o_g + 3 * N_HEADS
    bd = jnp.einsum('gh,plde->lpgdhe', jnp.eye(N_KV, dtype=_F32), w_cmp[0]).reshape(CMP_BLOCK, 2, LANES, LANES)
    pe_t = jnp.transpose(jnp.tile(cmp_pe[0], (1, 1, N_KV)), (1, 0, 2))
    egate = np.zeros((LANES, 3 * ATT_WIDTH), np.float32)
    for br in range(3):
        for h in range(N_HEADS):
            egate[br * N_HEADS + h, br * ATT_WIDTH + h * HEAD_DIM:br * ATT_WIDTH + (h + 1) * HEAD_DIM] = 1.0
    ones = np.ones((HEAD_DIM, HEAD_DIM), np.float32)
    perm = np.zeros((PAIR_TOKENS, PAIR_TOKENS), np.float32)
    tok = np.arange(PAIR_TOKENS)
    perm[(tok % CMP_BLOCK) * (PAIR_TOKENS // CMP_BLOCK) + tok // CMP_BLOCK, tok] = 1.0
    pe_tok = jnp.tile(jnp.transpose(cmp_pe[0], (0, 2, 1)), (1, N_KV, PAIR_TOKENS // CMP_BLOCK))
    pe_tok = pe_tok.reshape(2 * KV_WIDTH, PAIR_TOKENS)
    zero = jnp.zeros_like(bd[:, 0])
    bd_kv = jnp.concatenate([jnp.concatenate([bd[:, 0], zero], axis=2),
                             jnp.concatenate([zero, bd[:, 1]], axis=2)], axis=1)
    return {
        'perm': jnp.asarray(perm, _BF), 'pe_tok': pe_tok, 'bd_kv': bd_kv.astype(_BF),
        'norm_mix': norm_mix[0][None], 'wq': w_in[:, :o_kv].astype(_BF), 'wkv': w_in[:, o_kv:o_g].astype(_BF),
        'wg': jnp.pad(w_in[:, o_g:o_c], ((0, 0), (0, LANES - 3 * N_HEADS))).astype(_BF),
        'wc': w_in[:, o_c:].astype(_BF),
        'q_gain': jnp.tile(q_norm[0], N_HEADS)[None],
        'k_gain': jnp.stack([jnp.tile(k_norm[0, 1], N_KV), jnp.tile(k_norm[0, 2], N_KV)]),
        'kc_gain': jnp.tile(k_norm[0, 0], N_KV)[None],
        'g512': jnp.asarray(np.kron(np.eye(N_HEADS, dtype=np.float32), ones), _BF),
        'g128': jnp.asarray(np.kron(np.eye(N_KV, dtype=np.float32), ones), _BF),
        'bd': bd.astype(_BF), 'pe_t': pe_t,
        'egate': jnp.asarray(egate, _BF),
        'conv_w': conv_w[0], 'out_norm': out_norm[0][None], 'w_out': w_out[0].astype(_BF),
        'norm_mlp': norm_mlp[0][None], 'w_up': w_up[0].astype(_BF), 'w_down': w_down[0].astype(_BF),
        'norm_ple': norm_ple[0][None], 'w_ple_gate': w_ple_gate[0].astype(_BF),
        'w_ple_proj': w_ple_proj[0].astype(_BF),
    }


def kernel(x_prompt, x_sample, p_prompt, p_sample, cache_cmp_kv, cache_sel_kv, state_win_kv, state_conv, page_table, rel_bias, norm_mix, w_in, q_norm, k_norm, cmp_pe, w_cmp, conv_w, out_norm, w_out, norm_mlp, w_up, w_down, norm_ple, w_ple_gate, w_ple_proj):
    bp, tp, _ = x_prompt.shape
    db, ts, _ = x_sample.shape
    assert norm_mix.shape[0] == 1 and ts == 1 and tp >= WINDOW
    n_pages = page_table.shape[1]
    past_len = n_pages * PAGE_SIZE
    wlen = state_win_kv.shape[2]
    assert wlen == WINDOW and past_len >= 4 * SEL_BLOCK
    w = _prep(norm_mix, w_in, q_norm, k_norm, cmp_pe, w_cmp, conv_w, out_norm, w_out, norm_mlp, w_up, w_down,
              norm_ple, w_ple_gate, w_ple_proj)
    w['tiles'], cbias = _prompt_tables(rel_bias, tp)
    sbias, stabs = _sample_tables(rel_bias, past_len, wlen)
    kv6 = lambda a, b, t: a.reshape(1, b, t, 2, N_KV, HEAD_DIM)

    xp = x_prompt.reshape(bp * tp, D_MODEL)
    q, cmp_p, sel_p, win_p, gates, cb, u, cmp_t, sel_t, win_t = _inproj(xp, w, tp)
    seq = lambda a: a.reshape(bp, tp, a.shape[-1])
    kv6_t = lambda a: jnp.transpose(a.reshape(bp, 2, N_KV, HEAD_DIM, a.shape[-1]), (0, 4, 1, 2, 3))[None]
    kc = _compress_prompt(cmp_p.reshape(2, bp, tp, KV_WIDTH), w)
    o_att = _attn_prompt(seq(q), seq(sel_p), seq(win_p), seq(gates), kc, cbias, w)
    y_p = _tail(xp, o_att.reshape(bp * tp, ATT_WIDTH), cb, u, None, p_prompt[0].reshape(bp * tp, PLE_DIM), w, tp)

    xs = x_sample.reshape(db, D_MODEL)
    q_s, cmp_s, sel_s, win_s, gates_s, cb_s, u_s = _inproj(xs, w)
    qh = q_s[:, :ATT_WIDTH].reshape(db, N_KV, GROUP, HEAD_DIM)
    zq = jnp.zeros_like(qh[:, 0])
    q8 = jnp.concatenate([jnp.concatenate([qh[:, 0], zq], axis=-1), jnp.concatenate([zq, qh[:, 1]], axis=-1)], axis=1)
    n_phys = cache_cmp_kv.shape[1]
    token_minor = lambda a: jnp.transpose(a[0], (0, 2, 3, 4, 1))
    o_c, picks = _sample_cmp(page_table, q8, token_minor(cache_cmp_kv).reshape(n_phys, 2 * KV_WIDTH, PAGE_SIZE), sbias, w)
    picks = picks[:, :N_KV, :N_PICK].reshape(db, N_KV * N_PICK)
    gates8 = jnp.pad(jnp.transpose(gates_s[:, :3 * N_HEADS].reshape(db, 3, N_HEADS), (0, 2, 1)),
                     ((0, 0), (0, 0), (0, LANES - 3)))
    o8, win_new = _sample_attn(page_table, picks, q8,
                               token_minor(cache_sel_kv).reshape(n_phys, 2 * N_KV, HEAD_DIM, PAGE_SIZE),
                               token_minor(state_win_kv).reshape(db, 2 * KV_WIDTH, wlen),
                               sel_s[:, None, :], win_s[:, None, :], gates8, o_c, stabs)
    o_att_s = o8.reshape(db, ATT_WIDTH)
    win_new = jnp.transpose(win_new.reshape(db, 2, N_KV, HEAD_DIM, wlen), (0, 4, 1, 2, 3))
    u_prev = jnp.transpose(state_conv[0], (1, 0, 2))
    y_s = _tail(xs, o_att_s, cb_s, u_s, u_prev, p_sample[0].reshape(db, PLE_DIM), w, None)

    return (y_p.reshape(bp, tp, D_MODEL), y_s.reshape(db, 1, D_MODEL),
            kv6_t(cmp_t), kv6_t(sel_t), kv6_t(win_t),
            seq(u)[:, tp - 2:][None],
            kv6(jnp.concatenate([cmp_s[0], cmp_s[1]], axis=-1), db, 1), kv6(sel_s, db, 1), win_new[None],
            jnp.concatenate([state_conv[0][:, 1:], u_s[:, None, :]], axis=1)[None])
```

```python
import functools
import math

import jax
import jax.numpy as jnp
import numpy as np
from jax import lax
from jax.experimental import pallas as pl
from jax.experimental.pallas import tpu as pltpu

D_MODEL = 1024
HEAD_DIM = 64
N_HEADS = 8
N_KV = 2
GROUP = N_HEADS // N_KV
ATT_WIDTH = N_HEADS * HEAD_DIM
KV_WIDTH = N_KV * HEAD_DIM
CONV_DIM = D_MODEL - ATT_WIDTH
PAGE_SIZE = 128
CMP_BLOCK = 32
SEL_BLOCK = 64
TOP_N = 8
WINDOW = 512
RP_BUCKETS = 32
RP_MAX_DIST = 128
D_FF = 4 * D_MODEL
PLE_DIM = 256
SCALE = HEAD_DIM ** -0.5
NEG = -1e30
EPS = 1e-6
LOG2E = 1.4426950408889634

LANES = 128
QBLK = 128
ROW_TILE = 512
VMEM_LIMIT = 56 * 1024 * 1024

_BF = jnp.bfloat16
_F32 = jnp.float32


def _dot(a, b):
    return jnp.dot(a, b, preferred_element_type=_F32)


def _dot_nt(a, b):
    return lax.dot_general(a, b, (((1,), (1,)), ((), ())), preferred_element_type=_F32)


def _rms_rows(x, gain):
    return x * lax.rsqrt(jnp.mean(x * x, axis=-1, keepdims=True) + EPS) * gain


def _group_rms(z, gmat, gain):
    ssq = _dot((z * z).astype(_BF), gmat) * (1.0 / HEAD_DIM)
    return z * lax.rsqrt(ssq + EPS) * gain


def _inproj_body(x_ref, nm_ref, wq_ref, wkv_ref, wg_ref, wc_ref, qg_ref, kg_ref, g512_ref, g128_ref,
                 q_ref, cmp_ref, sel_ref, win_ref, gate_ref, cb_ref, u_ref, *kv_t_refs):
    a = _rms_rows(x_ref[...], nm_ref[...]).astype(_BF)
    zq = _dot(a, wq_ref[...])
    qn = _group_rms(zq, g512_ref[...], qg_ref[...]) * SCALE
    q_ref[:, 0:ATT_WIDTH] = qn.astype(_BF)
    q_ref[:, ATT_WIDTH:] = (qn * LOG2E).astype(_BF)
    zkv = _dot(a, wkv_ref[...])
    g128 = g128_ref[...]
    halves = ((zkv[:, 0:128], zkv[:, 128:256]),
              (_group_rms(zkv[:, 256:384], g128, kg_ref[0:1, :]), zkv[:, 384:512]),
              (_group_rms(zkv[:, 512:640], g128, kg_ref[1:2, :]), zkv[:, 640:768]))
    for i, (ref, (k, v)) in enumerate(zip((cmp_ref, sel_ref, win_ref), halves)):
        if i == 0:
            ref[0] = k
            ref[1] = v
        else:
            ref[:, 0:128] = k
            ref[:, 128:256] = v
        if kv_t_refs:
            kv_t_refs[i][0, 0:128, :] = k.T
            kv_t_refs[i][0, 128:256, :] = v.T
    gate_ref[...] = jax.nn.sigmoid(_dot(a, wg_ref[...]))
    zc = _dot(a, wc_ref[...])
    cb_ref[...] = zc[:, 0:512]
    u_ref[...] = zc[:, 512:1024] * zc[:, 1024:1536]


def _inproj(x, w, seq_len=None):
    n = x.shape[0]
    tm = min(ROW_TILE, n)
    row = lambda c: pl.BlockSpec((tm, c), lambda i: (i, 0))
    full = lambda a: pl.BlockSpec(a.shape, lambda i: (0,) * a.ndim)
    consts = (w['norm_mix'], w['wq'], w['wkv'], w['wg'], w['wc'], w['q_gain'], w['k_gain'], w['g512'], w['g128'])
    out_specs = [row(2 * ATT_WIDTH), pl.BlockSpec((2, tm, KV_WIDTH), lambda i: (0, i, 0)), row(256), row(256),
                 row(128), row(512), row(512)]
    out_shape = ([jax.ShapeDtypeStruct((n, 2 * ATT_WIDTH), _BF), jax.ShapeDtypeStruct((2, n, KV_WIDTH), _F32)]
                 + [jax.ShapeDtypeStruct((n, 256), _F32)] * 2
                 + [jax.ShapeDtypeStruct((n, 128), _F32)] + [jax.ShapeDtypeStruct((n, 512), _F32)] * 2)
    if seq_len is not None:
        nt = seq_len // tm
        lead = nt - WINDOW // tm
        kv_t = pl.BlockSpec((1, 256, tm), lambda i: (i // nt, 0, i % nt))
        win_t = pl.BlockSpec((1, 256, tm), lambda i: (i // nt, 0, jnp.maximum(i % nt - lead, 0)))
        out_specs += [kv_t, kv_t, win_t]
        out_shape += [jax.ShapeDtypeStruct((n // seq_len, 256, seq_len), _F32)] * 2 + [
            jax.ShapeDtypeStruct((n // seq_len, 256, WINDOW), _F32)]
    return pl.pallas_call(
        _inproj_body,
        grid=(n // tm,),
        in_specs=[row(D_MODEL)] + [full(c) for c in consts],
        out_specs=out_specs,
        out_shape=out_shape,
        compiler_params=pltpu.CompilerParams(dimension_semantics=("arbitrary",), vmem_limit_bytes=VMEM_LIMIT),
        name="inproj",
    )(x, *consts)


FF_CHUNK = 1024


def _tail_body(halo, h_ref, o_ref, cb_ref, u_ref, up_ref, p_ref, cw_ref, on_ref, nmlp_ref, nple_ref,
               wout_hbm, wup_hbm, wdn_hbm, wgate_hbm, wproj_hbm, y_ref,
               uext_ref, wout_ref, wup_ref, wdn_ref, wgate_ref, wproj_ref, wsem):
    @pl.when(pl.program_id(0) == 0)
    def _load_weights():
        copies = [pltpu.make_async_copy(src, dst, wsem.at[i]) for i, (src, dst) in enumerate(
            ((wout_hbm, wout_ref), (wup_hbm, wup_ref), (wdn_hbm, wdn_ref), (wgate_hbm, wgate_ref),
             (wproj_hbm, wproj_ref)))]
        for c in copies:
            c.start()
        for c in copies:
            c.wait()

    tm = h_ref.shape[0]
    u = u_ref[...]
    if halo:
        first = (pl.program_id(0) % halo) == 0
        prev = jnp.where(first, 0.0, up_ref[...])
        uext_ref[0:8, :] = prev
        uext_ref[8:tm + 8, :] = u
        u2 = uext_ref[6:tm + 6, :]
        u1 = uext_ref[7:tm + 7, :]
    else:
        u2 = up_ref[0]
        u1 = up_ref[1]
    yc = cw_ref[0:1, :] * u2 + cw_ref[1:2, :] * u1 + cw_ref[2:3, :] * u
    mix_a = _rms_rows(o_ref[...], on_ref[:, 0:ATT_WIDTH]).astype(_BF)
    mix_c = _rms_rows(cb_ref[...] * yc, on_ref[:, ATT_WIDTH:]).astype(_BF)
    h = h_ref[...] + _dot(mix_a, wout_ref[0:ATT_WIDTH, :]) + _dot(mix_c, wout_ref[ATT_WIDTH:, :])
    a = _rms_rows(h, nmlp_ref[...]).astype(_BF)
    y_ref[...] = h
    for c in range(D_FF // FF_CHUNK):
        t = jnp.maximum(_dot(a, wup_ref[:, c * FF_CHUNK:(c + 1) * FF_CHUNK]), 0.0)
        y_ref[...] += _dot((t * t).astype(_BF), wdn_ref[c * FF_CHUNK:(c + 1) * FF_CHUNK, :])
    h = y_ref[...]
    a = _rms_rows(h, nple_ref[...]).astype(_BF)
    gate = jax.nn.sigmoid(_dot(a, wgate_ref[...]))
    y_ref[...] = h + gate * _dot(p_ref[...].astype(_BF), wproj_ref[...])


def _tail(h, o_att, cb, u, u_prev, p, w, seq_len):
    n = h.shape[0]
    tm = min(ROW_TILE, n)
    row = lambda c: pl.BlockSpec((tm, c), lambda i: (i, 0))
    const = lambda a: pl.BlockSpec(a.shape, lambda i: (0,) * a.ndim)
    if seq_len is not None:
        halo = seq_len // tm
        up_spec = pl.BlockSpec((8, CONV_DIM), lambda i: (jnp.maximum(i * (tm // 8) - 1, 0), 0))
        up = u
    else:
        halo = 0
        up_spec = pl.BlockSpec((2, tm, CONV_DIM), lambda i: (0, i, 0))
        up = u_prev
    consts = (w['conv_w'], w['out_norm'], w['norm_mlp'], w['norm_ple'])
    mats = (w['w_out'], w['w_up'], w['w_down'], w['w_ple_gate'], w['w_ple_proj'])
    return pl.pallas_call(
        functools.partial(_tail_body, halo),
        grid=(n // tm,),
        in_specs=[row(D_MODEL), row(ATT_WIDTH), row(CONV_DIM), row(CONV_DIM), up_spec, row(PLE_DIM)]
        + [const(c) for c in consts] + [pl.BlockSpec(memory_space=pl.ANY)] * len(mats),
        out_specs=row(D_MODEL),
        out_shape=jax.ShapeDtypeStruct((n, D_MODEL), _F32),
        scratch_shapes=[pltpu.VMEM((tm + 8, CONV_DIM), _F32)] + [pltpu.VMEM(m.shape, _BF) for m in mats]
        + [pltpu.SemaphoreType.DMA((len(mats),))],
        compiler_params=pltpu.CompilerParams(dimension_semantics=("arbitrary",), vmem_limit_bytes=VMEM_LIMIT),
        name="tail",
    )(h, o_att, cb, u, up, p, *consts, *mats)


def _split_heads(x, lane_lo):
    xr = pltpu.roll(x, HEAD_DIM, axis=1)
    zero = jnp.zeros_like(x)
    a = (jnp.where(lane_lo, x, zero), jnp.where(lane_lo, xr, zero))
    b = (jnp.where(lane_lo, zero, xr), jnp.where(lane_lo, zero, x))
    return a, b


def _compress_rows(load, ns, bd_ref, pe_ref):
    acc = [jnp.zeros((2 * ns, LANES), _F32), jnp.zeros((2 * ns, LANES), _F32)]
    for l in range(CMP_BLOCK):
        for plane in range(2):
            x = jnp.concatenate([load(plane, l), load(plane, CMP_BLOCK + l)], axis=0)
            x = x + pe_ref[l, plane:plane + 1, :]
            acc[plane] = acc[plane] + _dot(x.astype(_BF), bd_ref[l, plane])
    return acc


def _compress_prompt_body(slab_ref, bd_ref, pe_ref, kg_ref, g128_ref, kA_ref, kB_ref, vA_ref, vB_ref):
    ns = slab_ref.shape[2] // SEL_BLOCK
    k, v = _compress_rows(lambda plane, t0: slab_ref[plane, 0, pl.ds(t0, ns, stride=SEL_BLOCK), :],
                          ns, bd_ref, pe_ref)
    k = _group_rms(k, g128_ref[...], kg_ref[...])
    lane_lo = lax.broadcasted_iota(jnp.int32, k.shape, 1) < HEAD_DIM
    ka, kb = _split_heads(k, lane_lo)
    va, vb = _split_heads(v, lane_lo)
    pad = jnp.zeros((LANES - 2 * ns, LANES), _BF)
    for g in range(N_KV):
        for ref, val in ((kA_ref, ka[g]), (kB_ref, kb[g]), (vA_ref, va[g]), (vB_ref, vb[g])):
            ref[0, g, 0:2 * ns, :] = val.astype(_BF)
            if 2 * ns < LANES:
                ref[0, g, 2 * ns:, :] = pad


def _compress_prompt(slab, w):
    _, b, t, _ = slab.shape
    assert t % QBLK == 0 and t // CMP_BLOCK <= LANES
    full = lambda a: pl.BlockSpec(a.shape, lambda i: (0,) * a.ndim)
    consts = (w['bd'], w['pe_t'], w['kc_gain'], w['g128'])
    out = jax.ShapeDtypeStruct((b, N_KV, LANES, LANES), _BF)
    return pl.pallas_call(
        _compress_prompt_body,
        grid=(b,),
        in_specs=[pl.BlockSpec((2, 1, t, LANES), lambda i: (0, i, 0, 0))] + [full(c) for c in consts],
        out_specs=[pl.BlockSpec((1, N_KV, LANES, LANES), lambda i: (i, 0, 0, 0))] * 4,
        out_shape=[out] * 4,
        compiler_params=pltpu.CompilerParams(dimension_semantics=("arbitrary",), vmem_limit_bytes=VMEM_LIMIT),
        name="compress_prompt",
    )(slab, *consts)


BUILD_ROWS = 256
QSTEP = 2


def _top_extra(impb, cand, lane_f, n_extra):
    v = jnp.where(cand, impb, -1.0)
    picked = jnp.zeros(impb.shape, _F32)
    for _ in range(n_extra):
        mx = jnp.max(v, axis=-1, keepdims=True)
        first = jnp.min(jnp.where(v == mx, lane_f, 1e9), axis=-1, keepdims=True)
        hit = lane_f == first
        picked = jnp.where(hit, 1.0, picked)
        v = jnp.where(hit, -1.0, v)
    return picked


def _attn_prompt_body(q_ref, ks_ref, kw_ref, gate_ref, kcA_ref, kcB_ref, vcA_ref, vcB_ref, cbias_ref, tiles_ref,
                      eg_ref, o_ref, ksA, ksB, vsA, vsB, kwA, kwB, vwA, vwB, s_scr, w_scr, m_scr, acc_scr):
    step = pl.program_id(1)
    t_len = ks_ref.shape[1]
    n_sel = t_len // SEL_BLOCK

    @pl.when(step == 0)
    def _build():
        def chunk(c, carry):
            r0 = pl.multiple_of(c * BUILD_ROWS, BUILD_ROWS)
            rows = pl.ds(r0, BUILD_ROWS)
            lane = lax.broadcasted_iota(jnp.int32, (BUILD_ROWS, LANES), 1)
            blk = (r0 + lax.broadcasted_iota(jnp.int32, (BUILD_ROWS, LANES), 0)) // SEL_BLOCK
            lane_lo = lane < HEAD_DIM
            oh_hi = jnp.where(lane == blk + HEAD_DIM, 1.0, 0.0)
            oh_lo = jnp.where(lane == blk, 1.0, 0.0)
            one_hi = jnp.where(lane == HEAD_DIM, 1.0, 0.0)
            one_lo = jnp.where(lane == 0, 1.0, 0.0)
            for src, k_a, k_b, v_a, v_b, onehot in ((ks_ref, ksA, ksB, vsA, vsB, True),
                                                    (kw_ref, kwA, kwB, vwA, vwB, False)):
                ka, kb = _split_heads(src[0, rows, 0:128], lane_lo)
                va, vb = _split_heads(src[0, rows, 128:256], lane_lo)
                for g in range(N_KV):
                    if onehot:
                        k_a[g, rows, :] = jnp.where(lane_lo, ka[g], oh_hi).astype(_BF)
                        k_b[g, rows, :] = jnp.where(lane_lo, oh_lo, kb[g]).astype(_BF)
                    else:
                        k_a[g, rows, :] = ka[g].astype(_BF)
                        k_b[g, rows, :] = kb[g].astype(_BF)
                    v_a[g, rows, :] = jnp.where(lane_lo, va[g], one_hi).astype(_BF)
                    v_b[g, rows, :] = jnp.where(lane_lo, one_lo, vb[g]).astype(_BF)
            return carry
        lax.fori_loop(0, t_len // BUILD_ROWS, chunk, 0)

    lane2 = lax.broadcasted_iota(jnp.int32, (2 * QBLK, LANES), 1)
    lo2 = lane2 < HEAD_DIM
    lane1 = lax.broadcasted_iota(jnp.int32, (QBLK, LANES), 1)
    lane1_f = lane1.astype(_F32)
    row1 = lax.broadcasted_iota(jnp.int32, (QBLK, LANES), 0)
    halves = range(QSTEP)
    qbs = [step * QSTEP + h for h in halves]
    qb_last = qbs[-1]
    rows_of = [slice(h * QBLK, (h + 1) * QBLK) for h in halves]
    curs = [(qb * QBLK + row1) // SEL_BLOCK for qb in qbs]

    def normalize(acc, x):
        l = jnp.sum(jnp.where(lane2 == (HEAD_DIM if x == 0 else 0), acc, 0.0), axis=-1, keepdims=True)
        keep = lo2 if x == 0 else jnp.logical_not(lo2)
        return jnp.where(keep, acc / l, 0.0)

    chains = [(h, g, x) for h in halves for g in range(N_KV) for x in range(2)]
    kv_chains = [(g, x) for g in range(N_KV) for x in range(2)]
    o_cmp, q_plain, q_pair, importance = [], [], [], []
    for h in halves:
        for g in range(N_KV):
            qs = jnp.concatenate([q_ref[0, rows_of[h], (2 * g) * LANES:(2 * g + 1) * LANES],
                                  q_ref[0, rows_of[h], (2 * g + 1) * LANES:(2 * g + 2) * LANES]], axis=0)
            zero = jnp.zeros_like(qs)
            q_a = jnp.where(lo2, qs, zero)
            q_b = jnp.where(lo2, zero, qs)
            qs2 = jnp.concatenate([q_ref[0, rows_of[h], ATT_WIDTH + (2 * g) * LANES:ATT_WIDTH + (2 * g + 1) * LANES],
                                   q_ref[0, rows_of[h], ATT_WIDTH + (2 * g + 1) * LANES:ATT_WIDTH + (2 * g + 2) * LANES]],
                                  axis=0)
            q_pair.append(qs2)
            q_plain += [jnp.where(lo2, qs2, zero), jnp.where(lo2, zero, qs2)]

            def cmp_probs(qx, k_ref, x):
                s = _dot_nt(qx, k_ref[0, g]) + cbias_ref[h, g, x]
                m = jnp.max(s, axis=-1, keepdims=True)
                e = jnp.where(s > 0.5 * NEG, jnp.exp(s - m), 0.0)
                l = jnp.sum(e, axis=-1, keepdims=True)
                return e / jnp.where(l > 0.0, l, 1.0)
            p_a = cmp_probs(q_a, kcA_ref, 0)
            p_b = cmp_probs(q_b, kcB_ref, 1)
            o_cmp.append(_dot(p_a.astype(_BF), vcA_ref[0, g]) + _dot(p_b.astype(_BF), vcB_ref[0, g]))
            imp = p_a[0:QBLK] + p_a[QBLK:] + p_b[0:QBLK] + p_b[QBLK:]
            importance.append(imp + pltpu.roll(imp, LANES - n_sel, axis=1))

    two = lambda a: jnp.concatenate([a, a], axis=0)
    cands = jnp.concatenate([two((lane1 >= 1) & (lane1 <= curs[h] - 2)) for h in halves], axis=0)
    picked = _top_extra(jnp.concatenate(importance, axis=0), cands,
                        jnp.concatenate([lane1_f] * (2 * QSTEP), axis=0), TOP_N - 3)
    q_aug = []
    for h in halves:
        cur = curs[h]
        forced = (lane1 == 0) | (lane1 == cur) | (lane1 == cur - 1)
        few = cur <= TOP_N - 1
        for g in range(N_KV):
            i = h * N_KV + g
            chosen = forced | (few & (lane1 <= cur)) | ((picked[i * QBLK:(i + 1) * QBLK] > 0.5) & jnp.logical_not(few))
            sb_lo = jnp.where(chosen | (lane1 >= n_sel), 0.0, NEG)
            sb_hi = pltpu.roll(sb_lo, HEAD_DIM, axis=1)
            q_aug += [jnp.where(lo2, q_pair[i], two(sb_hi).astype(_BF)),
                      jnp.where(lo2, two(sb_lo).astype(_BF), q_pair[i])]

    k_sel, v_sel, k_win, v_win = (ksA, ksB), (vsA, vsB), (kwA, kwB), (vwA, vwB)
    n_win = WINDOW // QBLK + 1

    w_firsts = [jnp.maximum(qb - (n_win - 1), 0) for qb in qbs]
    w_rows = [pl.ds(pl.multiple_of(wf * QBLK, QBLK), n_win * QBLK) for wf in w_firsts]
    m_win = []
    for ch, (h, g, x) in enumerate(chains):
        s = _dot_nt(q_plain[ch], k_win[x][g, w_rows[h], :])
        m = None
        for i in range(n_win):
            idx = qbs[h] - w_firsts[h] - i
            sb = s[:, i * QBLK:(i + 1) * QBLK] + tiles_ref[jnp.where(idx < 0, n_win, idx), g, x]
            w_scr[ch, :, i * QBLK:(i + 1) * QBLK] = sb
            m = sb if m is None else jnp.maximum(m, sb)
        m_win.append(jnp.max(m, axis=-1, keepdims=True))
    o_win = []
    for ch, (h, g, x) in enumerate(chains):
        p = jnp.exp2(w_scr[ch] - m_win[ch]).astype(_BF)
        o_win.append(normalize(_dot(p, v_win[x][g, w_rows[h], :]), x))

    n_quart = 4
    qw = t_len // n_quart
    per_q = qw // QBLK
    m_scr[...] = jnp.full(m_scr.shape, NEG, _F32)
    for qi in range(n_quart):
        @pl.when(qb_last * QBLK >= qi * qw)
        def _scores(qi=qi):
            for g, x in kv_chains:
                chs = [chains.index((h, g, x)) for h in halves]
                s_all = _dot_nt(jnp.concatenate([q_aug[ch] for ch in chs], axis=0),
                                k_sel[x][g, qi * qw:(qi + 1) * qw, :])
                for h, ch in zip(halves, chs):
                    s = s_all[h * 2 * QBLK:(h + 1) * 2 * QBLK]
                    m = m_scr[ch]
                    for ci in range(per_q):
                        c = qi * per_q + ci
                        idx = qbs[h] - c
                        sb = s[:, ci * QBLK:(ci + 1) * QBLK] + tiles_ref[jnp.where(idx < 0, n_win, jnp.minimum(idx, 2)), g, x]
                        s_scr[ch, :, c * QBLK:(c + 1) * QBLK] = sb
                        m = jnp.maximum(m, sb)
                    m_scr[ch] = m
    m_sel = [jnp.max(m_scr[ch], axis=-1, keepdims=True) for ch in range(len(chains))]

    acc_scr[...] = jnp.zeros(acc_scr.shape, _F32)
    for qi in range(n_quart):
        @pl.when(qb_last * QBLK >= qi * qw)
        def _weighted(qi=qi):
            for g, x in kv_chains:
                chs = [chains.index((h, g, x)) for h in halves]
                p = jnp.concatenate([jnp.exp2(s_scr[ch, :, qi * qw:(qi + 1) * qw] - m_sel[ch]).astype(_BF)
                                     for ch in chs], axis=0)
                pv = _dot(p, v_sel[x][g, qi * qw:(qi + 1) * qw, :])
                for h, ch in zip(halves, chs):
                    acc_scr[ch] += pv[h * 2 * QBLK:(h + 1) * 2 * QBLK]

    gates = gate_ref[0]
    g_hi = gates.astype(_BF)
    g_lo = (gates - g_hi.astype(_F32)).astype(_BF)
    gexp = _dot(g_hi, eg_ref[...]) + _dot(g_lo, eg_ref[...])
    for h in halves:
        for g in range(N_KV):
            i = h * N_KV + g
            o_c = o_cmp[i]
            o_s = normalize(acc_scr[2 * i], 0) + normalize(acc_scr[2 * i + 1], 1)
            o_w = o_win[2 * i] + o_win[2 * i + 1]
            for pr in range(2):
                rows = slice(pr * QBLK, (pr + 1) * QBLK)
                col = (2 * g + pr) * LANES
                gx = lambda br: gexp[rows_of[h], br * ATT_WIDTH + col:br * ATT_WIDTH + col + LANES]
                o_ref[0, rows_of[h], col:col + LANES] = gx(0) * o_c[rows] + gx(1) * o_s[rows] + gx(2) * o_w[rows]


def _attn_prompt(q, sel, win, gates, kc, cbias, w):
    b, t, _ = sel.shape
    nq = t // QBLK
    assert t // SEL_BLOCK <= CMP_BLOCK and t % (4 * QBLK) == 0 and t >= WINDOW + QBLK and nq % QSTEP == 0
    kcA, kcB, vcA, vcB = kc
    full = lambda a: pl.BlockSpec(a.shape, lambda i, j: (0,) * a.ndim)
    slab = pl.BlockSpec((1, t, 256), lambda i, j: (i, 0, 0))
    kcs = pl.BlockSpec((1, N_KV, LANES, LANES), lambda i, j: (i, 0, 0, 0))
    n_chain = QSTEP * 2 * N_KV
    q_rows = QSTEP * QBLK
    scratch = [pltpu.VMEM((N_KV, t, LANES), _BF)] * 8 + [
        pltpu.VMEM((n_chain, 2 * QBLK, t), _F32), pltpu.VMEM((n_chain, 2 * QBLK, WINDOW + QBLK), _F32),
        pltpu.VMEM((n_chain, 2 * QBLK, LANES), _F32), pltpu.VMEM((n_chain, 2 * QBLK, LANES), _F32)]
    return pl.pallas_call(
        _attn_prompt_body,
        grid=(b, nq // QSTEP),
        in_specs=[pl.BlockSpec((1, q_rows, 2 * ATT_WIDTH), lambda i, j: (i, j, 0)), slab, slab,
                  pl.BlockSpec((1, q_rows, LANES), lambda i, j: (i, j, 0)), kcs, kcs, kcs, kcs,
                  pl.BlockSpec((QSTEP, N_KV, 2, 2 * QBLK, LANES), lambda i, j: (j, 0, 0, 0, 0)),
                  full(w['tiles']), full(w['egate'])],
        out_specs=pl.BlockSpec((1, q_rows, ATT_WIDTH), lambda i, j: (i, j, 0)),
        out_shape=jax.ShapeDtypeStruct((b, t, ATT_WIDTH), _F32),
        scratch_shapes=scratch,
        compiler_params=pltpu.CompilerParams(dimension_semantics=("arbitrary", "arbitrary"),
                                             vmem_limit_bytes=VMEM_LIMIT),
        name="attn_prompt",
    )(q, sel, win, gates, kcA, kcB, vcA, vcB, cbias, w['tiles'], w['egate'])


N_PICK = TOP_N - 3
N_SLOT = TOP_N - 1
PAGE_ROWS = 2 * KV_WIDTH


PAIR_TOKENS = 2 * PAGE_SIZE
SSTEP = 2


def _sample_cmp_body(pt_ref, q_ref, cache_ref, bd_ref, pet_ref, perm_ref, kg_ref, g128_ref, sb_ref, oc_ref, pick_ref,
                     buf, rows_scr, sem):
    b = pl.program_id(0)
    nb = pl.num_programs(0)
    n_pairs = pt_ref.shape[1] // 2
    nc = 8 * n_pairs
    slot = b % 2

    def page_copy(bb, pair, half, sl):
        return pltpu.make_async_copy(cache_ref.at[pt_ref[bb, 2 * pair + half]],
                                     buf.at[sl, pair, :, half * PAGE_SIZE:(half + 1) * PAGE_SIZE], sem.at[sl])

    def for_pages(fn):
        def step(pair, c):
            fn(pair, 0)
            fn(pair, 1)
            return c
        lax.fori_loop(0, n_pairs, step, 0)

    @pl.when(b == 0)
    def _prime():
        for_pages(lambda pair, half: page_copy(0, pair, half, 0).start())

    @pl.when(b + 1 < nb)
    def _prefetch():
        for_pages(lambda pair, half: page_copy(b + 1, pair, half, 1 - slot).start())

    for_pages(lambda pair, half: page_copy(b, pair, half, slot).wait())

    def regroup(pair, c):
        x = (buf[slot, pair] + pet_ref[...]).astype(_BF)
        rows_scr[pair] = _dot_nt(perm_ref[...], x)
        return c
    lax.fori_loop(0, n_pairs, regroup, 0, unroll=16)

    x_all = jnp.concatenate([rows_scr[:, 8 * l:8 * (l + 1), :].reshape(nc, 2 * KV_WIDTH).astype(_BF)
                             for l in range(CMP_BLOCK)], axis=1)
    acc = _dot(x_all, bd_ref[...].reshape(CMP_BLOCK * 2 * KV_WIDTH, 2 * KV_WIDTH))
    k = _group_rms(acc[:, 0:KV_WIDTH], g128_ref[...], kg_ref[...]).astype(_BF)
    v = acc[:, KV_WIDTH:].astype(_BF)
    s = _dot_nt(q_ref[0], k) + sb_ref[...]
    m = jnp.max(s, axis=-1, keepdims=True)
    e = jnp.exp(s - m)
    p = e / jnp.sum(e, axis=-1, keepdims=True)
    oc_ref[0] = _dot(p.astype(_BF), v)
    imp8 = p + pltpu.roll(p, nc - 1, axis=1)
    rows = [imp8[h:h + 1] for h in range(N_HEADS)]
    imp_g = [rows[g * GROUP] + rows[g * GROUP + 1] + rows[g * GROUP + 2] + rows[g * GROUP + 3] for g in range(N_KV)]
    imp = jnp.concatenate(imp_g + [jnp.full((8 - N_KV, nc), -1.0, _F32)], axis=0)
    lane = lax.broadcasted_iota(jnp.int32, (8, nc), 1)
    lane_f = lane.astype(_F32)
    v_c = jnp.where((lane % 2 == 0) & (lane >= 2) & (lane <= nc - 4), imp, -1.0)
    picks = jnp.zeros((8, nc), _F32)
    for i in range(N_PICK):
        mx = jnp.max(v_c, axis=-1, keepdims=True)
        first = jnp.min(jnp.where(v_c == mx, lane_f, 1e9), axis=-1, keepdims=True)
        picks = jnp.where(lane == i, first * 0.5, picks)
        v_c = jnp.where(lane_f == first, -1.0, v_c)
    pick_ref[0] = picks.astype(jnp.int32)


def _sample_cmp(page_table, q8, cache, sbias, w):
    db, n_pages = page_table.shape
    nc = 4 * n_pages
    assert n_pages % 2 == 0 and nc // 2 - 2 >= N_PICK
    full = lambda a: pl.BlockSpec(a.shape, lambda i, pt: (0,) * a.ndim)
    consts = (w['bd_kv'], w['pe_tok'], w['perm'], w['kc_gain'], w['g128'], sbias)
    grid_spec = pltpu.PrefetchScalarGridSpec(
        num_scalar_prefetch=1,
        grid=(db,),
        in_specs=[pl.BlockSpec((1, 8, LANES), lambda i, pt: (i, 0, 0)), pl.BlockSpec(memory_space=pl.ANY)]
        + [full(c) for c in consts],
        out_specs=[pl.BlockSpec((1, 8, LANES), lambda i, pt: (i, 0, 0)),
                   pl.BlockSpec((1, 8, nc), lambda i, pt: (i, 0, 0))],
        scratch_shapes=[pltpu.VMEM((2, n_pages // 2, 2 * KV_WIDTH, PAIR_TOKENS), _F32),
                        pltpu.VMEM((n_pages // 2, PAIR_TOKENS, 2 * KV_WIDTH), _F32),
                        pltpu.SemaphoreType.DMA((2,))],
    )
    return pl.pallas_call(
        _sample_cmp_body,
        grid_spec=grid_spec,
        out_shape=[jax.ShapeDtypeStruct((db, 8, LANES), _F32), jax.ShapeDtypeStruct((db, 8, nc), jnp.int32)],
        compiler_params=pltpu.CompilerParams(dimension_semantics=("arbitrary",), vmem_limit_bytes=VMEM_LIMIT),
        name="sample_cmp",
    )(page_table, q8, cache, *consts)


def _sample_attn_body(pt_ref, pick_ref, q_ref, cache_ref, win_ref, nsel_ref, nwin_ref, ncol_ref, gate_ref, oc_ref,
                      bsel_ref, b0_ref, bwin_ref, o_ref, wout_ref, kb, sem):
    step = pl.program_id(0)
    n_steps = pl.num_programs(0)
    nj = 2 * pt_ref.shape[1]
    wlen = win_ref.shape[2]
    slot = step % 2
    seqs = range(SSTEP)

    def block_of(bb, g, s):
        if s == 0:
            return 0
        if s == N_SLOT - 1:
            return nj - 1
        return pick_ref[bb, g * N_PICK + (s - 1)]

    def tile_copy(st, u, g, s, plane, sl):
        bb = st * SSTEP + u
        page = pt_ref[bb, block_of(bb, g, s) // 2]
        return pltpu.make_async_copy(cache_ref.at[page, plane * N_KV + g],
                                     kb.at[sl, u, g, plane, :, pl.ds(s * PAGE_SIZE, PAGE_SIZE)], sem.at[sl])

    def for_tiles(fn):
        for u in seqs:
            for g in range(N_KV):
                for s in range(N_SLOT):
                    for plane in range(2):
                        fn(u, g, s, plane)

    @pl.when(step == 0)
    def _prime():
        for_tiles(lambda u, g, s, plane: tile_copy(0, u, g, s, plane, 0).start())

    @pl.when(step + 1 < n_steps)
    def _prefetch():
        for_tiles(lambda u, g, s, plane: tile_copy(step + 1, u, g, s, plane, 1 - slot).start())

    head_g = lax.broadcasted_iota(jnp.int32, (8, HEAD_DIM), 0) // GROUP
    own_half = lambda x: jnp.where(head_g == 0, x[:, 0:HEAD_DIM], x[:, HEAD_DIM:])
    b0 = b0_ref[:, 0:1]
    lane_half = lax.broadcasted_iota(jnp.int32, (8, PAGE_SIZE), 1) // SEL_BLOCK

    o_win = []
    for u in seqs:
        q8 = q_ref[u]
        x = win_ref[u]
        lane_w = lax.broadcasted_iota(jnp.int32, x.shape, 1)
        wout_ref[u] = jnp.where(lane_w == wlen - 1, ncol_ref[u], pltpu.roll(x, wlen - 1, axis=1))
        sw = _dot(q8, x[0:KV_WIDTH, :].astype(_BF)) + bwin_ref[...]
        sw_new = jnp.sum(q8.astype(_F32) * nwin_ref[u][:, 0:KV_WIDTH], axis=-1, keepdims=True) + b0
        mw = jnp.maximum(jnp.max(sw, axis=-1, keepdims=True), sw_new)
        ew = jnp.exp(sw - mw)
        ew_new = jnp.exp(sw_new - mw)
        o_w = _dot_nt(ew.astype(_BF), x[KV_WIDTH:, :].astype(_BF)) + ew_new * nwin_ref[u][:, KV_WIDTH:]
        o_win.append(own_half(o_w / (jnp.sum(ew, axis=-1, keepdims=True) + ew_new)))

    for_tiles(lambda u, g, s, plane: tile_copy(step, u, g, s, plane, slot).wait())

    for u in seqs:
        bb = step * SSTEP + u
        q8 = q_ref[u]
        ss_new = jnp.sum(q8.astype(_F32) * nsel_ref[u][:, 0:KV_WIDTH], axis=-1, keepdims=True) + b0
        o_sel = []
        for g in range(N_KV):
            pieces = []
            for s in range(N_SLOT):
                j = jnp.full((8, PAGE_SIZE), block_of(bb, g, s), jnp.int32)
                if s == 0:
                    tab = bsel_ref[0]
                elif s == N_SLOT - 1:
                    tab = bsel_ref[2]
                else:
                    tab = jnp.where(j == nj - 2, bsel_ref[1], bsel_ref[0])
                pieces.append(jnp.where(lane_half == j % 2, tab, NEG))
            ss = (_dot(q8[:, g * HEAD_DIM:(g + 1) * HEAD_DIM], kb[slot, u, g, 0].astype(_BF))
                  + jnp.concatenate(pieces, axis=1))
            ms = jnp.maximum(jnp.max(ss, axis=-1, keepdims=True), ss_new)
            es = jnp.exp(ss - ms)
            es_new = jnp.exp(ss_new - ms)
            v_new = nsel_ref[u][:, KV_WIDTH + g * HEAD_DIM:KV_WIDTH + (g + 1) * HEAD_DIM]
            o_g = _dot_nt(es.astype(_BF), kb[slot, u, g, 1].astype(_BF)) + es_new * v_new
            o_sel.append(o_g / (jnp.sum(es, axis=-1, keepdims=True) + es_new))
        o_s = jnp.where(head_g == 0, o_sel[0], o_sel[1])
        gates = gate_ref[u]
        o_ref[u] = gates[:, 0:1] * own_half(oc_ref[u]) + gates[:, 1:2] * o_s + gates[:, 2:3] * o_win[u]


def _sample_attn(page_table, picks, q8, cache, win, new_sel, new_win, gates8, o_c, tabs):
    db, n_pages = page_table.shape
    wlen = win.shape[2]
    full = lambda a: pl.BlockSpec(a.shape, lambda i, pt, pk: (0,) * a.ndim)
    assert db % SSTEP == 0
    per_b = lambda r, c: pl.BlockSpec((SSTEP, r, c), lambda i, pt, pk: (i, 0, 0))
    bsel, b0, bwin = tabs
    grid_spec = pltpu.PrefetchScalarGridSpec(
        num_scalar_prefetch=2,
        grid=(db // SSTEP,),
        in_specs=[per_b(8, LANES), pl.BlockSpec(memory_space=pl.ANY), per_b(2 * KV_WIDTH, wlen), per_b(1, 256),
                  per_b(1, 256), per_b(2 * KV_WIDTH, 1), per_b(8, LANES), per_b(8, LANES),
                  full(bsel), full(b0), full(bwin)],
        out_specs=[per_b(8, HEAD_DIM), per_b(2 * KV_WIDTH, wlen)],
        scratch_shapes=[pltpu.VMEM((2, SSTEP, N_KV, 2, HEAD_DIM, N_SLOT * PAGE_SIZE), _F32),
                        pltpu.SemaphoreType.DMA((2,))],
    )
    return pl.pallas_call(
        _sample_attn_body,
        grid_spec=grid_spec,
        out_shape=[jax.ShapeDtypeStruct((db, 8, HEAD_DIM), _F32), jax.ShapeDtypeStruct((db, 2 * KV_WIDTH, wlen), _F32)],
        compiler_params=pltpu.CompilerParams(dimension_semantics=("arbitrary",), vmem_limit_bytes=VMEM_LIMIT),
        name="sample_attn",
    )(page_table, picks, q8, cache, win, new_sel, new_win, new_win[:, 0, :, None], gates8, o_c, bsel, b0, bwin)


def _rel_bucket(dist):
    n = np.maximum(np.asarray(dist), 0)
    max_exact = RP_BUCKETS // 2
    nf = np.maximum(n, 1).astype(np.float32)
    large = max_exact + (np.log(nf / np.float32(max_exact)) / np.float32(math.log(RP_MAX_DIST / max_exact))
                         * np.float32(RP_BUCKETS - max_exact)).astype(np.int32)
    large = np.minimum(large, RP_BUCKETS - 1)
    return np.where(n < max_exact, n, large)


def _bias_of(rel_bias, dist):
    onehot = _rel_bucket(dist)[..., None, None] == np.arange(RP_BUCKETS)[:, None]
    return jnp.sum(jnp.where(onehot, rel_bias, 0.0), axis=-2)


def _head_major(x, lead):
    n = x.shape[-2]
    x = x.reshape(lead + (QBLK, n, N_KV, 2, 2))
    nl = len(lead)
    x = jnp.transpose(x, tuple(range(nl)) + (nl + 2, nl + 4, nl + 3, nl, nl + 1))
    return x.reshape(lead + (N_KV, 2, 2 * QBLK, n))


def _prompt_tables(rel_bias, t_len):
    nq = t_len // QBLK
    ns = t_len // SEL_BLOCK
    ti = np.arange(QBLK)[:, None]
    ki = np.arange(QBLK)[None, :]
    n_idx = WINDOW // QBLK + 1
    pad = QBLK - 1
    bvec = _bias_of(rel_bias, np.arange(-pad, t_len + pad))
    wins = jnp.stack([bvec[i * QBLK:i * QBLK + 2 * QBLK - 1][::-1] for i in range(n_idx)])
    skew = jnp.tile(wins, (1, QBLK + 1, 1))[:, :QBLK * 2 * QBLK].reshape(n_idx, QBLK, 2 * QBLK, N_HEADS)
    tiles = skew[:, ::-1, :QBLK]
    dist = np.stack([i * QBLK + ti - ki for i in range(n_idx)])
    valid = dist >= 0
    valid[n_idx - 1] &= dist[n_idx - 1] < WINDOW
    tiles = jnp.where(valid[..., None], tiles, NEG)
    tiles = jnp.concatenate([tiles, jnp.full_like(tiles[:1], NEG)])
    tiles = _head_major(tiles * LOG2E, (n_idx + 1,))
    shift = CMP_BLOCK * ns * 2
    bcmp = _bias_of(rel_bias, np.arange(-shift, t_len))
    cols = [bcmp[shift - (CMP_BLOCK * blk + CMP_BLOCK - 1):][:t_len]
            for blk in list(range(0, 2 * ns, 2)) + list(range(1, 2 * ns, 2))]
    cb = jnp.stack(cols, axis=1)
    cb = jnp.pad(cb, ((0, 0), (0, LANES - 2 * ns), (0, 0)))
    lane = np.arange(LANES)
    blk = np.where(lane < ns, 2 * lane, 2 * (lane - ns) + 1)
    dist_c = np.arange(t_len)[:, None] - (CMP_BLOCK * blk + CMP_BLOCK - 1)[None, :]
    valid_c = (dist_c >= 0) & (lane < 2 * ns)[None, :]
    cb = jnp.where(valid_c[..., None], cb, NEG)
    cb = _head_major(cb.reshape(nq, QBLK, LANES, N_HEADS), (nq,))
    return tiles, cb


def _sample_tables(rel_bias, past_len, wlen):
    blk = np.arange(past_len // CMP_BLOCK)
    sbias = _bias_of(rel_bias, past_len - (CMP_BLOCK * blk + CMP_BLOCK - 1)).T
    pos = np.arange(PAGE_SIZE) % SEL_BLOCK
    bsel = jnp.stack([_bias_of(rel_bias, np.full(PAGE_SIZE, past_len)).T,
                      _bias_of(rel_bias, 2 * SEL_BLOCK - pos).T,
                      _bias_of(rel_bias, SEL_BLOCK - pos).T])
    b0 = _bias_of(rel_bias, np.zeros((LANES,), np.int32)).T
    tok = np.arange(wlen)
    bwin = jnp.where((tok >= 1)[None, :], _bias_of(rel_bias, wlen - tok).T, NEG)
    return sbias, (bsel, b0, bwin)


def _prep(norm_mix, w_in, q_norm, k_norm, cmp_pe, w_cmp, conv_w, out_norm, w_out, norm_mlp, w_up, w_down,
          norm_ple, w_ple_gate, w_ple_proj):
    w_in = w_in[0]
    o_kv = ATT_WIDTH
    o_g = o_kv + 6 * KV_WIDTH
    o_c = o_g + 3 * N_HEADS
    bd = jnp.einsum('gh,plde->lpgdhe', jnp.eye(N_KV, dtype=_F32), w_cmp[0]).reshape(CMP_BLOCK, 2, LANES, LANES)
    pe_t = jnp.transpose(jnp.tile(cmp_pe[0], (1, 1, N_KV)), (1, 0, 2))
    egate = np.zeros((LANES, 3 * ATT_WIDTH), np.float32)
    for br in range(3):
        for h in range(N_HEADS):
            egate[br * N_HEADS + h, br * ATT_WIDTH + h * HEAD_DIM:br * ATT_WIDTH + (h + 1) * HEAD_DIM] = 1.0
    ones = np.ones((HEAD_DIM, HEAD_DIM), np.float32)
    perm = np.zeros((PAIR_TOKENS, PAIR_TOKENS), np.float32)
    tok = np.arange(PAIR_TOKENS)
    perm[(tok % CMP_BLOCK) * (PAIR_TOKENS // CMP_BLOCK) + tok // CMP_BLOCK, tok] = 1.0
    pe_tok = jnp.tile(jnp.transpose(cmp_pe[0], (0, 2, 1)), (1, N_KV, PAIR_TOKENS // CMP_BLOCK))
    pe_tok = pe_tok.reshape(2 * KV_WIDTH, PAIR_TOKENS)
    zero = jnp.zeros_like(bd[:, 0])
    bd_kv = jnp.concatenate([jnp.concatenate([bd[:, 0], zero], axis=2),
                             jnp.concatenate([zero, bd[:, 1]], axis=2)], axis=1)
    return {
        'perm': jnp.asarray(perm, _BF), 'pe_tok': pe_tok, 'bd_kv': bd_kv.astype(_BF),
        'norm_mix': norm_mix[0][None], 'wq': w_in[:, :o_kv].astype(_BF), 'wkv': w_in[:, o_kv:o_g].astype(_BF),
        'wg': jnp.pad(w_in[:, o_g:o_c], ((0, 0), (0, LANES - 3 * N_HEADS))).astype(_BF),
        'wc': w_in[:, o_c:].astype(_BF),
        'q_gain': jnp.tile(q_norm[0], N_HEADS)[None],
        'k_gain': jnp.stack([jnp.tile(k_norm[0, 1], N_KV), jnp.tile(k_norm[0, 2], N_KV)]),
        'kc_gain': jnp.tile(k_norm[0, 0], N_KV)[None],
        'g512': jnp.asarray(np.kron(np.eye(N_HEADS, dtype=np.float32), ones), _BF),
        'g128': jnp.asarray(np.kron(np.eye(N_KV, dtype=np.float32), ones), _BF),
        'bd': bd.astype(_BF), 'pe_t': pe_t,
        'egate': jnp.asarray(egate, _BF),
        'conv_w': conv_w[0], 'out_norm': out_norm[0][None], 'w_out': w_out[0].astype(_BF),
        'norm_mlp': norm_mlp[0][None], 'w_up': w_up[0].astype(_BF), 'w_down': w_down[0].astype(_BF),
        'norm_ple': norm_ple[0][None], 'w_ple_gate': w_ple_gate[0].astype(_BF),
        'w_ple_proj': w_ple_proj[0].astype(_BF),
    }


def kernel(x_prompt, x_sample, p_prompt, p_sample, cache_cmp_kv, cache_sel_kv, state_win_kv, state_conv, page_table, rel_bias, norm_mix, w_in, q_norm, k_norm, cmp_pe, w_cmp, conv_w, out_norm, w_out, norm_mlp, w_up, w_down, norm_ple, w_ple_gate, w_ple_proj):
    bp, tp, _ = x_prompt.shape
    db, ts, _ = x_sample.shape
    assert norm_mix.shape[0] == 1 and ts == 1 and tp >= WINDOW
    n_pages = page_table.shape[1]
    past_len = n_pages * PAGE_SIZE
    wlen = state_win_kv.shape[2]
    assert wlen == WINDOW and past_len >= 4 * SEL_BLOCK
    w = _prep(norm_mix, w_in, q_norm, k_norm, cmp_pe, w_cmp, conv_w, out_norm, w_out, norm_mlp, w_up, w_down,
              norm_ple, w_ple_gate, w_ple_proj)
    w['tiles'], cbias = _prompt_tables(rel_bias, tp)
    sbias, stabs = _sample_tables(rel_bias, past_len, wlen)
    kv6 = lambda a, b, t: a.reshape(1, b, t, 2, N_KV, HEAD_DIM)

    xp = x_prompt.reshape(bp * tp, D_MODEL)
    q, cmp_p, sel_p, win_p, gates, cb, u, cmp_t, sel_t, win_t = _inproj(xp, w, tp)
    seq = lambda a: a.reshape(bp, tp, a.shape[-1])
    kv6_t = lambda a: jnp.transpose(a.reshape(bp, 2, N_KV, HEAD_DIM, a.shape[-1]), (0, 4, 1, 2, 3))[None]
    kc = _compress_prompt(cmp_p.reshape(2, bp, tp, KV_WIDTH), w)
    o_att = _attn_prompt(seq(q), seq(sel_p), seq(win_p), seq(gates), kc, cbias, w)
    y_p = _tail(xp, o_att.reshape(bp * tp, ATT_WIDTH), cb, u, None, p_prompt[0].reshape(bp * tp, PLE_DIM), w, tp)

    xs = x_sample.reshape(db, D_MODEL)
    q_s, cmp_s, sel_s, win_s, gates_s, cb_s, u_s = _inproj(xs, w)
    qh = q_s[:, :ATT_WIDTH].reshape(db, N_KV, GROUP, HEAD_DIM)
    zq = jnp.zeros_like(qh[:, 0])
    q8 = jnp.concatenate([jnp.concatenate([qh[:, 0], zq], axis=-1), jnp.concatenate([zq, qh[:, 1]], axis=-1)], axis=1)
    n_phys = cache_cmp_kv.shape[1]
    token_minor = lambda a: jnp.transpose(a[0], (0, 2, 3, 4, 1))
    o_c, picks = _sample_cmp(page_table, q8, token_minor(cache_cmp_kv).reshape(n_phys, 2 * KV_WIDTH, PAGE_SIZE), sbias, w)
    picks = picks[:, :N_KV, :N_PICK].reshape(db, N_KV * N_PICK)
    gates8 = jnp.pad(jnp.transpose(gates_s[:, :3 * N_HEADS].reshape(db, 3, N_HEADS), (0, 2, 1)),
                     ((0, 0), (0, 0), (0, LANES - 3)))
    o8, win_new = _sample_attn(page_table, picks, q8,
                               token_minor(cache_sel_kv).reshape(n_phys, 2 * N_KV, HEAD_DIM, PAGE_SIZE),
                               token_minor(state_win_kv).reshape(db, 2 * KV_WIDTH, wlen),
                               sel_s[:, None, :], win_s[:, None, :], gates8, o_c, stabs)
    o_att_s = o8.reshape(db, ATT_WIDTH)
    win_new = jnp.transpose(win_new.reshape(db, 2, N_KV, HEAD_DIM, wlen), (0, 4, 1, 2, 3))
    u_prev = jnp.transpose(state_conv[0], (1, 0, 2))
    y_s = _tail(xs, o_att_s, cb_s, u_s, u_prev, p_sample[0].reshape(db, PLE_DIM), w, None)

    return (y_p.reshape(bp, tp, D_MODEL), y_s.reshape(db, 1, D_MODEL),
            kv6_t(cmp_t), kv6_t(sel_t), kv6_t(win_t),
            seq(u)[:, tp - 2:][None],
            kv6(jnp.concatenate([cmp_s[0], cmp_s[1]], axis=-1), db, 1), kv6(sel_s, db, 1), win_new[None],
            jnp.concatenate([state_conv[0][:, 1:], u_s[:, None, :]], axis=1)[None])
```

```python
import functools
import math

import jax
import jax.numpy as jnp
import numpy as np
from jax import lax
from jax.experimental import pallas as pl
from jax.experimental.pallas import tpu as pltpu

D_MODEL = 1024
HEAD_DIM = 64
N_HEADS = 8
N_KV = 2
GROUP = N_HEADS // N_KV
ATT_WIDTH = N_HEADS * HEAD_DIM
KV_WIDTH = N_KV * HEAD_DIM
CONV_DIM = D_MODEL - ATT_WIDTH
PAGE_SIZE = 128
CMP_BLOCK = 32
SEL_BLOCK = 64
TOP_N = 8
WINDOW = 512
RP_BUCKETS = 32
RP_MAX_DIST = 128
D_FF = 4 * D_MODEL
PLE_DIM = 256
SCALE = HEAD_DIM ** -0.5
NEG = -1e30
EPS = 1e-6
LOG2E = 1.4426950408889634

LANES = 128
QBLK = 128
ROW_TILE = 512
VMEM_LIMIT = 56 * 1024 * 1024

_BF = jnp.bfloat16
_F32 = jnp.float32


def _dot(a, b):
    return jnp.dot(a, b, preferred_element_type=_F32)


def _dot_nt(a, b):
    return lax.dot_general(a, b, (((1,), (1,)), ((), ())), preferred_element_type=_F32)


def _rms_rows(x, gain):
    return x * lax.rsqrt(jnp.mean(x * x, axis=-1, keepdims=True) + EPS) * gain


def _group_rms(z, gmat, gain):
    ssq = _dot((z * z).astype(_BF), gmat) * (1.0 / HEAD_DIM)
    return z * lax.rsqrt(ssq + EPS) * gain


def _inproj_body(x_ref, nm_ref, wq_ref, wkv_ref, wg_ref, wc_ref, qg_ref, kg_ref, g512_ref, g128_ref,
                 q_ref, cmp_ref, sel_ref, win_ref, gate_ref, cb_ref, u_ref, *kv_t_refs):
    a = _rms_rows(x_ref[...], nm_ref[...]).astype(_BF)
    zq = _dot(a, wq_ref[...])
    qn = _group_rms(zq, g512_ref[...], qg_ref[...]) * SCALE
    q_ref[:, 0:ATT_WIDTH] = qn.astype(_BF)
    q_ref[:, ATT_WIDTH:] = (qn * LOG2E).astype(_BF)
    zkv = _dot(a, wkv_ref[...])
    g128 = g128_ref[...]
    halves = ((zkv[:, 0:128], zkv[:, 128:256]),
              (_group_rms(zkv[:, 256:384], g128, kg_ref[0:1, :]), zkv[:, 384:512]),
              (_group_rms(zkv[:, 512:640], g128, kg_ref[1:2, :]), zkv[:, 640:768]))
    for i, (ref, (k, v)) in enumerate(zip((cmp_ref, sel_ref, win_ref), halves)):
        if i == 0:
            ref[0] = k
            ref[1] = v
        else:
            ref[:, 0:128] = k
            ref[:, 128:256] = v
        if kv_t_refs:
            kv_t_refs[i][0, 0:128, :] = k.T
            kv_t_refs[i][0, 128:256, :] = v.T
    gate_ref[...] = jax.nn.sigmoid(_dot(a, wg_ref[...]))
    zc = _dot(a, wc_ref[...])
    cb_ref[...] = zc[:, 0:512]
    u_ref[...] = zc[:, 512:1024] * zc[:, 1024:1536]


def _inproj(x, w, seq_len=None):
    n = x.shape[0]
    tm = min(ROW_TILE, n)
    row = lambda c: pl.BlockSpec((tm, c), lambda i: (i, 0))
    full = lambda a: pl.BlockSpec(a.shape, lambda i: (0,) * a.ndim)
    consts = (w['norm_mix'], w['wq'], w['wkv'], w['wg'], w['wc'], w['q_gain'], w['k_gain'], w['g512'], w['g128'])
    out_specs = [row(2 * ATT_WIDTH), pl.BlockSpec((2, tm, KV_WIDTH), lambda i: (0, i, 0)), row(256), row(256),
                 row(128), row(512), row(512)]
    out_shape = ([jax.ShapeDtypeStruct((n, 2 * ATT_WIDTH), _BF), jax.ShapeDtypeStruct((2, n, KV_WIDTH), _F32)]
                 + [jax.ShapeDtypeStruct((n, 256), _F32)] * 2
                 + [jax.ShapeDtypeStruct((n, 128), _F32)] + [jax.ShapeDtypeStruct((n, 512), _F32)] * 2)
    if seq_len is not None:
        nt = seq_len // tm
        lead = nt - WINDOW // tm
        kv_t = pl.BlockSpec((1, 256, tm), lambda i: (i // nt, 0, i % nt))
        win_t = pl.BlockSpec((1, 256, tm), lambda i: (i // nt, 0, jnp.maximum(i % nt - lead, 0)))
        out_specs += [kv_t, kv_t, win_t]
        out_shape += [jax.ShapeDtypeStruct((n // seq_len, 256, seq_len), _F32)] * 2 + [
            jax.ShapeDtypeStruct((n // seq_len, 256, WINDOW), _F32)]
    return pl.pallas_call(
        _inproj_body,
        grid=(n // tm,),
        in_specs=[row(D_MODEL)] + [full(c) for c in consts],
        out_specs=out_specs,
        out_shape=out_shape,
        compiler_params=pltpu.CompilerParams(dimension_semantics=("arbitrary",), vmem_limit_bytes=VMEM_LIMIT),
        name="inproj",
    )(x, *consts)


FF_CHUNK = 1024


def _tail_body(halo, h_ref, o_ref, cb_ref, u_ref, up_ref, p_ref, cw_ref, on_ref, nmlp_ref, nple_ref,
               wout_hbm, wup_hbm, wdn_hbm, wgate_hbm, wproj_hbm, y_ref,
               uext_ref, wout_ref, wup_ref, wdn_ref, wgate_ref, wproj_ref, wsem):
    @pl.when(pl.program_id(0) == 0)
    def _load_weights():
        copies = [pltpu.make_async_copy(src, dst, wsem.at[i]) for i, (src, dst) in enumerate(
            ((wout_hbm, wout_ref), (wup_hbm, wup_ref), (wdn_hbm, wdn_ref), (wgate_hbm, wgate_ref),
             (wproj_hbm, wproj_ref)))]
        for c in copies:
            c.start()
        for c in copies:
            c.wait()

    tm = h_ref.shape[0]
    u = u_ref[...]
    if halo:
        first = (pl.program_id(0) % halo) == 0
        prev = jnp.where(first, 0.0, up_ref[...])
        uext_ref[0:8, :] = prev
        uext_ref[8:tm + 8, :] = u
        u2 = uext_ref[6:tm + 6, :]
        u1 = uext_ref[7:tm + 7, :]
    else:
        u2 = up_ref[0]
        u1 = up_ref[1]
    yc = cw_ref[0:1, :] * u2 + cw_ref[1:2, :] * u1 + cw_ref[2:3, :] * u
    mix_a = _rms_rows(o_ref[...], on_ref[:, 0:ATT_WIDTH]).astype(_BF)
    mix_c = _rms_rows(cb_ref[...] * yc, on_ref[:, ATT_WIDTH:]).astype(_BF)
    h = h_ref[...] + _dot(mix_a, wout_ref[0:ATT_WIDTH, :]) + _dot(mix_c, wout_ref[ATT_WIDTH:, :])
    a = _rms_rows(h, nmlp_ref[...]).astype(_BF)
    y_ref[...] = h
    for c in range(D_FF // FF_CHUNK):
        t = jnp.maximum(_dot(a, wup_ref[:, c * FF_CHUNK:(c + 1) * FF_CHUNK]), 0.0)
        y_ref[...] += _dot((t * t).astype(_BF), wdn_ref[c * FF_CHUNK:(c + 1) * FF_CHUNK, :])
    h = y_ref[...]
    a = _rms_rows(h, nple_ref[...]).astype(_BF)
    gate = jax.nn.sigmoid(_dot(a, wgate_ref[...]))
    y_ref[...] = h + gate * _dot(p_ref[...].astype(_BF), wproj_ref[...])


def _tail(h, o_att, cb, u, u_prev, p, w, seq_len):
    n = h.shape[0]
    tm = min(ROW_TILE, n)
    row = lambda c: pl.BlockSpec((tm, c), lambda i: (i, 0))
    const = lambda a: pl.BlockSpec(a.shape, lambda i: (0,) * a.ndim)
    if seq_len is not None:
        halo = seq_len // tm
        up_spec = pl.BlockSpec((8, CONV_DIM), lambda i: (jnp.maximum(i * (tm // 8) - 1, 0), 0))
        up = u
    else:
        halo = 0
        up_spec = pl.BlockSpec((2, tm, CONV_DIM), lambda i: (0, i, 0))
        up = u_prev
    consts = (w['conv_w'], w['out_norm'], w['norm_mlp'], w['norm_ple'])
    mats = (w['w_out'], w['w_up'], w['w_down'], w['w_ple_gate'], w['w_ple_proj'])
    return pl.pallas_call(
        functools.partial(_tail_body, halo),
        grid=(n // tm,),
        in_specs=[row(D_MODEL), row(ATT_WIDTH), row(CONV_DIM), row(CONV_DIM), up_spec, row(PLE_DIM)]
        + [const(c) for c in consts] + [pl.BlockSpec(memory_space=pl.ANY)] * len(mats),
        out_specs=row(D_MODEL),
        out_shape=jax.ShapeDtypeStruct((n, D_MODEL), _F32),
        scratch_shapes=[pltpu.VMEM((tm + 8, CONV_DIM), _F32)] + [pltpu.VMEM(m.shape, _BF) for m in mats]
        + [pltpu.SemaphoreType.DMA((len(mats),))],
        compiler_params=pltpu.CompilerParams(dimension_semantics=("arbitrary",), vmem_limit_bytes=VMEM_LIMIT),
        name="tail",
    )(h, o_att, cb, u, up, p, *consts, *mats)


def _split_heads(x, lane_lo):
    xr = pltpu.roll(x, HEAD_DIM, axis=1)
    zero = jnp.zeros_like(x)
    a = (jnp.where(lane_lo, x, zero), jnp.where(lane_lo, xr, zero))
    b = (jnp.where(lane_lo, zero, xr), jnp.where(lane_lo, zero, x))
    return a, b


def _compress_rows(load, ns, bd_ref, pe_ref):
    acc = [jnp.zeros((2 * ns, LANES), _F32), jnp.zeros((2 * ns, LANES), _F32)]
    for l in range(CMP_BLOCK):
        for plane in range(2):
            x = jnp.concatenate([load(plane, l), load(plane, CMP_BLOCK + l)], axis=0)
            x = x + pe_ref[l, plane:plane + 1, :]
            acc[plane] = acc[plane] + _dot(x.astype(_BF), bd_ref[l, plane])
    return acc


def _compress_prompt_body(slab_ref, bd_ref, pe_ref, kg_ref, g128_ref, kA_ref, kB_ref, vA_ref, vB_ref):
    ns = slab_ref.shape[2] // SEL_BLOCK
    k, v = _compress_rows(lambda plane, t0: slab_ref[plane, 0, pl.ds(t0, ns, stride=SEL_BLOCK), :],
                          ns, bd_ref, pe_ref)
    k = _group_rms(k, g128_ref[...], kg_ref[...])
    lane_lo = lax.broadcasted_iota(jnp.int32, k.shape, 1) < HEAD_DIM
    ka, kb = _split_heads(k, lane_lo)
    va, vb = _split_heads(v, lane_lo)
    pad = jnp.zeros((LANES - 2 * ns, LANES), _BF)
    for g in range(N_KV):
        for ref, val in ((kA_ref, ka[g]), (kB_ref, kb[g]), (vA_ref, va[g]), (vB_ref, vb[g])):
            ref[0, g, 0:2 * ns, :] = val.astype(_BF)
            if 2 * ns < LANES:
                ref[0, g, 2 * ns:, :] = pad


def _compress_prompt(slab, w):
    _, b, t, _ = slab.shape
    assert t % QBLK == 0 and t // CMP_BLOCK <= LANES
    full = lambda a: pl.BlockSpec(a.shape, lambda i: (0,) * a.ndim)
    consts = (w['bd'], w['pe_t'], w['kc_gain'], w['g128'])
    out = jax.ShapeDtypeStruct((b, N_KV, LANES, LANES), _BF)
    return pl.pallas_call(
        _compress_prompt_body,
        grid=(b,),
        in_specs=[pl.BlockSpec((2, 1, t, LANES), lambda i: (0, i, 0, 0))] + [full(c) for c in consts],
        out_specs=[pl.BlockSpec((1, N_KV, LANES, LANES), lambda i: (i, 0, 0, 0))] * 4,
        out_shape=[out] * 4,
        compiler_params=pltpu.CompilerParams(dimension_semantics=("arbitrary",), vmem_limit_bytes=VMEM_LIMIT),
        name="compress_prompt",
    )(slab, *consts)


BUILD_ROWS = 256
QSTEP = 2


def _top_extra(impb, cand, lane_f, n_extra):
    v = jnp.where(cand, impb, -1.0)
    picked = jnp.zeros(impb.shape, _F32)
    for _ in range(n_extra):
        mx = jnp.max(v, axis=-1, keepdims=True)
        first = jnp.min(jnp.where(v == mx, lane_f, 1e9), axis=-1, keepdims=True)
        hit = lane_f == first
        picked = jnp.where(hit, 1.0, picked)
        v = jnp.where(hit, -1.0, v)
    return picked


def _attn_prompt_body(q_ref, ks_ref, kw_ref, gate_ref, kcA_ref, kcB_ref, vcA_ref, vcB_ref, cbias_ref, tiles_ref,
                      eg_ref, o_ref, ksA, ksB, vsA, vsB, kwA, kwB, vwA, vwB, s_scr, w_scr, m_scr, acc_scr):
    step = pl.program_id(1)
    t_len = ks_ref.shape[1]
    n_sel = t_len // SEL_BLOCK

    @pl.when(step == 0)
    def _build():
        def chunk(c, carry):
            r0 = pl.multiple_of(c * BUILD_ROWS, BUILD_ROWS)
            rows = pl.ds(r0, BUILD_ROWS)
            lane = lax.broadcasted_iota(jnp.int32, (BUILD_ROWS, LANES), 1)
            blk = (r0 + lax.broadcasted_iota(jnp.int32, (BUILD_ROWS, LANES), 0)) // SEL_BLOCK
            lane_lo = lane < HEAD_DIM
            oh_hi = jnp.where(lane == blk + HEAD_DIM, 1.0, 0.0)
            oh_lo = jnp.where(lane == blk, 1.0, 0.0)
            one_hi = jnp.where(lane == HEAD_DIM, 1.0, 0.0)
            one_lo = jnp.where(lane == 0, 1.0, 0.0)
            for src, k_a, k_b, v_a, v_b, onehot in ((ks_ref, ksA, ksB, vsA, vsB, True),
                                                    (kw_ref, kwA, kwB, vwA, vwB, False)):
                ka, kb = _split_heads(src[0, rows, 0:128], lane_lo)
                va, vb = _split_heads(src[0, rows, 128:256], lane_lo)
                for g in range(N_KV):
                    if onehot:
                        k_a[g, rows, :] = jnp.where(lane_lo, ka[g], oh_hi).astype(_BF)
                        k_b[g, rows, :] = jnp.where(lane_lo, oh_lo, kb[g]).astype(_BF)
                    else:
                        k_a[g, rows, :] = ka[g].astype(_BF)
                        k_b[g, rows, :] = kb[g].astype(_BF)
                    v_a[g, rows, :] = jnp.where(lane_lo, va[g], one_hi).astype(_BF)
                    v_b[g, rows, :] = jnp.where(lane_lo, one_lo, vb[g]).astype(_BF)
            return carry
        lax.fori_loop(0, t_len // BUILD_ROWS, chunk, 0)

    lane2 = lax.broadcasted_iota(jnp.int32, (2 * QBLK, LANES), 1)
    lo2 = lane2 < HEAD_DIM
    lane1 = lax.broadcasted_iota(jnp.int32, (QBLK, LANES), 1)
    lane1_f = lane1.astype(_F32)
    row1 = lax.broadcasted_iota(jnp.int32, (QBLK, LANES), 0)
    halves = range(QSTEP)
    qbs = [step * QSTEP + h for h in halves]
    qb_last = qbs[-1]
    rows_of = [slice(h * QBLK, (h + 1) * QBLK) for h in halves]
    curs = [(qb * QBLK + row1) // SEL_BLOCK for qb in qbs]

    def normalize(acc, x):
        l = jnp.sum(jnp.where(lane2 == (HEAD_DIM if x == 0 else 0), acc, 0.0), axis=-1, keepdims=True)
        keep = lo2 if x == 0 else jnp.logical_not(lo2)
        return jnp.where(keep, acc / l, 0.0)

    chains = [(h, g, x) for h in halves for g in range(N_KV) for x in range(2)]
    kv_chains = [(g, x) for g in range(N_KV) for x in range(2)]
    o_cmp, q_plain, q_pair, importance = [], [], [], []
    for h in halves:
        for g in range(N_KV):
            qs = jnp.concatenate([q_ref[0, rows_of[h], (2 * g) * LANES:(2 * g + 1) * LANES],
                                  q_ref[0, rows_of[h], (2 * g + 1) * LANES:(2 * g + 2) * LANES]], axis=0)
            zero = jnp.zeros_like(qs)
            q_a = jnp.where(lo2, qs, zero)
            q_b = jnp.where(lo2, zero, qs)
            qs2 = jnp.concatenate([q_ref[0, rows_of[h], ATT_WIDTH + (2 * g) * LANES:ATT_WIDTH + (2 * g + 1) * LANES],
                                   q_ref[0, rows_of[h], ATT_WIDTH + (2 * g + 1) * LANES:ATT_WIDTH + (2 * g + 2) * LANES]],
                                  axis=0)
            q_pair.append(qs2)
            q_plain += [jnp.where(lo2, qs2, zero), jnp.where(lo2, zero, qs2)]

            def cmp_probs(qx, k_ref, x):
                s = _dot_nt(qx, k_ref[0, g]) + cbias_ref[h, g, x]
                m = jnp.max(s, axis=-1, keepdims=True)
                e = jnp.where(s > 0.5 * NEG, jnp.exp(s - m), 0.0)
                l = jnp.sum(e, axis=-1, keepdims=True)
                return e / jnp.where(l > 0.0, l, 1.0)
            p_a = cmp_probs(q_a, kcA_ref, 0)
            p_b = cmp_probs(q_b, kcB_ref, 1)
            o_cmp.append(_dot(p_a.astype(_BF), vcA_ref[0, g]) + _dot(p_b.astype(_BF), vcB_ref[0, g]))
            imp = p_a[0:QBLK] + p_a[QBLK:] + p_b[0:QBLK] + p_b[QBLK:]
            importance.append(imp + pltpu.roll(imp, LANES - n_sel, axis=1))

    two = lambda a: jnp.concatenate([a, a], axis=0)
    cands = jnp.concatenate([two((lane1 >= 1) & (lane1 <= curs[h] - 2)) for h in halves], axis=0)
    picked = _top_extra(jnp.concatenate(importance, axis=0), cands,
                        jnp.concatenate([lane1_f] * (2 * QSTEP), axis=0), TOP_N - 3)
    q_aug = []
    for h in halves:
        cur = curs[h]
        forced = (lane1 == 0) | (lane1 == cur) | (lane1 == cur - 1)
        few = cur <= TOP_N - 1
        for g in range(N_KV):
            i = h * N_KV + g
            chosen = forced | (few & (lane1 <= cur)) | ((picked[i * QBLK:(i + 1) * QBLK] > 0.5) & jnp.logical_not(few))
            sb_lo = jnp.where(chosen | (lane1 >= n_sel), 0.0, NEG)
            sb_hi = pltpu.roll(sb_lo, HEAD_DIM, axis=1)
            q_aug += [jnp.where(lo2, q_pair[i], two(sb_hi).astype(_BF)),
                      jnp.where(lo2, two(sb_lo).astype(_BF), q_pair[i])]

    k_sel, v_sel, k_win, v_win = (ksA, ksB), (vsA, vsB), (kwA, kwB), (vwA, vwB)
    n_win = WINDOW // QBLK + 1

    w_firsts = [jnp.maximum(qb - (n_win - 1), 0) for qb in qbs]
    w_rows = [pl.ds(pl.multiple_of(wf * QBLK, QBLK), n_win * QBLK) for wf in w_firsts]
    m_win = []
    for ch, (h, g, x) in enumerate(chains):
        s = _dot_nt(q_plain[ch], k_win[x][g, w_rows[h], :])
        m = None
        for i in range(n_win):
            idx = qbs[h] - w_firsts[h] - i
            sb = s[:, i * QBLK:(i + 1) * QBLK] + tiles_ref[jnp.where(idx < 0, n_win, idx), g, x]
            w_scr[ch, :, i * QBLK:(i + 1) * QBLK] = sb
            m = sb if m is None else jnp.maximum(m, sb)
        m_win.append(jnp.max(m, axis=-1, keepdims=True))
    o_win = []
    for ch, (h, g, x) in enumerate(chains):
        p = jnp.exp2(w_scr[ch] - m_win[ch]).astype(_BF)
        o_win.append(normalize(_dot(p, v_win[x][g, w_rows[h], :]), x))

    n_quart = 4
    qw = t_len // n_quart
    per_q = qw // QBLK
    m_scr[...] = jnp.full(m_scr.shape, NEG, _F32)
    for qi in range(n_quart):
        @pl.when(qb_last * QBLK >= qi * qw)
        def _scores(qi=qi):
            for g, x in kv_chains:
                chs = [chains.index((h, g, x)) for h in halves]
                s_all = _dot_nt(jnp.concatenate([q_aug[ch] for ch in chs], axis=0),
                                k_sel[x][g, qi * qw:(qi + 1) * qw, :])
                for h, ch in zip(halves, chs):
                    s = s_all[h * 2 * QBLK:(h + 1) * 2 * QBLK]
                    m = m_scr[ch]
                    for ci in range(per_q):
                        c = qi * per_q + ci
                        idx = qbs[h] - c
                        sb = s[:, ci * QBLK:(ci + 1) * QBLK] + tiles_ref[jnp.where(idx < 0, n_win, jnp.minimum(idx, 2)), g, x]
                        s_scr[ch, :, c * QBLK:(c + 1) * QBLK] = sb
                        m = jnp.maximum(m, sb)
                    m_scr[ch] = m
    m_sel = [jnp.max(m_scr[ch], axis=-1, keepdims=True) for ch in range(len(chains))]

    acc_scr[...] = jnp.zeros(acc_scr.shape, _F32)
    for qi in range(n_quart):
        @pl.when(qb_last * QBLK >= qi * qw)
        def _weighted(qi=qi):
            for g, x in kv_chains:
                chs = [chains.index((h, g, x)) for h in halves]
                p = jnp.concatenate([jnp.exp2(s_scr[ch, :, qi * qw:(qi + 1) * qw] - m_sel[ch]).astype(_BF)
                                     for ch in chs], axis=0)
                pv = _dot(p, v_sel[x][g, qi * qw:(qi + 1) * qw, :])
                for h, ch in zip(halves, chs):
                    acc_scr[ch] += pv[h * 2 * QBLK:(h + 1) * 2 * QBLK]

    gates = gate_ref[0]
    g_hi = gates.astype(_BF)
    g_lo = (gates - g_hi.astype(_F32)).astype(_BF)
    gexp = _dot(g_hi, eg_ref[...]) + _dot(g_lo, eg_ref[...])
    for h in halves:
        for g in range(N_KV):
            i = h * N_KV + g
            o_c = o_cmp[i]
            o_s = normalize(acc_scr[2 * i], 0) + normalize(acc_scr[2 * i + 1], 1)
            o_w = o_win[2 * i] + o_win[2 * i + 1]
            for pr in range(2):
                rows = slice(pr * QBLK, (pr + 1) * QBLK)
                col = (2 * g + pr) * LANES
                gx = lambda br: gexp[rows_of[h], br * ATT_WIDTH + col:br * ATT_WIDTH + col + LANES]
                o_ref[0, rows_of[h], col:col + LANES] = gx(0) * o_c[rows] + gx(1) * o_s[rows] + gx(2) * o_w[rows]


def _attn_prompt(q, sel, win, gates, kc, cbias, w):
    b, t, _ = sel.shape
    nq = t // QBLK
    assert t // SEL_BLOCK <= CMP_BLOCK and t % (4 * QBLK) == 0 and t >= WINDOW + QBLK and nq % QSTEP == 0
    kcA, kcB, vcA, vcB = kc
    full = lambda a: pl.BlockSpec(a.shape, lambda i, j: (0,) * a.ndim)
    slab = pl.BlockSpec((1, t, 256), lambda i, j: (i, 0, 0))
    kcs = pl.BlockSpec((1, N_KV, LANES, LANES), lambda i, j: (i, 0, 0, 0))
    n_chain = QSTEP * 2 * N_KV
    q_rows = QSTEP * QBLK
    scratch = [pltpu.VMEM((N_KV, t, LANES), _BF)] * 8 + [
        pltpu.VMEM((n_chain, 2 * QBLK, t), _F32), pltpu.VMEM((n_chain, 2 * QBLK, WINDOW + QBLK), _F32),
        pltpu.VMEM((n_chain, 2 * QBLK, LANES), _F32), pltpu.VMEM((n_chain, 2 * QBLK, LANES), _F32)]
    return pl.pallas_call(
        _attn_prompt_body,
        grid=(b, nq // QSTEP),
        in_specs=[pl.BlockSpec((1, q_rows, 2 * ATT_WIDTH), lambda i, j: (i, j, 0)), slab, slab,
                  pl.BlockSpec((1, q_rows, LANES), lambda i, j: (i, j, 0)), kcs, kcs, kcs, kcs,
                  pl.BlockSpec((QSTEP, N_KV, 2, 2 * QBLK, LANES), lambda i, j: (j, 0, 0, 0, 0)),
                  full(w['tiles']), full(w['egate'])],
        out_specs=pl.BlockSpec((1, q_rows, ATT_WIDTH), lambda i, j: (i, j, 0)),
        out_shape=jax.ShapeDtypeStruct((b, t, ATT_WIDTH), _F32),
        scratch_shapes=scratch,
        compiler_params=pltpu.CompilerParams(dimension_semantics=("arbitrary", "arbitrary"),
                                             vmem_limit_bytes=VMEM_LIMIT),
        name="attn_prompt",
    )(q, sel, win, gates, kcA, kcB, vcA, vcB, cbias, w['tiles'], w['egate'])


N_PICK = TOP_N - 3
N_SLOT = TOP_N - 1
PAGE_ROWS = 2 * KV_WIDTH


PAIR_TOKENS = 2 * PAGE_SIZE
SSTEP = 2


def _sample_cmp_body(pt_ref, q_ref, cache_ref, bd_ref, pet_ref, perm_ref, kg_ref, g128_ref, sb_ref, oc_ref, pick_ref,
                     buf, rows_scr, sem):
    b = pl.program_id(0)
    nb = pl.num_programs(0)
    n_pairs = pt_ref.shape[1] // 2
    nc = 8 * n_pairs
    slot = b % 2

    def page_copy(bb, pair, half, sl):
        return pltpu.make_async_copy(cache_ref.at[pt_ref[bb, 2 * pair + half]],
                                     buf.at[sl, pair, :, half * PAGE_SIZE:(half + 1) * PAGE_SIZE], sem.at[sl])

    def for_pages(fn):
        def step(pair, c):
            fn(pair, 0)
            fn(pair, 1)
            return c
        lax.fori_loop(0, n_pairs, step, 0, unroll=8)

    @pl.when(b == 0)
    def _prime():
        for_pages(lambda pair, half: page_copy(0, pair, half, 0).start())

    @pl.when(b + 1 < nb)
    def _prefetch():
        for_pages(lambda pair, half: page_copy(b + 1, pair, half, 1 - slot).start())

    for_pages(lambda pair, half: page_copy(b, pair, half, slot).wait())

    def regroup(pair, c):
        x = (buf[slot, pair] + pet_ref[...]).astype(_BF)
        rows_scr[pair] = _dot_nt(perm_ref[...], x)
        return c
    lax.fori_loop(0, n_pairs, regroup, 0, unroll=16)

    x_all = jnp.concatenate([rows_scr[:, 8 * l:8 * (l + 1), :].reshape(nc, 2 * KV_WIDTH).astype(_BF)
                             for l in range(CMP_BLOCK)], axis=1)
    acc = _dot(x_all, bd_ref[...].reshape(CMP_BLOCK * 2 * KV_WIDTH, 2 * KV_WIDTH))
    k = _group_rms(acc[:, 0:KV_WIDTH], g128_ref[...], kg_ref[...]).astype(_BF)
    v = acc[:, KV_WIDTH:].astype(_BF)
    s = _dot_nt(q_ref[0], k) + sb_ref[...]
    m = jnp.max(s, axis=-1, keepdims=True)
    e = jnp.exp(s - m)
    p = e / jnp.sum(e, axis=-1, keepdims=True)
    oc_ref[0] = _dot(p.astype(_BF), v)
    imp8 = p + pltpu.roll(p, nc - 1, axis=1)
    rows = [imp8[h:h + 1] for h in range(N_HEADS)]
    imp_g = [rows[g * GROUP] + rows[g * GROUP + 1] + rows[g * GROUP + 2] + rows[g * GROUP + 3] for g in range(N_KV)]
    imp = jnp.concatenate(imp_g + [jnp.full((8 - N_KV, nc), -1.0, _F32)], axis=0)
    lane = lax.broadcasted_iota(jnp.int32, (8, nc), 1)
    lane_f = lane.astype(_F32)
    v_c = jnp.where((lane % 2 == 0) & (lane >= 2) & (lane <= nc - 4), imp, -1.0)
    picks = jnp.zeros((8, nc), _F32)
    for i in range(N_PICK):
        mx = jnp.max(v_c, axis=-1, keepdims=True)
        first = jnp.min(jnp.where(v_c == mx, lane_f, 1e9), axis=-1, keepdims=True)
        picks = jnp.where(lane == i, first * 0.5, picks)
        v_c = jnp.where(lane_f == first, -1.0, v_c)
    pick_ref[0] = picks.astype(jnp.int32)


def _sample_cmp(page_table, q8, cache, sbias, w):
    db, n_pages = page_table.shape
    nc = 4 * n_pages
    assert n_pages % 2 == 0 and nc // 2 - 2 >= N_PICK
    full = lambda a: pl.BlockSpec(a.shape, lambda i, pt: (0,) * a.ndim)
    consts = (w['bd_kv'], w['pe_tok'], w['perm'], w['kc_gain'], w['g128'], sbias)
    grid_spec = pltpu.PrefetchScalarGridSpec(
        num_scalar_prefetch=1,
        grid=(db,),
        in_specs=[pl.BlockSpec((1, 8, LANES), lambda i, pt: (i, 0, 0)), pl.BlockSpec(memory_space=pl.ANY)]
        + [full(c) for c in consts],
        out_specs=[pl.BlockSpec((1, 8, LANES), lambda i, pt: (i, 0, 0)),
                   pl.BlockSpec((1, 8, nc), lambda i, pt: (i, 0, 0))],
        scratch_shapes=[pltpu.VMEM((2, n_pages // 2, 2 * KV_WIDTH, PAIR_TOKENS), _F32),
                        pltpu.VMEM((n_pages // 2, PAIR_TOKENS, 2 * KV_WIDTH), _F32),
                        pltpu.SemaphoreType.DMA((2,))],
    )
    return pl.pallas_call(
        _sample_cmp_body,
        grid_spec=grid_spec,
        out_shape=[jax.ShapeDtypeStruct((db, 8, LANES), _F32), jax.ShapeDtypeStruct((db, 8, nc), jnp.int32)],
        compiler_params=pltpu.CompilerParams(dimension_semantics=("arbitrary",), vmem_limit_bytes=VMEM_LIMIT),
        name="sample_cmp",
    )(page_table, q8, cache, *consts)


def _sample_attn_body(pt_ref, pick_ref, q_ref, cache_ref, win_ref, nsel_ref, nwin_ref, ncol_ref, gate_ref, oc_ref,
                      bsel_ref, b0_ref, bwin_ref, o_ref, wout_ref, kb, sem):
    step = pl.program_id(0)
    n_steps = pl.num_programs(0)
    nj = 2 * pt_ref.shape[1]
    wlen = win_ref.shape[2]
    slot = step % 2
    seqs = range(SSTEP)

    def block_of(bb, g, s):
        if s == 0:
            return 0
        if s == N_SLOT - 1:
            return nj - 1
        return pick_ref[bb, g * N_PICK + (s - 1)]

    def tile_copy(st, u, g, s, plane, sl):
        bb = st * SSTEP + u
        page = pt_ref[bb, block_of(bb, g, s) // 2]
        return pltpu.make_async_copy(cache_ref.at[page, plane * N_KV + g],
                                     kb.at[sl, u, g, plane, :, pl.ds(s * PAGE_SIZE, PAGE_SIZE)], sem.at[sl])

    def for_tiles(fn):
        for u in seqs:
            for g in range(N_KV):
                for s in range(N_SLOT):
                    for plane in range(2):
                        fn(u, g, s, plane)

    @pl.when(step == 0)
    def _prime():
        for_tiles(lambda u, g, s, plane: tile_copy(0, u, g, s, plane, 0).start())

    @pl.when(step + 1 < n_steps)
    def _prefetch():
        for_tiles(lambda u, g, s, plane: tile_copy(step + 1, u, g, s, plane, 1 - slot).start())

    head_g = lax.broadcasted_iota(jnp.int32, (8, HEAD_DIM), 0) // GROUP
    own_half = lambda x: jnp.where(head_g == 0, x[:, 0:HEAD_DIM], x[:, HEAD_DIM:])
    b0 = b0_ref[:, 0:1]
    lane_half = lax.broadcasted_iota(jnp.int32, (8, PAGE_SIZE), 1) // SEL_BLOCK

    o_win = []
    for u in seqs:
        q8 = q_ref[u]
        x = win_ref[u]
        lane_w = lax.broadcasted_iota(jnp.int32, x.shape, 1)
        wout_ref[u] = jnp.where(lane_w == wlen - 1, ncol_ref[u], pltpu.roll(x, wlen - 1, axis=1))
        sw = _dot(q8, x[0:KV_WIDTH, :].astype(_BF)) + bwin_ref[...]
        sw_new = jnp.sum(q8.astype(_F32) * nwin_ref[u][:, 0:KV_WIDTH], axis=-1, keepdims=True) + b0
        mw = jnp.maximum(jnp.max(sw, axis=-1, keepdims=True), sw_new)
        ew = jnp.exp(sw - mw)
        ew_new = jnp.exp(sw_new - mw)
        o_w = _dot_nt(ew.astype(_BF), x[KV_WIDTH:, :].astype(_BF)) + ew_new * nwin_ref[u][:, KV_WIDTH:]
        o_win.append(own_half(o_w / (jnp.sum(ew, axis=-1, keepdims=True) + ew_new)))

    for_tiles(lambda u, g, s, plane: tile_copy(step, u, g, s, plane, slot).wait())

    for u in seqs:
        bb = step * SSTEP + u
        q8 = q_ref[u]
        ss_new = jnp.sum(q8.astype(_F32) * nsel_ref[u][:, 0:KV_WIDTH], axis=-1, keepdims=True) + b0
        o_sel = []
        for g in range(N_KV):
            pieces = []
            for s in range(N_SLOT):
                j = jnp.full((8, PAGE_SIZE), block_of(bb, g, s), jnp.int32)
                if s == 0:
                    tab = bsel_ref[0]
                elif s == N_SLOT - 1:
                    tab = bsel_ref[2]
                else:
                    tab = jnp.where(j == nj - 2, bsel_ref[1], bsel_ref[0])
                pieces.append(jnp.where(lane_half == j % 2, tab, NEG))
            ss = (_dot(q8[:, g * HEAD_DIM:(g + 1) * HEAD_DIM], kb[slot, u, g, 0].astype(_BF))
                  + jnp.concatenate(pieces, axis=1))
            ms = jnp.maximum(jnp.max(ss, axis=-1, keepdims=True), ss_new)
            es = jnp.exp(ss - ms)
            es_new = jnp.exp(ss_new - ms)
            v_new = nsel_ref[u][:, KV_WIDTH + g * HEAD_DIM:KV_WIDTH + (g + 1) * HEAD_DIM]
            o_g = _dot_nt(es.astype(_BF), kb[slot, u, g, 1].astype(_BF)) + es_new * v_new
            o_sel.append(o_g / (jnp.sum(es, axis=-1, keepdims=True) + es_new))
        o_s = jnp.where(head_g == 0, o_sel[0], o_sel[1])
        gates = gate_ref[u]
        o_ref[u] = gates[:, 0:1] * own_half(oc_ref[u]) + gates[:, 1:2] * o_s + gates[:, 2:3] * o_win[u]


def _sample_attn(page_table, picks, q8, cache, win, new_sel, new_win, gates8, o_c, tabs):
    db, n_pages = page_table.shape
    wlen = win.shape[2]
    full = lambda a: pl.BlockSpec(a.shape, lambda i, pt, pk: (0,) * a.ndim)
    assert db % SSTEP == 0
    per_b = lambda r, c: pl.BlockSpec((SSTEP, r, c), lambda i, pt, pk: (i, 0, 0))
    bsel, b0, bwin = tabs
    grid_spec = pltpu.PrefetchScalarGridSpec(
        num_scalar_prefetch=2,
        grid=(db // SSTEP,),
        in_specs=[per_b(8, LANES), pl.BlockSpec(memory_space=pl.ANY), per_b(2 * KV_WIDTH, wlen), per_b(1, 256),
                  per_b(1, 256), per_b(2 * KV_WIDTH, 1), per_b(8, LANES), per_b(8, LANES),
                  full(bsel), full(b0), full(bwin)],
        out_specs=[per_b(8, HEAD_DIM), per_b(2 * KV_WIDTH, wlen)],
        scratch_shapes=[pltpu.VMEM((2, SSTEP, N_KV, 2, HEAD_DIM, N_SLOT * PAGE_SIZE), _F32),
                        pltpu.SemaphoreType.DMA((2,))],
    )
    return pl.pallas_call(
        _sample_attn_body,
        grid_spec=grid_spec,
        out_shape=[jax.ShapeDtypeStruct((db, 8, HEAD_DIM), _F32), jax.ShapeDtypeStruct((db, 2 * KV_WIDTH, wlen), _F32)],
        compiler_params=pltpu.CompilerParams(dimension_semantics=("arbitrary",), vmem_limit_bytes=VMEM_LIMIT),
        name="sample_attn",
    )(page_table, picks, q8, cache, win, new_sel, new_win, new_win[:, 0, :, None], gates8, o_c, bsel, b0, bwin)


def _rel_bucket(dist):
    n = np.maximum(np.asarray(dist), 0)
    max_exact = RP_BUCKETS // 2
    nf = np.maximum(n, 1).astype(np.float32)
    large = max_exact + (np.log(nf / np.float32(max_exact)) / np.float32(math.log(RP_MAX_DIST / max_exact))
                         * np.float32(RP_BUCKETS - max_exact)).astype(np.int32)
    large = np.minimum(large, RP_BUCKETS - 1)
    return np.where(n < max_exact, n, large)


def _bias_of(rel_bias, dist):
    onehot = _rel_bucket(dist)[..., None, None] == np.arange(RP_BUCKETS)[:, None]
    return jnp.sum(jnp.where(onehot, rel_bias, 0.0), axis=-2)


def _head_major(x, lead):
    n = x.shape[-2]
    x = x.reshape(lead + (QBLK, n, N_KV, 2, 2))
    nl = len(lead)
    x = jnp.transpose(x, tuple(range(nl)) + (nl + 2, nl + 4, nl + 3, nl, nl + 1))
    return x.reshape(lead + (N_KV, 2, 2 * QBLK, n))


def _prompt_tables(rel_bias, t_len):
    nq = t_len // QBLK
    ns = t_len // SEL_BLOCK
    ti = np.arange(QBLK)[:, None]
    ki = np.arange(QBLK)[None, :]
    n_idx = WINDOW // QBLK + 1
    pad = QBLK - 1
    bvec = _bias_of(rel_bias, np.arange(-pad, t_len + pad))
    wins = jnp.stack([bvec[i * QBLK:i * QBLK + 2 * QBLK - 1][::-1] for i in range(n_idx)])
    skew = jnp.tile(wins, (1, QBLK + 1, 1))[:, :QBLK * 2 * QBLK].reshape(n_idx, QBLK, 2 * QBLK, N_HEADS)
    tiles = skew[:, ::-1, :QBLK]
    dist = np.stack([i * QBLK + ti - ki for i in range(n_idx)])
    valid = dist >= 0
    valid[n_idx - 1] &= dist[n_idx - 1] < WINDOW
    tiles = jnp.where(valid[..., None], tiles, NEG)
    tiles = jnp.concatenate([tiles, jnp.full_like(tiles[:1], NEG)])
    tiles = _head_major(tiles * LOG2E, (n_idx + 1,))
    shift = CMP_BLOCK * ns * 2
    bcmp = _bias_of(rel_bias, np.arange(-shift, t_len))
    cols = [bcmp[shift - (CMP_BLOCK * blk + CMP_BLOCK - 1):][:t_len]
            for blk in list(range(0, 2 * ns, 2)) + list(range(1, 2 * ns, 2))]
    cb = jnp.stack(cols, axis=1)
    cb = jnp.pad(cb, ((0, 0), (0, LANES - 2 * ns), (0, 0)))
    lane = np.arange(LANES)
    blk = np.where(lane < ns, 2 * lane, 2 * (lane - ns) + 1)
    dist_c = np.arange(t_len)[:, None] - (CMP_BLOCK * blk + CMP_BLOCK - 1)[None, :]
    valid_c = (dist_c >= 0) & (lane < 2 * ns)[None, :]
    cb = jnp.where(valid_c[..., None], cb, NEG)
    cb = _head_major(cb.reshape(nq, QBLK, LANES, N_HEADS), (nq,))
    return tiles, cb


def _sample_tables(rel_bias, past_len, wlen):
    blk = np.arange(past_len // CMP_BLOCK)
    sbias = _bias_of(rel_bias, past_len - (CMP_BLOCK * blk + CMP_BLOCK - 1)).T
    pos = np.arange(PAGE_SIZE) % SEL_BLOCK
    bsel = jnp.stack([_bias_of(rel_bias, np.full(PAGE_SIZE, past_len)).T,
                      _bias_of(rel_bias, 2 * SEL_BLOCK - pos).T,
                      _bias_of(rel_bias, SEL_BLOCK - pos).T])
    b0 = _bias_of(rel_bias, np.zeros((LANES,), np.int32)).T
    tok = np.arange(wlen)
    bwin = jnp.where((tok >= 1)[None, :], _bias_of(rel_bias, wlen - tok).T, NEG)
    return sbias, (bsel, b0, bwin)


def _prep(norm_mix, w_in, q_norm, k_norm, cmp_pe, w_cmp, conv_w, out_norm, w_out, norm_mlp, w_up, w_down,
          norm_ple, w_ple_gate, w_ple_proj):
    w_in = w_in[0]
    o_kv = ATT_WIDTH
    o_g = o_kv + 6 * KV_WIDTH
    o_c = o_g + 3 * N_HEADS
    bd = jnp.einsum('gh,plde->lpgdhe', jnp.eye(N_KV, dtype=_F32), w_cmp[0]).reshape(CMP_BLOCK, 2, LANES, LANES)
    pe_t = jnp.transpose(jnp.tile(cmp_pe[0], (1, 1, N_KV)), (1, 0, 2))
    egate = np.zeros((LANES, 3 * ATT_WIDTH), np.float32)
    for br in range(3):
        for h in range(N_HEADS):
            egate[br * N_HEADS + h, br * ATT_WIDTH + h * HEAD_DIM:br * ATT_WIDTH + (h + 1) * HEAD_DIM] = 1.0
    ones = np.ones((HEAD_DIM, HEAD_DIM), np.float32)
    perm = np.zeros((PAIR_TOKENS, PAIR_TOKENS), np.float32)
    tok = np.arange(PAIR_TOKENS)
    perm[(tok % CMP_BLOCK) * (PAIR_TOKENS // CMP_BLOCK) + tok // CMP_BLOCK, tok] = 1.0
    pe_tok = jnp.tile(jnp.transpose(cmp_pe[0], (0, 2, 1)), (1, N_KV, PAIR_TOKENS // CMP_BLOCK))
    pe_tok = pe_tok.reshape(2 * KV_WIDTH, PAIR_TOKENS)
    zero = jnp.zeros_like(bd[:, 0])
    bd_kv = jnp.concatenate([jnp.concatenate([bd[:, 0], zero], axis=2),
                             jnp.concatenate([zero, bd[:, 1]], axis=2)], axis=1)
    return {
        'perm': jnp.asarray(perm, _BF), 'pe_tok': pe_tok, 'bd_kv': bd_kv.astype(_BF),
        'norm_mix': norm_mix[0][None], 'wq': w_in[:, :o_kv].astype(_BF), 'wkv': w_in[:, o_kv:o_g].astype(_BF),
        'wg': jnp.pad(w_in[:, o_g:o_c], ((0, 0), (0, LANES - 3 * N_HEADS))).astype(_BF),
        'wc': w_in[:, o_c:].astype(_BF),
        'q_gain': jnp.tile(q_norm[0], N_HEADS)[None],
        'k_gain': jnp.stack([jnp.tile(k_norm[0, 1], N_KV), jnp.tile(k_norm[0, 2], N_KV)]),
        'kc_gain': jnp.tile(k_norm[0, 0], N_KV)[None],
        'g512': jnp.asarray(np.kron(np.eye(N_HEADS, dtype=np.float32), ones), _BF),
        'g128': jnp.asarray(np.kron(np.eye(N_KV, dtype=np.float32), ones), _BF),
        'bd': bd.astype(_BF), 'pe_t': pe_t,
        'egate': jnp.asarray(egate, _BF),
        'conv_w': conv_w[0], 'out_norm': out_norm[0][None], 'w_out': w_out[0].astype(_BF),
        'norm_mlp': norm_mlp[0][None], 'w_up': w_up[0].astype(_BF), 'w_down': w_down[0].astype(_BF),
        'norm_ple': norm_ple[0][None], 'w_ple_gate': w_ple_gate[0].astype(_BF),
        'w_ple_proj': w_ple_proj[0].astype(_BF),
    }


def kernel(x_prompt, x_sample, p_prompt, p_sample, cache_cmp_kv, cache_sel_kv, state_win_kv, state_conv, page_table, rel_bias, norm_mix, w_in, q_norm, k_norm, cmp_pe, w_cmp, conv_w, out_norm, w_out, norm_mlp, w_up, w_down, norm_ple, w_ple_gate, w_ple_proj):
    bp, tp, _ = x_prompt.shape
    db, ts, _ = x_sample.shape
    assert norm_mix.shape[0] == 1 and ts == 1 and tp >= WINDOW
    n_pages = page_table.shape[1]
    past_len = n_pages * PAGE_SIZE
    wlen = state_win_kv.shape[2]
    assert wlen == WINDOW and past_len >= 4 * SEL_BLOCK
    w = _prep(norm_mix, w_in, q_norm, k_norm, cmp_pe, w_cmp, conv_w, out_norm, w_out, norm_mlp, w_up, w_down,
              norm_ple, w_ple_gate, w_ple_proj)
    w['tiles'], cbias = _prompt_tables(rel_bias, tp)
    sbias, stabs = _sample_tables(rel_bias, past_len, wlen)
    kv6 = lambda a, b, t: a.reshape(1, b, t, 2, N_KV, HEAD_DIM)

    xp = x_prompt.reshape(bp * tp, D_MODEL)
    q, cmp_p, sel_p, win_p, gates, cb, u, cmp_t, sel_t, win_t = _inproj(xp, w, tp)
    seq = lambda a: a.reshape(bp, tp, a.shape[-1])
    kv6_t = lambda a: jnp.transpose(a.reshape(bp, 2, N_KV, HEAD_DIM, a.shape[-1]), (0, 4, 1, 2, 3))[None]
    kc = _compress_prompt(cmp_p.reshape(2, bp, tp, KV_WIDTH), w)
    o_att = _attn_prompt(seq(q), seq(sel_p), seq(win_p), seq(gates), kc, cbias, w)
    y_p = _tail(xp, o_att.reshape(bp * tp, ATT_WIDTH), cb, u, None, p_prompt[0].reshape(bp * tp, PLE_DIM), w, tp)

    xs = x_sample.reshape(db, D_MODEL)
    q_s, cmp_s, sel_s, win_s, gates_s, cb_s, u_s = _inproj(xs, w)
    qh = q_s[:, :ATT_WIDTH].reshape(db, N_KV, GROUP, HEAD_DIM)
    zq = jnp.zeros_like(qh[:, 0])
    q8 = jnp.concatenate([jnp.concatenate([qh[:, 0], zq], axis=-1), jnp.concatenate([zq, qh[:, 1]], axis=-1)], axis=1)
    n_phys = cache_cmp_kv.shape[1]
    token_minor = lambda a: jnp.transpose(a[0], (0, 2, 3, 4, 1))
    o_c, picks = _sample_cmp(page_table, q8, token_minor(cache_cmp_kv).reshape(n_phys, 2 * KV_WIDTH, PAGE_SIZE), sbias, w)
    picks = picks[:, :N_KV, :N_PICK].reshape(db, N_KV * N_PICK)
    gates8 = jnp.pad(jnp.transpose(gates_s[:, :3 * N_HEADS].reshape(db, 3, N_HEADS), (0, 2, 1)),
                     ((0, 0), (0, 0), (0, LANES - 3)))
    o8, win_new = _sample_attn(page_table, picks, q8,
                               token_minor(cache_sel_kv).reshape(n_phys, 2 * N_KV, HEAD_DIM, PAGE_SIZE),
                               token_minor(state_win_kv).reshape(db, 2 * KV_WIDTH, wlen),
                               sel_s[:, None, :], win_s[:, None, :], gates8, o_c, stabs)
    o_att_s = o8.reshape(db, ATT_WIDTH)
    win_new = jnp.transpose(win_new.reshape(db, 2, N_KV, HEAD_DIM, wlen), (0, 4, 1, 2, 3))
    u_prev = jnp.transpose(state_conv[0], (1, 0, 2))
    y_s = _tail(xs, o_att_s, cb_s, u_s, u_prev, p_sample[0].reshape(db, PLE_DIM), w, None)

    return (y_p.reshape(bp, tp, D_MODEL), y_s.reshape(db, 1, D_MODEL),
            kv6_t(cmp_t), kv6_t(sel_t), kv6_t(win_t),
            seq(u)[:, tp - 2:][None],
            kv6(jnp.concatenate([cmp_s[0], cmp_s[1]], axis=-1), db, 1), kv6(sel_s, db, 1), win_new[None],
            jnp.concatenate([state_conv[0][:, 1:], u_s[:, None, :]], axis=1)[None])
```
